```python
import jax, jax.numpy as jnp
from jax import lax
import numpy as np

D_MODEL = 2048
BATCH = 4
SEQ = 2048
DEPTH = 1
DEC_BATCH = 128
DEC_SEQ = 1
PAST_LEN = 8192
PAGE_SIZE = 128

N_HEADS = 16
N_KV_HEADS = 4
HEAD_DIM = 64
GQA_GROUP = N_HEADS // N_KV_HEADS
ATTN_WIDTH = N_HEADS * HEAD_DIM
KV_WIDTH = N_KV_HEADS * HEAD_DIM
WINDOW = 128
ATTN_BLOCK = WINDOW
ROT_DIM = HEAD_DIM // 4
ROPE_THETA = 500000.0
CONV_WIDTH = D_MODEL - ATTN_WIDTH
CONV_K = 3
MIX_WIDTH = ATTN_WIDTH + CONV_WIDTH
IN_COLS = ATTN_WIDTH + 2 * KV_WIDTH + 3 * CONV_WIDTH
SPLIT_POINTS = (ATTN_WIDTH, ATTN_WIDTH + KV_WIDTH, ATTN_WIDTH + 2 * KV_WIDTH,
                ATTN_WIDTH + 2 * KV_WIDTH + CONV_WIDTH, ATTN_WIDTH + 2 * KV_WIDTH + 2 * CONV_WIDTH)
N_EXPERTS = 256
TOP_K = 8
N_EXPERT_GROUPS = 8
TOPK_GROUPS = 4
D_EXPERT = 512
D_SHARED = 512
ROUTED_SCALE = 2.5
PLE_DIM = 256
LN_EPS = 1e-5
DEEPNORM_ALPHA = (2 * DEPTH) ** 0.25
DEEPNORM_BETA = (8 * DEPTH) ** -0.25

kernel_name = 'hymba_swa_sink_shortconv_moe_decode_step'


def _layer_norm(x, g, b):
    xf = x.astype(jnp.float32)
    xc = xf - jnp.mean(xf, -1, keepdims=True)
    var = jnp.mean(xc * xc, -1, keepdims=True)
    return (xc * lax.rsqrt(var + LN_EPS) * g.astype(jnp.float32) + b.astype(jnp.float32)).astype(x.dtype)


def _rms_norm(x, g):
    xf = x.astype(jnp.float32)
    return (xf * lax.rsqrt(jnp.mean(xf * xf, -1, keepdims=True) + LN_EPS) * g.astype(jnp.float32)).astype(x.dtype)


def _rope_partial(x, pos):
    half = ROT_DIM // 2
    inv_freq = ROPE_THETA ** (-jnp.arange(half, dtype=jnp.float32) * 2.0 / ROT_DIM)
    ang = pos.astype(jnp.float32)[:, None] * inv_freq[None, :]
    cos = jnp.cos(ang)[:, None, :]
    sin = jnp.sin(ang)[:, None, :]
    xr = x[..., :ROT_DIM].astype(jnp.float32)
    x1, x2 = xr[..., :half], xr[..., half:]
    rot = jnp.concatenate([x1 * cos - x2 * sin, x2 * cos + x1 * sin], -1).astype(x.dtype)
    return jnp.concatenate([rot, x[..., ROT_DIM:]], -1)


def _mixer_inputs(h, w_in, pos):
    n, L, _ = h.shape
    q, k, v, b_gate, c_gate, xt = jnp.split(h @ w_in, SPLIT_POINTS, axis=-1)
    q = _rope_partial(q.reshape(n, L, N_HEADS, HEAD_DIM), pos)
    k = _rope_partial(k.reshape(n, L, N_KV_HEADS, HEAD_DIM), pos)
    v = v.reshape(n, L, N_KV_HEADS, HEAD_DIM)
    return q, k, v, b_gate, c_gate * xt


def _sink_softmax(s, sinks):
    sk = jnp.broadcast_to(sinks.astype(jnp.float32).reshape(N_KV_HEADS, GQA_GROUP, 1, 1), s.shape[:-1] + (1,))
    return jax.nn.softmax(jnp.concatenate([s, sk], -1), axis=-1)[..., :-1]


def _swa_prompt(q, k, v, sinks):
    n, L = q.shape[:2]
    nb = L // ATTN_BLOCK
    qb = q.reshape(n, nb, ATTN_BLOCK, N_KV_HEADS, GQA_GROUP, HEAD_DIM) * HEAD_DIM ** -0.5

    def band(t):
        tb = t.reshape(n, nb, ATTN_BLOCK, N_KV_HEADS, HEAD_DIM)
        prev = jnp.concatenate([jnp.zeros_like(tb[:, :1]), tb[:, :-1]], 1)
        return jnp.concatenate([prev, tb], 2)

    kb, vb = band(k), band(v)
    s = jnp.einsum('bnqkgd,bnskd->bnkgqs', qb, kb, preferred_element_type=jnp.float32)
    qi = jnp.arange(ATTN_BLOCK)[:, None] + ATTN_BLOCK
    sj = jnp.arange(2 * ATTN_BLOCK)[None, :]
    rel = qi - sj
    local = (rel >= 0) & (rel <= WINDOW)
    exists = (jnp.arange(nb)[:, None, None] > 0) | (sj[None] >= ATTN_BLOCK)
    mask = (local[None] & exists)[None, :, None, None]
    p = _sink_softmax(jnp.where(mask, s, -jnp.inf), sinks)
    o = jnp.einsum('bnkgqs,bnskd->bnqkgd', p.astype(v.dtype), vb)
    return o.reshape(n, L, ATTN_WIDTH)


def _swa_decode(q, k, v, cache_k, cache_v, sinks, past_len):
    n, L = q.shape[:2]
    wb = cache_k.shape[1]
    kc = jnp.concatenate([cache_k, k], 1)
    vc = jnp.concatenate([cache_v, v], 1)
    qg = q.reshape(n, L, N_KV_HEADS, GQA_GROUP, HEAD_DIM) * HEAD_DIM ** -0.5
    s = jnp.einsum('bqkgd,bskd->bkgqs', qg, kc, preferred_element_type=jnp.float32)
    qpos = past_len + jnp.arange(L)
    kpos = past_len - wb + jnp.arange(wb + L)
    rel = qpos[:, None] - kpos[None, :]
    mask = (rel >= 0) & (rel <= WINDOW)
    p = _sink_softmax(jnp.where(mask, s, -jnp.inf), sinks)
    o = jnp.einsum('bkgqs,bskd->bqkgd', p.astype(v.dtype), vc).reshape(n, L, ATTN_WIDTH)
    return o, kc[:, -wb:], vc[:, -wb:]


def _short_conv(u_ext, w, L):
    acc = u_ext[:, 0:L] * w[0]
    for j in range(1, CONV_K):
        acc = acc + u_ext[:, j:j + L] * w[j]
    return acc


def _swiglu(x, wg, wu, wd):
    return (jax.nn.silu(x @ wg) * (x @ wu)) @ wd


def _moe_ffn(h, w_router, router_bias, w_gate, w_up, w_down, ws_gate, ws_up, ws_down):
    T, D = h.shape
    scores = jax.nn.sigmoid((h @ w_router).astype(jnp.float32))
    biased = scores + router_bias.astype(jnp.float32)
    per_group = N_EXPERTS // N_EXPERT_GROUPS
    grp_score = lax.top_k(biased.reshape(T, N_EXPERT_GROUPS, per_group), 2)[0].sum(-1)
    _, top_groups = lax.top_k(grp_score, TOPK_GROUPS)
    group_keep = jnp.any(top_groups[:, :, None] == jnp.arange(N_EXPERT_GROUPS)[None, None, :], axis=1)
    expert_keep = jnp.repeat(group_keep, per_group, axis=1)
    _, top_e = lax.top_k(jnp.where(expert_keep, biased, -jnp.inf), TOP_K)
    gate = jnp.take_along_axis(scores, top_e, axis=1)
    gate = gate / jnp.sum(gate, -1, keepdims=True) * ROUTED_SCALE
    A = T * TOP_K
    bm = min(128, max(8, A // N_EXPERTS))
    n_blocks = -(-A // bm) + N_EXPERTS
    P = n_blocks * bm
    flat_e = top_e.reshape(A)
    flat_tok = jnp.arange(A, dtype=jnp.int32) // TOP_K
    flat_gate = gate.reshape(A)
    order = jnp.argsort(flat_e)
    se = flat_e[order]
    counts = jnp.bincount(flat_e, length=N_EXPERTS)
    padded = (counts + bm - 1) // bm * bm
    pad_end = jnp.cumsum(padded)
    pad_start = pad_end - padded
    start = jnp.cumsum(counts) - counts
    dest = pad_start[se] + jnp.arange(A) - start[se]
    row_tok = jnp.full((P,), T, jnp.int32).at[dest].set(flat_tok[order])
    row_gate = jnp.zeros((P,), jnp.float32).at[dest].set(flat_gate[order])
    block_e = jnp.minimum(jnp.searchsorted(pad_end, jnp.arange(n_blocks) * bm, side='right'), N_EXPERTS - 1)
    h_pad = jnp.concatenate([h, jnp.zeros((1, D), h.dtype)], 0)
    xb = h_pad[row_tok].reshape(n_blocks, bm, D)

    def expert_block(args):
        xblk, e = args
        return _swiglu(xblk, w_gate[e], w_up[e], w_down[e])

    yb = lax.map(expert_block, (xb, block_e)).reshape(P, D)
    routed = jax.ops.segment_sum(yb.astype(jnp.float32) * row_gate[:, None], row_tok, num_segments=T + 1)[:T]
    shared = _swiglu(h, ws_gate, ws_up, ws_down).astype(jnp.float32)
    return (routed + shared).astype(h.dtype)


def _finish_layer(h, attn_o, conv_o, p, gn_attn, gn_conv, w_out, ln1_g, ln1_b, w_router, router_bias,
                  w_gate, w_up, w_down, ws_gate, ws_up, ws_down, w_ple_gate, w_ple, ln2_g, ln2_b):
    mixed = jnp.concatenate([_rms_norm(attn_o, gn_attn), _rms_norm(conv_o, gn_conv)], -1) @ w_out
    h1 = _layer_norm(DEEPNORM_ALPHA * h + mixed, ln1_g, ln1_b)
    n, L, D = h1.shape
    ffn = _moe_ffn(h1.reshape(n * L, D), w_router, router_bias, w_gate, w_up, w_down,
                   ws_gate, ws_up, ws_down).reshape(n, L, D)
    ple = jax.nn.sigmoid((h1 @ w_ple_gate).astype(jnp.float32)) * (p @ w_ple).astype(jnp.float32)
    r = (DEEPNORM_ALPHA * h1 + ffn).astype(jnp.float32) + ple
    return _layer_norm(r, ln2_g, ln2_b).astype(h.dtype)


def setup_inputs(seed: int = 0) -> dict:
    key = jax.random.key(seed)
    ks = jax.random.split(key, 32)
    f32 = jnp.float32

    def nrm(k, shape, scale):
        return jax.random.normal(k, shape, f32) * scale

    win_buf = min(WINDOW, PAST_LEN)
    beta = DEEPNORM_BETA
    return {
        'x_prompt': nrm(ks[0], (BATCH, SEQ, D_MODEL), 1.0),
        'x_sample': nrm(ks[1], (DEC_BATCH, DEC_SEQ, D_MODEL), 1.0),
        'p_prompt': nrm(ks[2], (DEPTH, BATCH, SEQ, PLE_DIM), 1.0),
        'p_sample': nrm(ks[3], (DEPTH, DEC_BATCH, DEC_SEQ, PLE_DIM), 1.0),
        'cache_k': nrm(ks[4], (DEPTH, DEC_BATCH, win_buf, N_KV_HEADS, HEAD_DIM), 1.0),
        'cache_v': nrm(ks[5], (DEPTH, DEC_BATCH, win_buf, N_KV_HEADS, HEAD_DIM), 1.0),
        'state_conv': nrm(ks[6], (DEPTH, DEC_BATCH, CONV_K - 1, CONV_WIDTH), 1.0),
        'ln_in_g': 1.0 + nrm(ks[7], (D_MODEL,), 0.02),
        'ln_in_b': nrm(ks[8], (D_MODEL,), 0.02),
        'w_in': nrm(ks[9], (DEPTH, D_MODEL, IN_COLS), D_MODEL ** -0.5),
        'attn_sinks': nrm(ks[10], (DEPTH, N_HEADS), 0.5),
        'conv_w': nrm(ks[11], (DEPTH, CONV_K, CONV_WIDTH), CONV_K ** -0.5),
        'gn_attn': 1.0 + nrm(ks[12], (DEPTH, ATTN_WIDTH), 0.02),
        'gn_conv': 1.0 + nrm(ks[13], (DEPTH, CONV_WIDTH), 0.02),
        'w_out': nrm(ks[14], (DEPTH, MIX_WIDTH, D_MODEL), MIX_WIDTH ** -0.5 * beta),
        'ln1_g': 1.0 + nrm(ks[15], (DEPTH, D_MODEL), 0.02),
        'ln1_b': nrm(ks[16], (DEPTH, D_MODEL), 0.02),
        'w_router': nrm(ks[17], (DEPTH, D_MODEL, N_EXPERTS), D_MODEL ** -0.5),
        'router_bias': nrm(ks[18], (DEPTH, N_EXPERTS), 0.01),
        'w_gate': nrm(ks[19], (DEPTH, N_EXPERTS, D_MODEL, D_EXPERT), D_MODEL ** -0.5),
        'w_up': nrm(ks[20], (DEPTH, N_EXPERTS, D_MODEL, D_EXPERT), D_MODEL ** -0.5),
        'w_down': nrm(ks[21], (DEPTH, N_EXPERTS, D_EXPERT, D_MODEL), D_EXPERT ** -0.5 * beta),
        'ws_gate': nrm(ks[22], (DEPTH, D_MODEL, D_SHARED), D_MODEL ** -0.5),
        'ws_up': nrm(ks[23], (DEPTH, D_MODEL, D_SHARED), D_MODEL ** -0.5),
        'ws_down': nrm(ks[24], (DEPTH, D_SHARED, D_MODEL), D_SHARED ** -0.5 * beta),
        'w_ple_gate': nrm(ks[25], (DEPTH, D_MODEL, D_MODEL), D_MODEL ** -0.5),
        'w_ple': nrm(ks[26], (DEPTH, PLE_DIM, D_MODEL), PLE_DIM ** -0.5),
        'ln2_g': 1.0 + nrm(ks[27], (DEPTH, D_MODEL), 0.02),
        'ln2_b': nrm(ks[28], (DEPTH, D_MODEL), 0.02),
    }


def reference(x_prompt, x_sample, p_prompt, p_sample, cache_k, cache_v, state_conv,
              ln_in_g, ln_in_b, w_in, attn_sinks, conv_w, gn_attn, gn_conv, w_out, ln1_g, ln1_b,
              w_router, router_bias, w_gate, w_up, w_down, ws_gate, ws_up, ws_down,
              w_ple_gate, w_ple, ln2_g, ln2_b):
    win_buf = cache_k.shape[2]
    seq_p = x_prompt.shape[1]
    seq_s = x_sample.shape[1]
    pos_p = jnp.arange(seq_p)
    pos_s = PAST_LEN + jnp.arange(seq_s)
    hp = _layer_norm(x_prompt, ln_in_g, ln_in_b)
    hs = _layer_norm(x_sample, ln_in_g, ln_in_b)
    kp_new, vp_new, cp_new, ks_new, vs_new, cs_new = [], [], [], [], [], []
    for l in range(DEPTH):
        lw = (gn_attn[l], gn_conv[l], w_out[l], ln1_g[l], ln1_b[l], w_router[l], router_bias[l],
              w_gate[l], w_up[l], w_down[l], ws_gate[l], ws_up[l], ws_down[l],
              w_ple_gate[l], w_ple[l], ln2_g[l], ln2_b[l])
        q, k, v, b_gate, u = _mixer_inputs(hp, w_in[l], pos_p)
        attn = _swa_prompt(q, k, v, attn_sinks[l])
        conv = b_gate * _short_conv(jnp.pad(u, ((0, 0), (CONV_K - 1, 0), (0, 0))), conv_w[l], seq_p)
        kp_new.append(k[:, -win_buf:])
        vp_new.append(v[:, -win_buf:])
        cp_new.append(u[:, -(CONV_K - 1):])
        hp = _finish_layer(hp, attn, conv, p_prompt[l], *lw)
        q, k, v, b_gate, u = _mixer_inputs(hs, w_in[l], pos_s)
        attn, k_win, v_win = _swa_decode(q, k, v, cache_k[l], cache_v[l], attn_sinks[l], PAST_LEN)
        u_ext = jnp.concatenate([state_conv[l], u], 1)
        conv = b_gate * _short_conv(u_ext, conv_w[l], seq_s)
        ks_new.append(k_win)
        vs_new.append(v_win)
        cs_new.append(u_ext[:, -(CONV_K - 1):])
        hs = _finish_layer(hs, attn, conv, p_sample[l], *lw)
    return (hp, hs, jnp.stack(kp_new), jnp.stack(vp_new), jnp.stack(cp_new),
            jnp.stack(ks_new), jnp.stack(vs_new), jnp.stack(cs_new))
```

```python
import functools

import numpy as np
import jax
import jax.numpy as jnp
from jax import lax
from jax.experimental import pallas as pl
from jax.experimental.pallas import tpu as pltpu

D_MODEL = 2048
N_HEADS = 16
N_KV_HEADS = 4
HEAD_DIM = 64
GQA_GROUP = N_HEADS // N_KV_HEADS
ATTN_WIDTH = N_HEADS * HEAD_DIM
KV_WIDTH = N_KV_HEADS * HEAD_DIM
WINDOW = 128
ATTN_BLOCK = WINDOW
ROT_DIM = HEAD_DIM // 4
ROPE_THETA = 500000.0
CONV_WIDTH = D_MODEL - ATTN_WIDTH
CONV_K = 3
IN_COLS = ATTN_WIDTH + 2 * KV_WIDTH + 3 * CONV_WIDTH
N_EXPERTS = 256
TOP_K = 8
N_EXPERT_GROUPS = 8
TOPK_GROUPS = 4
D_EXPERT = 512
D_SHARED = 512
ROUTED_SCALE = 2.5
PLE_DIM = 256
LN_EPS = 1e-5
DEPTH = 1
PAST_LEN = 8192
DEEPNORM_ALPHA = (2 * DEPTH) ** 0.25

LANES = 128
COL_B, COL_C, COL_X, COL_Q, COL_KV = 0, CONV_WIDTH, 2 * CONV_WIDTH, 3 * CONV_WIDTH, 3 * CONV_WIDTH + ATTN_WIDTH

ROW_TILE = 640
INPROJ_COL_TILE = 1536
EXPERT_ROWS = 128
SAMPLE_CHUNK = 8
VMEM_LIMIT = 56 * 1024 * 1024

BF16 = jnp.bfloat16
F32 = jnp.float32


def _layer_norm(x, g, b):
    xc = x - jnp.mean(x, -1, keepdims=True)
    var = jnp.mean(xc * xc, -1, keepdims=True)
    return xc * lax.rsqrt(var + LN_EPS) * g + b


def _rms_norm(x, g):
    return x * lax.rsqrt(jnp.mean(x * x, -1, keepdims=True) + LN_EPS) * g


def _rope_lanes(x, cos_t, sin_lo, sin_hi):
    half = ROT_DIM // 2
    ax = x.ndim - 1
    return x * cos_t + pltpu.roll(x, LANES - half, ax) * sin_lo + pltpu.roll(x, half, ax) * sin_hi


def _rope_wide(x, cos_t, sin_lo, sin_hi):
    n = x.shape[-1] // LANES
    return jnp.concatenate(
        [_rope_lanes(x[..., c * LANES:(c + 1) * LANES], cos_t, sin_lo, sin_hi) for c in range(n)], axis=-1)


def _ln_inproj_kernel(x_ref, g_ref, b_ref, w_ref, o_ref, h_scr):
    @pl.when(pl.program_id(1) == 0)
    def _():
        h_scr[...] = _layer_norm(x_ref[...], g_ref[...], b_ref[...]).astype(BF16)

    o_ref[...] = jnp.dot(h_scr[...], w_ref[...], preferred_element_type=F32)


def _ln_inproj(x_all, g, b, w_bf16):
    t = x_all.shape[0]
    return pl.pallas_call(
        _ln_inproj_kernel,
        out_shape=jax.ShapeDtypeStruct((t, IN_COLS), F32),
        grid=(t // ROW_TILE, IN_COLS // INPROJ_COL_TILE),
        in_specs=[
            pl.BlockSpec((ROW_TILE, D_MODEL), lambda i, j: (i, 0)),
            pl.BlockSpec((1, D_MODEL), lambda i, j: (0, 0)),
            pl.BlockSpec((1, D_MODEL), lambda i, j: (0, 0)),
            pl.BlockSpec((D_MODEL, INPROJ_COL_TILE), lambda i, j: (0, j)),
        ],
        out_specs=pl.BlockSpec((ROW_TILE, INPROJ_COL_TILE), lambda i, j: (i, j)),
        scratch_shapes=[pltpu.VMEM((ROW_TILE, D_MODEL), BF16)],
        compiler_params=pltpu.CompilerParams(
            dimension_semantics=("arbitrary", "arbitrary"), vmem_limit_bytes=VMEM_LIMIT),
        name="ln_inproj",
    )(x_all, g, b, w_bf16)


def _prompt_mixer_kernel(sink_ref, b_ref, c_ref, x_ref, q_ref, kv_ref, kvp_ref, cp_ref, xp_ref,
                         rope_ref, ropep_ref, convw_ref, gna_ref, gnc_ref,
                         mixed_ref, kwin_ref, vwin_ref, cstate_ref, *, n_blocks):
    i = pl.program_id(1)
    blk = ATTN_BLOCK
    cos_t, sin_lo, sin_hi = rope_ref[0], rope_ref[1], rope_ref[2]
    qr = _rope_wide(q_ref[...], cos_t, sin_lo, sin_hi) * (HEAD_DIM ** -0.5)
    k_cur = _rope_wide(kv_ref[:, :KV_WIDTH], cos_t, sin_lo, sin_hi)
    k_prev = _rope_wide(kvp_ref[:, :KV_WIDTH], ropep_ref[0], ropep_ref[1], ropep_ref[2])
    v_cur = kv_ref[:, KV_WIDTH:]
    v_prev = kvp_ref[:, KV_WIDTH:]
    k_all = jnp.concatenate([k_prev, k_cur], axis=0).astype(BF16)
    v_all = jnp.concatenate([v_prev, v_cur], axis=0).astype(BF16)

    qi = lax.broadcasted_iota(jnp.int32, (blk, 2 * blk), 0)
    sj = lax.broadcasted_iota(jnp.int32, (blk, 2 * blk), 1)
    rel = qi + blk - sj
    mask = (rel >= 0) & (rel <= WINDOW) & ((sj >= blk) | (i > 0))
    mask = jnp.concatenate([mask] * GQA_GROUP, axis=0)

    heads = []
    for g in range(N_KV_HEADS):
        qg = jnp.concatenate(
            [qr[:, (g * GQA_GROUP + hh) * HEAD_DIM:(g * GQA_GROUP + hh + 1) * HEAD_DIM] for hh in range(GQA_GROUP)],
            axis=0).astype(BF16)
        kg = k_all[:, g * HEAD_DIM:(g + 1) * HEAD_DIM]
        vg = v_all[:, g * HEAD_DIM:(g + 1) * HEAD_DIM]
        s = lax.dot_general(qg, kg, (((1,), (1,)), ((), ())), preferred_element_type=F32)
        s = jnp.where(mask, s, -jnp.inf)
        sink = jnp.concatenate(
            [jnp.full((blk, 1), sink_ref[g * GQA_GROUP + hh], F32) for hh in range(GQA_GROUP)], axis=0)
        m = jnp.maximum(jnp.max(s, -1, keepdims=True), sink)
        e = jnp.exp(s - m)
        p = e / (jnp.sum(e, -1, keepdims=True) + jnp.exp(sink - m))
        og = jnp.dot(p.astype(BF16), vg, preferred_element_type=F32)
        heads += [og[hh * blk:(hh + 1) * blk] for hh in range(GQA_GROUP)]
    attn = jnp.concatenate(heads, axis=-1)
    mixed_ref[:, :ATTN_WIDTH] = _rms_norm(attn, gna_ref[...]).astype(BF16)

    u = c_ref[...] * x_ref[...]
    n_tail = CONV_K - 1
    tail_rows = cp_ref.shape[0]
    row_id = lax.broadcasted_iota(jnp.int32, u.shape, 0)
    acc = u * convw_ref[CONV_K - 1:CONV_K, :]
    for d in range(1, CONV_K):
        ud = pltpu.roll(u, d, 0)
        for r in range(d):
            src = tail_rows - d + r
            up = jnp.where(i > 0, cp_ref[src:src + 1, :] * xp_ref[src:src + 1, :], 0.0)
            ud = jnp.where(row_id == r, up, ud)
        acc = acc + ud * convw_ref[CONV_K - 1 - d:CONV_K - d, :]
    mixed_ref[:, ATTN_WIDTH:] = _rms_norm(b_ref[...] * acc, gnc_ref[...]).astype(BF16)

    @pl.when(i == n_blocks - 1)
    def _():
        kwin_ref[0] = k_cur
        vwin_ref[0] = v_cur
        cstate_ref[0] = u[blk - n_tail:, :]


def _prompt_mixer(proj, sinks, rope_tab, conv_w, gn_attn, gn_conv, n_batch, seq):
    blk = ATTN_BLOCK
    nb = seq // blk
    tail = 8
    wide = CONV_WIDTH // 1
    kvb = COL_KV // (2 * KV_WIDTH)

    def row(b, i, s):
        return b * nb + i

    def prev_row(b, i, s):
        return b * nb + jnp.maximum(i - 1, 0)

    def prev_tail(b, i, s):
        return jnp.maximum((b * nb + i) * (blk // tail) - 1, 0)

    grid_spec = pltpu.PrefetchScalarGridSpec(
        num_scalar_prefetch=1,
        grid=(n_batch, nb),
        in_specs=[
            pl.BlockSpec((blk, wide), lambda b, i, s: (row(b, i, s), COL_B // wide)),
            pl.BlockSpec((blk, wide), lambda b, i, s: (row(b, i, s), COL_C // wide)),
            pl.BlockSpec((blk, wide), lambda b, i, s: (row(b, i, s), COL_X // wide)),
            pl.BlockSpec((blk, ATTN_WIDTH), lambda b, i, s: (row(b, i, s), COL_Q // ATTN_WIDTH)),
            pl.BlockSpec((blk, 2 * KV_WIDTH), lambda b, i, s: (row(b, i, s), kvb)),
            pl.BlockSpec((blk, 2 * KV_WIDTH), lambda b, i, s: (prev_row(b, i, s), kvb)),
            pl.BlockSpec((tail, wide), lambda b, i, s: (prev_tail(b, i, s), COL_C // wide)),
            pl.BlockSpec((tail, wide), lambda b, i, s: (prev_tail(b, i, s), COL_X // wide)),
            pl.BlockSpec((3, blk, LANES), lambda b, i, s: (0, i, 0)),
            pl.BlockSpec((3, blk, LANES), lambda b, i, s: (0, jnp.maximum(i - 1, 0), 0)),
            pl.BlockSpec((CONV_K, CONV_WIDTH), lambda b, i, s: (0, 0)),
            pl.BlockSpec((1, ATTN_WIDTH), lambda b, i, s: (0, 0)),
            pl.BlockSpec((1, CONV_WIDTH), lambda b, i, s: (0, 0)),
        ],
        out_specs=[
            pl.BlockSpec((blk, D_MODEL), lambda b, i, s: (row(b, i, s), 0)),
            pl.BlockSpec((1, blk, KV_WIDTH), lambda b, i, s: (b, 0, 0)),
            pl.BlockSpec((1, blk, KV_WIDTH), lambda b, i, s: (b, 0, 0)),
            pl.BlockSpec((1, CONV_K - 1, CONV_WIDTH), lambda b, i, s: (b, 0, 0)),
        ],
    )
    return pl.pallas_call(
        functools.partial(_prompt_mixer_kernel, n_blocks=nb),
        out_shape=[
            jax.ShapeDtypeStruct((n_batch * seq, D_MODEL), BF16),
            jax.ShapeDtypeStruct((n_batch, blk, KV_WIDTH), F32),
            jax.ShapeDtypeStruct((n_batch, blk, KV_WIDTH), F32),
            jax.ShapeDtypeStruct((n_batch, CONV_K - 1, CONV_WIDTH), F32),
        ],
        grid_spec=grid_spec,
        compiler_params=pltpu.CompilerParams(
            dimension_semantics=("arbitrary", "arbitrary"), vmem_limit_bytes=VMEM_LIMIT),
        name="prompt_mixer",
    )(sinks, proj, proj, proj, proj, proj, proj, proj, proj, rope_tab, rope_tab, conv_w, gn_attn, gn_conv)


def _expand_groups(t):
    ax = t.ndim - 1
    lane = lax.broadcasted_iota(jnp.int32, t.shape[:-1] + (LANES,), ax)
    chunks = []
    for c in range(KV_WIDTH // LANES):
        a = t[..., c * LANES:(c + 1) * LANES]
        r = pltpu.roll(a, HEAD_DIM, ax)
        lo = jnp.where(lane < HEAD_DIM, a, r)
        hi = jnp.where(lane < HEAD_DIM, r, a)
        chunks += [lo] * (GQA_GROUP // 2) + [hi] * (GQA_GROUP // 2)
    return jnp.concatenate(chunks, axis=-1)


def _sample_mixer_kernel(sink_ref, b_ref, c_ref, x_ref, q_ref, kv_ref, ck_ref, cv_ref, st_ref,
                         rope_ref, seg_ref, segt_ref, convw_ref, gna_ref, gnc_ref,
                         mixed_ref, kwin_ref, vwin_ref, cstate_ref):
    nb, wb = ck_ref.shape[0], ck_ref.shape[1]
    cos_t, sin_lo, sin_hi = rope_ref[0:1, :], rope_ref[1:2, :], rope_ref[2:3, :]
    qr = _rope_wide(q_ref[...], cos_t, sin_lo, sin_hi) * (HEAD_DIM ** -0.5)
    k_new = _rope_wide(kv_ref[:, :KV_WIDTH], cos_t, sin_lo, sin_hi)
    v_new = kv_ref[:, KV_WIDTH:]
    ck = ck_ref[...]
    cv = cv_ref[...]

    seg = seg_ref[...]
    prod = _expand_groups(ck) * qr[:, None, :]
    s_old = jnp.dot(prod.reshape(nb * wb, ATTN_WIDTH).astype(BF16), seg,
                    preferred_element_type=F32).reshape(nb, wb, N_HEADS)
    s_new = jnp.dot((_expand_groups(k_new) * qr).astype(BF16), seg, preferred_element_type=F32)
    sink = sink_ref[...]
    m = jnp.maximum(jnp.maximum(jnp.max(s_old, axis=1), s_new), sink)
    e_old = jnp.exp(s_old - m[:, None, :])
    e_new = jnp.exp(s_new - m)
    inv = 1.0 / (jnp.sum(e_old, axis=1) + e_new + jnp.exp(sink - m))
    p_old = (e_old * inv[:, None, :]).astype(BF16)
    p_new = (e_new * inv).astype(BF16)
    segt = segt_ref[...]
    pe_old = jnp.dot(p_old.reshape(nb * wb, N_HEADS), segt, preferred_element_type=F32).reshape(nb, wb, ATTN_WIDTH)
    pe_new = jnp.dot(p_new, segt, preferred_element_type=F32)
    attn = jnp.sum(pe_old * _expand_groups(cv), axis=1) + pe_new * _expand_groups(v_new)
    mixed_ref[:, :ATTN_WIDTH] = _rms_norm(attn, gna_ref[...]).astype(BF16)

    u = c_ref[...] * x_ref[...]
    acc = u * convw_ref[CONV_K - 1:CONV_K, :]
    for j in range(CONV_K - 1):
        acc = acc + st_ref[j] * convw_ref[j:j + 1, :]
    mixed_ref[:, ATTN_WIDTH:] = _rms_norm(b_ref[...] * acc, gnc_ref[...]).astype(BF16)

    kwin_ref[:, 0:wb - 1, :] = ck_ref[:, 1:wb, :]
    vwin_ref[:, 0:wb - 1, :] = cv_ref[:, 1:wb, :]
    for n in range(nb):
        kwin_ref[n, wb - 1:wb, :] = k_new[n:n + 1, :]
        vwin_ref[n, wb - 1:wb, :] = v_new[n:n + 1, :]
    for j in range(CONV_K - 2):
        cstate_ref[j] = st_ref[j + 1]
    cstate_ref[CONV_K - 2] = u


def _sample_mixer(proj, row0, n_dec, sinks, cache_k, cache_v, state_conv, rope_row, conv_w, gn_attn, gn_conv):
    nb = SAMPLE_CHUNK
    wb = cache_k.shape[1]
    r0 = row0 // nb
    wide = CONV_WIDTH
    head_of_lane = np.arange(ATTN_WIDTH) // HEAD_DIM
    seg = jnp.asarray(head_of_lane[:, None] == np.arange(N_HEADS)[None, :], BF16)
    kvb = COL_KV // (2 * KV_WIDTH)
    return pl.pallas_call(
        _sample_mixer_kernel,
        out_shape=[
            jax.ShapeDtypeStruct((n_dec, D_MODEL), BF16),
            jax.ShapeDtypeStruct((n_dec, wb, KV_WIDTH), F32),
            jax.ShapeDtypeStruct((n_dec, wb, KV_WIDTH), F32),
            jax.ShapeDtypeStruct((CONV_K - 1, n_dec, CONV_WIDTH), F32),
        ],
        grid=(n_dec // nb,),
        in_specs=[
            pl.BlockSpec((1, N_HEADS), lambda i: (0, 0)),
            pl.BlockSpec((nb, wide), lambda i: (r0 + i, COL_B // wide)),
            pl.BlockSpec((nb, wide), lambda i: (r0 + i, COL_C // wide)),
            pl.BlockSpec((nb, wide), lambda i: (r0 + i, COL_X // wide)),
            pl.BlockSpec((nb, ATTN_WIDTH), lambda i: (r0 + i, COL_Q // ATTN_WIDTH)),
            pl.BlockSpec((nb, 2 * KV_WIDTH), lambda i: (r0 + i, kvb)),
            pl.BlockSpec((nb, wb, KV_WIDTH), lambda i: (i, 0, 0)),
            pl.BlockSpec((nb, wb, KV_WIDTH), lambda i: (i, 0, 0)),
            pl.BlockSpec((CONV_K - 1, nb, CONV_WIDTH), lambda i: (0, i, 0)),
            pl.BlockSpec((3, LANES), lambda i: (0, 0)),
            pl.BlockSpec((ATTN_WIDTH, N_HEADS), lambda i: (0, 0)),
            pl.BlockSpec((N_HEADS, ATTN_WIDTH), lambda i: (0, 0)),
            pl.BlockSpec((CONV_K, CONV_WIDTH), lambda i: (0, 0)),
            pl.BlockSpec((1, ATTN_WIDTH), lambda i: (0, 0)),
            pl.BlockSpec((1, CONV_WIDTH), lambda i: (0, 0)),
        ],
        out_specs=[
            pl.BlockSpec((nb, D_MODEL), lambda i: (i, 0)),
            pl.BlockSpec((nb, wb, KV_WIDTH), lambda i: (i, 0, 0)),
            pl.BlockSpec((nb, wb, KV_WIDTH), lambda i: (i, 0, 0)),
            pl.BlockSpec((CONV_K - 1, nb, CONV_WIDTH), lambda i: (0, i, 0)),
        ],
        compiler_params=pltpu.CompilerParams(
            dimension_semantics=("arbitrary",), vmem_limit_bytes=VMEM_LIMIT),
        name="sample_mixer",
    )(sinks.reshape(1, N_HEADS), proj, proj, proj, proj, proj, cache_k, cache_v, state_conv,
      rope_row, seg, seg.T, conv_w, gn_attn, gn_conv)


def _outproj_router_kernel(mixed_ref, x_ref, g0_ref, b0_ref, wout_ref, g1_ref, b1_ref, wr_ref,
                           h1_ref, h1b_ref, score_ref):
    h = _layer_norm(x_ref[...], g0_ref[...], b0_ref[...])
    mixed = jnp.dot(mixed_ref[...], wout_ref[...], preferred_element_type=F32)
    h1 = _layer_norm(DEEPNORM_ALPHA * h + mixed, g1_ref[...], b1_ref[...])
    h1_ref[...] = h1
    h1b = h1.astype(BF16)
    h1b_ref[...] = h1b
    score_ref[...] = jax.nn.sigmoid(jnp.dot(h1b, wr_ref[...], preferred_element_type=F32))


def _outproj_router(mixed, x_all, g0, b0, w_out, g1, b1, w_router):
    t = x_all.shape[0]
    tm = ROW_TILE // 2
    const = lambda i: (0, 0)
    resident = functools.partial(pl.BlockSpec, index_map=const, pipeline_mode=pl.Buffered(1))
    return pl.pallas_call(
        _outproj_router_kernel,
        out_shape=[
            jax.ShapeDtypeStruct((t, D_MODEL), F32),
            jax.ShapeDtypeStruct((t, D_MODEL), BF16),
            jax.ShapeDtypeStruct((t, N_EXPERTS), F32),
        ],
        grid=(t // tm,),
        in_specs=[
            pl.BlockSpec((tm, D_MODEL), lambda i: (i, 0)),
            pl.BlockSpec((tm, D_MODEL), lambda i: (i, 0)),
            pl.BlockSpec((1, D_MODEL), const),
            pl.BlockSpec((1, D_MODEL), const),
            resident((D_MODEL, D_MODEL)),
            pl.BlockSpec((1, D_MODEL), const),
            pl.BlockSpec((1, D_MODEL), const),
            resident((D_MODEL, N_EXPERTS)),
        ],
        out_specs=[
            pl.BlockSpec((tm, D_MODEL), lambda i: (i, 0)),
            pl.BlockSpec((tm, D_MODEL), lambda i: (i, 0)),
            pl.BlockSpec((tm, N_EXPERTS), lambda i: (i, 0)),
        ],
        compiler_params=pltpu.CompilerParams(
            dimension_semantics=("arbitrary",), vmem_limit_bytes=VMEM_LIMIT),
        name="outproj_router",
    )(mixed, x_all, g0, b0, w_out, g1, b1, w_router)


def _experts_kernel(be_ref, nused_ref, x_ref, wg_ref, wu_ref, wd_ref, y_ref):
    i = pl.program_id(0)

    @pl.when(i < nused_ref[0])
    def _():
        x = x_ref[...]
        g = jnp.dot(x, wg_ref[0].astype(BF16), preferred_element_type=F32)
        u = jnp.dot(x, wu_ref[0].astype(BF16), preferred_element_type=F32)
        a = (g * jax.nn.sigmoid(g) * u).astype(BF16)
        y_ref[...] = jnp.dot(a, wd_ref[0].astype(BF16), preferred_element_type=F32)

    @pl.when(i >= nused_ref[0])
    def _():
        y_ref[...] = jnp.zeros_like(y_ref)


def _experts(block_e, n_used, x_sorted, w_gate, w_up, w_down):
    p = x_sorted.shape[0]
    bm = EXPERT_ROWS
    n_blocks = p // bm

    def xrow(i, be, nu):
        return (jnp.minimum(i, nu[0] - 1), 0)

    grid_spec = pltpu.PrefetchScalarGridSpec(
        num_scalar_prefetch=2,
        grid=(n_blocks,),
        in_specs=[
            pl.BlockSpec((bm, D_MODEL), xrow),
            pl.BlockSpec((1, D_MODEL, D_EXPERT), lambda i, be, nu: (be[i], 0, 0)),
            pl.BlockSpec((1, D_MODEL, D_EXPERT), lambda i, be, nu: (be[i], 0, 0)),
            pl.BlockSpec((1, D_EXPERT, D_MODEL), lambda i, be, nu: (be[i], 0, 0)),
        ],
        out_specs=pl.BlockSpec((bm, D_MODEL), lambda i, be, nu: (i, 0)),
    )
    return pl.pallas_call(
        _experts_kernel,
        out_shape=jax.ShapeDtypeStruct((p, D_MODEL), F32),
        grid_spec=grid_spec,
        compiler_params=pltpu.CompilerParams(
            dimension_semantics=("arbitrary",), vmem_limit_bytes=VMEM_LIMIT),
        name="experts",
    )(block_e, n_used, x_sorted, w_gate, w_up, w_down)


def _final_kernel(h1_ref, h1b_ref, routed_ref, p_ref, wsg_ref, wsu_ref, wsd_ref, wpg_ref, wp_ref,
                  g2_ref, b2_ref, y_ref):
    hb = h1b_ref[...]
    g = jnp.dot(hb, wsg_ref[...], preferred_element_type=F32)
    u = jnp.dot(hb, wsu_ref[...], preferred_element_type=F32)
    shared = jnp.dot((g * jax.nn.sigmoid(g) * u).astype(BF16), wsd_ref[...], preferred_element_type=F32)
    gate = jax.nn.sigmoid(jnp.dot(hb, wpg_ref[...], preferred_element_type=F32))
    ple = gate * jnp.dot(p_ref[...], wp_ref[...], preferred_element_type=F32)
    r = DEEPNORM_ALPHA * h1_ref[...] + (routed_ref[...] + shared) + ple
    y_ref[...] = _layer_norm(r, g2_ref[...], b2_ref[...])


def _final(h1, h1b, routed, p_all, ws_gate, ws_up, ws_down, w_ple_gate, w_ple, g2, b2):
    t = h1.shape[0]
    tm = ROW_TILE // 2
    const = lambda i: (0, 0)
    resident = functools.partial(pl.BlockSpec, index_map=const, pipeline_mode=pl.Buffered(1))
    return pl.pallas_call(
        _final_kernel,
        out_shape=jax.ShapeDtypeStruct((t, D_MODEL), F32),
        grid=(t // tm,),
        in_specs=[
            pl.BlockSpec((tm, D_MODEL), lambda i: (i, 0)),
            pl.BlockSpec((tm, D_MODEL), lambda i: (i, 0)),
            pl.BlockSpec((tm, D_MODEL), lambda i: (i, 0)),
            pl.BlockSpec((tm, PLE_DIM), lambda i: (i, 0)),
            resident((D_MODEL, D_SHARED)),
            resident((D_MODEL, D_SHARED)),
            resident((D_SHARED, D_MODEL)),
            resident((D_MODEL, D_MODEL)),
            resident((PLE_DIM, D_MODEL)),
            pl.BlockSpec((1, D_MODEL), const),
            pl.BlockSpec((1, D_MODEL), const),
        ],
        out_specs=pl.BlockSpec((tm, D_MODEL), lambda i: (i, 0)),
        compiler_params=pltpu.CompilerParams(
            dimension_semantics=("arbitrary",), vmem_limit_bytes=VMEM_LIMIT),
        name="final",
    )(h1, h1b, routed, p_all, ws_gate, ws_up, ws_down, w_ple_gate, w_ple, g2, b2)


def _rope_tables(pos):
    half = ROT_DIM // 2
    inv_freq = ROPE_THETA ** (-jnp.arange(half, dtype=F32) * 2.0 / ROT_DIM)
    ang = pos.astype(F32)[:, None] * inv_freq[None, :]
    cos, sin = jnp.cos(ang), jnp.sin(ang)
    n = pos.shape[0]
    ones = jnp.ones((n, HEAD_DIM - ROT_DIM), F32)
    zeros = jnp.zeros((n, HEAD_DIM - ROT_DIM), F32)
    zh = jnp.zeros((n, half), F32)
    cos_t = jnp.concatenate([cos, cos, ones], -1)
    sin_lo = jnp.concatenate([-sin, zh, zeros], -1)
    sin_hi = jnp.concatenate([zh, sin, zeros], -1)
    per_head = jnp.stack([cos_t, sin_lo, sin_hi])
    return jnp.concatenate([per_head] * (LANES // HEAD_DIM), -1)


def _route(scores, router_bias):
    t = scores.shape[0]
    biased = scores + router_bias.astype(F32)
    per_group = N_EXPERTS // N_EXPERT_GROUPS
    grp_score = lax.top_k(biased.reshape(t, N_EXPERT_GROUPS, per_group), 2)[0].sum(-1)
    _, top_groups = lax.top_k(grp_score, TOPK_GROUPS)
    group_keep = jnp.any(top_groups[:, :, None] == jnp.arange(N_EXPERT_GROUPS)[None, None, :], axis=1)
    expert_keep = jnp.repeat(group_keep, per_group, axis=1)
    _, top_e = lax.top_k(jnp.where(expert_keep, biased, -jnp.inf), TOP_K)
    gate = jnp.take_along_axis(scores, top_e, axis=1)
    gate = gate / jnp.sum(gate, -1, keepdims=True) * ROUTED_SCALE
    return top_e, gate


def _dispatch(top_e, t):
    a = t * TOP_K
    bm = EXPERT_ROWS
    n_blocks = a // bm + N_EXPERTS
    flat_e = top_e.reshape(a)
    order = jnp.argsort(flat_e)
    se = flat_e[order]
    counts = jnp.bincount(flat_e, length=N_EXPERTS)
    padded = (counts + bm - 1) // bm * bm
    pad_end = jnp.cumsum(padded)
    pad_start = pad_end - padded
    start = jnp.cumsum(counts) - counts
    dest_sorted = pad_start[se] + jnp.arange(a) - start[se]
    row_tok = jnp.zeros((n_blocks * bm,), jnp.int32).at[dest_sorted].set((order // TOP_K).astype(jnp.int32))
    dest = jnp.zeros((a,), jnp.int32).at[order].set(dest_sorted.astype(jnp.int32))
    n_used = (pad_end[-1] // bm).astype(jnp.int32)
    blk_start = jnp.arange(n_blocks) * bm
    block_e = jnp.minimum(jnp.searchsorted(pad_end, blk_start, side='right'), N_EXPERTS - 1)
    last_e = block_e[jnp.maximum(n_used - 1, 0)]
    block_e = jnp.where(jnp.arange(n_blocks) < n_used, block_e, last_e).astype(jnp.int32)
    return row_tok, dest.reshape(t, TOP_K), block_e, n_used.reshape(1)


def kernel(x_prompt, x_sample, p_prompt, p_sample, cache_k, cache_v, state_conv, ln_in_g, ln_in_b, w_in,
           attn_sinks, conv_w, gn_attn, gn_conv, w_out, ln1_g, ln1_b, w_router, router_bias, w_gate, w_up,
           w_down, ws_gate, ws_up, ws_down, w_ple_gate, w_ple, ln2_g, ln2_b):
    n_batch, seq, d = x_prompt.shape
    n_dec, dec_seq, _ = x_sample.shape
    depth = w_in.shape[0]
    wb = cache_k.shape[2]
    assert depth == 1 and dec_seq == 1 and d == D_MODEL
    assert wb == ATTN_BLOCK and seq % ATTN_BLOCK == 0 and n_dec % SAMPLE_CHUNK == 0
    t_p = n_batch * seq
    t = t_p + n_dec
    assert t % ROW_TILE == 0 and t_p % SAMPLE_CHUNK == 0

    x_all = jnp.concatenate([x_prompt.reshape(t_p, d), x_sample.reshape(n_dec, d)], 0)
    p_all = jnp.concatenate([p_prompt[0].reshape(t_p, PLE_DIM), p_sample[0].reshape(n_dec, PLE_DIM)], 0).astype(BF16)
    row = lambda v: v.reshape(1, -1).astype(F32)

    w_in0 = w_in[0]
    qkv_w = ATTN_WIDTH + 2 * KV_WIDTH
    w_in_perm = jnp.concatenate([w_in0[:, qkv_w:], w_in0[:, :qkv_w]], axis=1).astype(BF16)

    proj = _ln_inproj(x_all, row(ln_in_g), row(ln_in_b), w_in_perm)

    rope_p = _rope_tables(jnp.arange(seq))
    rope_s = _rope_tables(PAST_LEN + jnp.arange(dec_seq))[:, 0, :]
    sinks = attn_sinks[0].astype(F32)
    cw, ga, gc = conv_w[0].astype(F32), row(gn_attn[0]), row(gn_conv[0])
    mixed_p, kwin_p, vwin_p, cstate_p = _prompt_mixer(proj, sinks, rope_p, cw, ga, gc, n_batch, seq)
    mixed_s, kwin_s, vwin_s, cstate_s = _sample_mixer(
        proj, t_p, n_dec, sinks, cache_k[0].reshape(n_dec, wb, KV_WIDTH), cache_v[0].reshape(n_dec, wb, KV_WIDTH),
        jnp.swapaxes(state_conv[0], 0, 1), rope_s, cw, ga, gc)
    cstate_s = jnp.swapaxes(cstate_s, 0, 1)
    mixed = jnp.concatenate([mixed_p, mixed_s], 0)

    h1, h1b, scores = _outproj_router(
        mixed, x_all, row(ln_in_g), row(ln_in_b), w_out[0].astype(BF16), row(ln1_g[0]), row(ln1_b[0]),
        w_router[0].astype(BF16))

    top_e, gate = _route(scores, router_bias[0])
    row_tok, dest, block_e, n_used = _dispatch(top_e, t)
    x_sorted = h1b[row_tok]
    yb = _experts(block_e, n_used, x_sorted, w_gate[0], w_up[0], w_down[0])
    routed = jnp.sum(yb[dest] * gate[:, :, None], axis=1)

    y = _final(h1, h1b, routed, p_all, ws_gate[0].astype(BF16), ws_up[0].astype(BF16), ws_down[0].astype(BF16),
               w_ple_gate[0].astype(BF16), w_ple[0].astype(BF16), row(ln2_g[0]), row(ln2_b[0]))

    kv_shape = (1, -1, wb, N_KV_HEADS, HEAD_DIM)
    return (y[:t_p].reshape(n_batch, seq, d), y[t_p:].reshape(n_dec, dec_seq, d),
            kwin_p.reshape(kv_shape), vwin_p.reshape(kv_shape), cstate_p[None],
            kwin_s.reshape(kv_shape), vwin_s.reshape(kv_shape), cstate_s[None])
```

```python
import functools

import numpy as np
import jax
import jax.numpy as jnp
from jax import lax
from jax.experimental import pallas as pl
from jax.experimental.pallas import tpu as pltpu

D_MODEL = 2048
N_HEADS = 16
N_KV_HEADS = 4
HEAD_DIM = 64
GQA_GROUP = N_HEADS // N_KV_HEADS
ATTN_WIDTH = N_HEADS * HEAD_DIM
KV_WIDTH = N_KV_HEADS * HEAD_DIM
WINDOW = 128
ATTN_BLOCK = WINDOW
ROT_DIM = HEAD_DIM // 4
ROPE_THETA = 500000.0
CONV_WIDTH = D_MODEL - ATTN_WIDTH
CONV_K = 3
IN_COLS = ATTN_WIDTH + 2 * KV_WIDTH + 3 * CONV_WIDTH
N_EXPERTS = 256
TOP_K = 8
N_EXPERT_GROUPS = 8
TOPK_GROUPS = 4
D_EXPERT = 512
D_SHARED = 512
ROUTED_SCALE = 2.5
PLE_DIM = 256
LN_EPS = 1e-5
DEPTH = 1
PAST_LEN = 8192
DEEPNORM_ALPHA = (2 * DEPTH) ** 0.25

LANES = 128
SUBLANES = 8
COL_B, COL_C, COL_X, COL_Q, COL_KV = 0, CONV_WIDTH, 2 * CONV_WIDTH, 3 * CONV_WIDTH, 3 * CONV_WIDTH + ATTN_WIDTH

ROW_TILE = 640
INPROJ_COL_TILE = 1536
EXPERT_ROWS = 128
SAMPLE_CHUNK = 8
VMEM_LIMIT = 56 * 1024 * 1024

BF16 = jnp.bfloat16
F32 = jnp.float32


def _layer_norm(x, g, b):
    xc = x - jnp.mean(x, -1, keepdims=True)
    var = jnp.mean(xc * xc, -1, keepdims=True)
    return xc * lax.rsqrt(var + LN_EPS) * g + b


def _rms_norm(x, g):
    return x * lax.rsqrt(jnp.mean(x * x, -1, keepdims=True) + LN_EPS) * g


def _rope_lanes(x, cos_t, sin_lo, sin_hi):
    half = ROT_DIM // 2
    ax = x.ndim - 1
    return x * cos_t + pltpu.roll(x, LANES - half, ax) * sin_lo + pltpu.roll(x, half, ax) * sin_hi


def _rope_wide(x, cos_t, sin_lo, sin_hi):
    n = x.shape[-1] // LANES
    return jnp.concatenate(
        [_rope_lanes(x[..., c * LANES:(c + 1) * LANES], cos_t, sin_lo, sin_hi) for c in range(n)], axis=-1)


def _ln_inproj_kernel(x_ref, g_ref, b_ref, w_ref, o_ref, h_scr):
    @pl.when(pl.program_id(1) == 0)
    def _():
        h_scr[...] = _layer_norm(x_ref[...], g_ref[...], b_ref[...]).astype(BF16)

    o_ref[...] = jnp.dot(h_scr[...], w_ref[...], preferred_element_type=F32)


def _ln_inproj(x_all, g, b, w_bf16):
    t = x_all.shape[0]
    return pl.pallas_call(
        _ln_inproj_kernel,
        out_shape=jax.ShapeDtypeStruct((t, IN_COLS), F32),
        grid=(t // ROW_TILE, IN_COLS // INPROJ_COL_TILE),
        in_specs=[
            pl.BlockSpec((ROW_TILE, D_MODEL), lambda i, j: (i, 0)),
            pl.BlockSpec((1, D_MODEL), lambda i, j: (0, 0)),
            pl.BlockSpec((1, D_MODEL), lambda i, j: (0, 0)),
            pl.BlockSpec((D_MODEL, INPROJ_COL_TILE), lambda i, j: (0, j)),
        ],
        out_specs=pl.BlockSpec((ROW_TILE, INPROJ_COL_TILE), lambda i, j: (i, j)),
        scratch_shapes=[pltpu.VMEM((ROW_TILE, D_MODEL), BF16)],
        compiler_params=pltpu.CompilerParams(
            dimension_semantics=("arbitrary", "arbitrary"), vmem_limit_bytes=VMEM_LIMIT),
        name="ln_inproj",
    )(x_all, g, b, w_bf16)


def _prompt_mixer_kernel(sink_ref, b_ref, c_ref, x_ref, q_ref, kv_ref, kvp_ref, cp_ref, xp_ref,
                         rope_ref, ropep_ref, convw_ref, gna_ref, gnc_ref,
                         mixed_ref, kwin_ref, vwin_ref, cstate_ref, *, n_blocks):
    i = pl.program_id(1)
    blk = ATTN_BLOCK
    cos_t, sin_lo, sin_hi = rope_ref[0], rope_ref[1], rope_ref[2]
    qr = _rope_wide(q_ref[...], cos_t, sin_lo, sin_hi) * (HEAD_DIM ** -0.5)
    k_cur = _rope_wide(kv_ref[:, :KV_WIDTH], cos_t, sin_lo, sin_hi)
    k_prev = _rope_wide(kvp_ref[:, :KV_WIDTH], ropep_ref[0], ropep_ref[1], ropep_ref[2])
    v_cur = kv_ref[:, KV_WIDTH:]
    v_prev = kvp_ref[:, KV_WIDTH:]
    k_all = jnp.concatenate([k_prev, k_cur], axis=0).astype(BF16)
    v_all = jnp.concatenate([v_prev, v_cur], axis=0).astype(BF16)

    qi = lax.broadcasted_iota(jnp.int32, (blk, 2 * blk), 0)
    sj = lax.broadcasted_iota(jnp.int32, (blk, 2 * blk), 1)
    rel = qi + blk - sj
    mask = (rel >= 0) & (rel <= WINDOW) & ((sj >= blk) | (i > 0))
    mask = jnp.concatenate([mask] * GQA_GROUP, axis=0)

    heads = []
    for g in range(N_KV_HEADS):
        qg = jnp.concatenate(
            [qr[:, (g * GQA_GROUP + hh) * HEAD_DIM:(g * GQA_GROUP + hh + 1) * HEAD_DIM] for hh in range(GQA_GROUP)],
            axis=0).astype(BF16)
        kg = k_all[:, g * HEAD_DIM:(g + 1) * HEAD_DIM]
        vg = v_all[:, g * HEAD_DIM:(g + 1) * HEAD_DIM]
        s = lax.dot_general(qg, kg, (((1,), (1,)), ((), ())), preferred_element_type=F32)
        s = jnp.where(mask, s, -jnp.inf)
        sink = jnp.concatenate(
            [jnp.full((blk, 1), sink_ref[g * GQA_GROUP + hh], F32) for hh in range(GQA_GROUP)], axis=0)
        m = jnp.maximum(jnp.max(s, -1, keepdims=True), sink)
        e = jnp.exp(s - m)
        p = e / (jnp.sum(e, -1, keepdims=True) + jnp.exp(sink - m))
        og = jnp.dot(p.astype(BF16), vg, preferred_element_type=F32)
        heads += [og[hh * blk:(hh + 1) * blk] for hh in range(GQA_GROUP)]
    attn = jnp.concatenate(heads, axis=-1)
    mixed_ref[:, :ATTN_WIDTH] = _rms_norm(attn, gna_ref[...]).astype(BF16)

    u = c_ref[...] * x_ref[...]
    n_tail = CONV_K - 1
    tail_rows = cp_ref.shape[0]
    row_id = lax.broadcasted_iota(jnp.int32, u.shape, 0)
    acc = u * convw_ref[CONV_K - 1:CONV_K, :]
    for d in range(1, CONV_K):
        ud = pltpu.roll(u, d, 0)
        for r in range(d):
            src = tail_rows - d + r
            up = jnp.where(i > 0, cp_ref[src:src + 1, :] * xp_ref[src:src + 1, :], 0.0)
            ud = jnp.where(row_id == r, up, ud)
        acc = acc + ud * convw_ref[CONV_K - 1 - d:CONV_K - d, :]
    mixed_ref[:, ATTN_WIDTH:] = _rms_norm(b_ref[...] * acc, gnc_ref[...]).astype(BF16)

    @pl.when(i == n_blocks - 1)
    def _():
        kwin_ref[0] = k_cur
        vwin_ref[0] = v_cur
        cstate_ref[0] = u[blk - n_tail:, :]


def _prompt_mixer(proj, sinks, rope_tab, conv_w, gn_attn, gn_conv, n_batch, seq):
    blk = ATTN_BLOCK
    nb = seq // blk
    tail = 8
    wide = CONV_WIDTH // 1
    kvb = COL_KV // (2 * KV_WIDTH)

    def row(b, i, s):
        return b * nb + i

    def prev_row(b, i, s):
        return b * nb + jnp.maximum(i - 1, 0)

    def prev_tail(b, i, s):
        return jnp.maximum((b * nb + i) * (blk // tail) - 1, 0)

    grid_spec = pltpu.PrefetchScalarGridSpec(
        num_scalar_prefetch=1,
        grid=(n_batch, nb),
        in_specs=[
            pl.BlockSpec((blk, wide), lambda b, i, s: (row(b, i, s), COL_B // wide)),
            pl.BlockSpec((blk, wide), lambda b, i, s: (row(b, i, s), COL_C // wide)),
            pl.BlockSpec((blk, wide), lambda b, i, s: (row(b, i, s), COL_X // wide)),
            pl.BlockSpec((blk, ATTN_WIDTH), lambda b, i, s: (row(b, i, s), COL_Q // ATTN_WIDTH)),
            pl.BlockSpec((blk, 2 * KV_WIDTH), lambda b, i, s: (row(b, i, s), kvb)),
            pl.BlockSpec((blk, 2 * KV_WIDTH), lambda b, i, s: (prev_row(b, i, s), kvb)),
            pl.BlockSpec((tail, wide), lambda b, i, s: (prev_tail(b, i, s), COL_C // wide)),
            pl.BlockSpec((tail, wide), lambda b, i, s: (prev_tail(b, i, s), COL_X // wide)),
            pl.BlockSpec((3, blk, LANES), lambda b, i, s: (0, i, 0)),
            pl.BlockSpec((3, blk, LANES), lambda b, i, s: (0, jnp.maximum(i - 1, 0), 0)),
            pl.BlockSpec((CONV_K, CONV_WIDTH), lambda b, i, s: (0, 0)),
            pl.BlockSpec((1, ATTN_WIDTH), lambda b, i, s: (0, 0)),
            pl.BlockSpec((1, CONV_WIDTH), lambda b, i, s: (0, 0)),
        ],
        out_specs=[
            pl.BlockSpec((blk, D_MODEL), lambda b, i, s: (row(b, i, s), 0)),
            pl.BlockSpec((1, blk, KV_WIDTH), lambda b, i, s: (b, 0, 0)),
            pl.BlockSpec((1, blk, KV_WIDTH), lambda b, i, s: (b, 0, 0)),
            pl.BlockSpec((1, CONV_K - 1, CONV_WIDTH), lambda b, i, s: (b, 0, 0)),
        ],
    )
    return pl.pallas_call(
        functools.partial(_prompt_mixer_kernel, n_blocks=nb),
        out_shape=[
            jax.ShapeDtypeStruct((n_batch * seq, D_MODEL), BF16),
            jax.ShapeDtypeStruct((n_batch, blk, KV_WIDTH), F32),
            jax.ShapeDtypeStruct((n_batch, blk, KV_WIDTH), F32),
            jax.ShapeDtypeStruct((n_batch, CONV_K - 1, CONV_WIDTH), F32),
        ],
        grid_spec=grid_spec,
        compiler_params=pltpu.CompilerParams(
            dimension_semantics=("arbitrary", "arbitrary"), vmem_limit_bytes=VMEM_LIMIT),
        name="prompt_mixer",
    )(sinks, proj, proj, proj, proj, proj, proj, proj, proj, rope_tab, rope_tab, conv_w, gn_attn, gn_conv)


def _expand_groups(t):
    ax = t.ndim - 1
    lane = lax.broadcasted_iota(jnp.int32, t.shape[:-1] + (LANES,), ax)
    chunks = []
    for c in range(KV_WIDTH // LANES):
        a = t[..., c * LANES:(c + 1) * LANES]
        r = pltpu.roll(a, HEAD_DIM, ax)
        lo = jnp.where(lane < HEAD_DIM, a, r)
        hi = jnp.where(lane < HEAD_DIM, r, a)
        chunks += [lo] * (GQA_GROUP // 2) + [hi] * (GQA_GROUP // 2)
    return jnp.concatenate(chunks, axis=-1)


def _sample_mixer_kernel(sink_ref, b_ref, c_ref, x_ref, q_ref, kv_ref, ck_ref, cv_ref, st_ref,
                         rope_ref, seg_ref, segt_ref, convw_ref, gna_ref, gnc_ref,
                         mixed_ref, kwin_ref, vwin_ref, cstate_ref):
    nb, wb = ck_ref.shape[0], ck_ref.shape[1]
    cos_t, sin_lo, sin_hi = rope_ref[0:1, :], rope_ref[1:2, :], rope_ref[2:3, :]
    qr = _rope_wide(q_ref[...], cos_t, sin_lo, sin_hi) * (HEAD_DIM ** -0.5)
    k_new = _rope_wide(kv_ref[:, :KV_WIDTH], cos_t, sin_lo, sin_hi)
    v_new = kv_ref[:, KV_WIDTH:]
    ck = ck_ref[...]
    cv = cv_ref[...]

    seg = seg_ref[...]
    prod = _expand_groups(ck) * qr[:, None, :]
    s_old = jnp.dot(prod.reshape(nb * wb, ATTN_WIDTH).astype(BF16), seg,
                    preferred_element_type=F32).reshape(nb, wb, N_HEADS)
    s_new = jnp.dot((_expand_groups(k_new) * qr).astype(BF16), seg, preferred_element_type=F32)
    sink = sink_ref[...]
    m = jnp.maximum(jnp.maximum(jnp.max(s_old, axis=1), s_new), sink)
    e_old = jnp.exp(s_old - m[:, None, :])
    e_new = jnp.exp(s_new - m)
    inv = 1.0 / (jnp.sum(e_old, axis=1) + e_new + jnp.exp(sink - m))
    p_old = (e_old * inv[:, None, :]).astype(BF16)
    p_new = (e_new * inv).astype(BF16)
    segt = segt_ref[...]
    pe_old = jnp.dot(p_old.reshape(nb * wb, N_HEADS), segt, preferred_element_type=F32).reshape(nb, wb, ATTN_WIDTH)
    pe_new = jnp.dot(p_new, segt, preferred_element_type=F32)
    attn = jnp.sum(pe_old * _expand_groups(cv), axis=1) + pe_new * _expand_groups(v_new)
    mixed_ref[:, :ATTN_WIDTH] = _rms_norm(attn, gna_ref[...]).astype(BF16)

    u = c_ref[...] * x_ref[...]
    acc = u * convw_ref[CONV_K - 1:CONV_K, :]
    for j in range(CONV_K - 1):
        acc = acc + st_ref[j] * convw_ref[j:j + 1, :]
    mixed_ref[:, ATTN_WIDTH:] = _rms_norm(b_ref[...] * acc, gnc_ref[...]).astype(BF16)

    kwin_ref[:, 0:wb - 1, :] = ck_ref[:, 1:wb, :]
    vwin_ref[:, 0:wb - 1, :] = cv_ref[:, 1:wb, :]
    for n in range(nb):
        kwin_ref[n, wb - 1:wb, :] = k_new[n:n + 1, :]
        vwin_ref[n, wb - 1:wb, :] = v_new[n:n + 1, :]
    for j in range(CONV_K - 2):
        cstate_ref[j] = st_ref[j + 1]
    cstate_ref[CONV_K - 2] = u


def _sample_mixer(proj, row0, n_dec, sinks, cache_k, cache_v, state_conv, rope_row, conv_w, gn_attn, gn_conv):
    nb = SAMPLE_CHUNK
    wb = cache_k.shape[1]
    r0 = row0 // nb
    wide = CONV_WIDTH
    head_of_lane = np.arange(ATTN_WIDTH) // HEAD_DIM
    seg = jnp.asarray(head_of_lane[:, None] == np.arange(N_HEADS)[None, :], BF16)
    kvb = COL_KV // (2 * KV_WIDTH)
    return pl.pallas_call(
        _sample_mixer_kernel,
        out_shape=[
            jax.ShapeDtypeStruct((n_dec, D_MODEL), BF16),
            jax.ShapeDtypeStruct((n_dec, wb, KV_WIDTH), F32),
            jax.ShapeDtypeStruct((n_dec, wb, KV_WIDTH), F32),
            jax.ShapeDtypeStruct((CONV_K - 1, n_dec, CONV_WIDTH), F32),
        ],
        grid=(n_dec // nb,),
        in_specs=[
            pl.BlockSpec((1, N_HEADS), lambda i: (0, 0)),
            pl.BlockSpec((nb, wide), lambda i: (r0 + i, COL_B // wide)),
            pl.BlockSpec((nb, wide), lambda i: (r0 + i, COL_C // wide)),
            pl.BlockSpec((nb, wide), lambda i: (r0 + i, COL_X // wide)),
            pl.BlockSpec((nb, ATTN_WIDTH), lambda i: (r0 + i, COL_Q // ATTN_WIDTH)),
            pl.BlockSpec((nb, 2 * KV_WIDTH), lambda i: (r0 + i, kvb)),
            pl.BlockSpec((nb, wb, KV_WIDTH), lambda i: (i, 0, 0)),
            pl.BlockSpec((nb, wb, KV_WIDTH), lambda i: (i, 0, 0)),
            pl.BlockSpec((CONV_K - 1, nb, CONV_WIDTH), lambda i: (0, i, 0)),
            pl.BlockSpec((3, LANES), lambda i: (0, 0)),
            pl.BlockSpec((ATTN_WIDTH, N_HEADS), lambda i: (0, 0)),
            pl.BlockSpec((N_HEADS, ATTN_WIDTH), lambda i: (0, 0)),
            pl.BlockSpec((CONV_K, CONV_WIDTH), lambda i: (0, 0)),
            pl.BlockSpec((1, ATTN_WIDTH), lambda i: (0, 0)),
            pl.BlockSpec((1, CONV_WIDTH), lambda i: (0, 0)),
        ],
        out_specs=[
            pl.BlockSpec((nb, D_MODEL), lambda i: (i, 0)),
            pl.BlockSpec((nb, wb, KV_WIDTH), lambda i: (i, 0, 0)),
            pl.BlockSpec((nb, wb, KV_WIDTH), lambda i: (i, 0, 0)),
            pl.BlockSpec((CONV_K - 1, nb, CONV_WIDTH), lambda i: (0, i, 0)),
        ],
        compiler_params=pltpu.CompilerParams(
            dimension_semantics=("arbitrary",), vmem_limit_bytes=VMEM_LIMIT),
        name="sample_mixer",
    )(sinks.reshape(1, N_HEADS), proj, proj, proj, proj, proj, cache_k, cache_v, state_conv,
      rope_row, seg, seg.T, conv_w, gn_attn, gn_conv)


def _pack_bf16_pairs(xb):
    n = xb.shape[1] // 2
    hi = pltpu.bitcast(xb[:, :n].astype(F32), jnp.uint32)
    lo = pltpu.bitcast(xb[:, n:].astype(F32), jnp.uint32)
    return hi | (lo >> 16)


def _unpack_bf16_pairs(xp):
    hi = pltpu.bitcast(xp & jnp.uint32(0xFFFF0000), F32).astype(BF16)
    lo = pltpu.bitcast(xp << 16, F32).astype(BF16)
    return jnp.concatenate([hi, lo], axis=1)


def _outproj_kernel(mixed_ref, x_ref, g0_ref, b0_ref, wout_ref, g1_ref, b1_ref, h1_ref, h1b_ref, h1p_ref):
    h = _layer_norm(x_ref[...], g0_ref[...], b0_ref[...])
    mixed = jnp.dot(mixed_ref[...], wout_ref[...], preferred_element_type=F32)
    h1 = _layer_norm(DEEPNORM_ALPHA * h + mixed, g1_ref[...], b1_ref[...])
    h1_ref[...] = h1
    h1b = h1.astype(BF16)
    h1b_ref[...] = h1b
    h1p_ref[...] = _pack_bf16_pairs(h1b)


def _outproj(mixed, x_all, g0, b0, w_out, g1, b1):
    t = x_all.shape[0]
    tm = ROW_TILE // 2
    const = lambda i: (0, 0)
    resident = functools.partial(pl.BlockSpec, index_map=const, pipeline_mode=pl.Buffered(1))
    return pl.pallas_call(
        _outproj_kernel,
        out_shape=[
            jax.ShapeDtypeStruct((t, D_MODEL), F32),
            jax.ShapeDtypeStruct((t, D_MODEL), BF16),
            jax.ShapeDtypeStruct((t, D_MODEL // 2), jnp.uint32),
        ],
        grid=(t // tm,),
        in_specs=[
            pl.BlockSpec((tm, D_MODEL), lambda i: (i, 0)),
            pl.BlockSpec((tm, D_MODEL), lambda i: (i, 0)),
            pl.BlockSpec((1, D_MODEL), const),
            pl.BlockSpec((1, D_MODEL), const),
            resident((D_MODEL, D_MODEL)),
            pl.BlockSpec((1, D_MODEL), const),
            pl.BlockSpec((1, D_MODEL), const),
        ],
        out_specs=[
            pl.BlockSpec((tm, D_MODEL), lambda i: (i, 0)),
            pl.BlockSpec((tm, D_MODEL), lambda i: (i, 0)),
            pl.BlockSpec((tm, D_MODEL // 2), lambda i: (i, 0)),
        ],
        compiler_params=pltpu.CompilerParams(
            dimension_semantics=("arbitrary",), vmem_limit_bytes=VMEM_LIMIT),
        name="outproj",
    )(mixed, x_all, g0, b0, w_out, g1, b1)


def _route_kernel(h_ref, wrt_ref, bias_ref, tri_ref, ones_ref, e_ref, gate_ref, rank_ref, cnt_ref):
    tm = h_ref.shape[0]
    neg = -jnp.inf
    big = float(N_EXPERTS)

    @pl.when(pl.program_id(0) == 0)
    def _():
        cnt_ref[...] = jnp.zeros_like(cnt_ref)

    logits = lax.dot_general(wrt_ref[...], h_ref[...], (((1,), (1,)), ((), ())), preferred_element_type=F32)
    scores = jax.nn.sigmoid(logits)
    biased = scores + bias_ref[...]

    per_group = N_EXPERTS // N_EXPERT_GROUPS
    rid = lax.broadcasted_iota(jnp.int32, (per_group, tm), 0).astype(F32)
    gscore = []
    for g in range(N_EXPERT_GROUPS):
        xg = biased[g * per_group:(g + 1) * per_group]
        m1 = jnp.max(xg, 0, keepdims=True)
        i1 = jnp.min(jnp.where(xg == m1, rid, big), 0, keepdims=True)
        m2 = jnp.max(jnp.where(rid == i1, neg, xg), 0, keepdims=True)
        gscore.append(m1 + m2)
    masked = []
    for g in range(N_EXPERT_GROUPS):
        beaten = jnp.zeros((1, tm), F32)
        for o in range(N_EXPERT_GROUPS):
            if o != g:
                wins = (gscore[o] > gscore[g]) | ((gscore[o] == gscore[g]) & (o < g))
                beaten = beaten + jnp.where(wins, 1.0, 0.0)
        masked.append(jnp.where(beaten < TOPK_GROUPS, biased[g * per_group:(g + 1) * per_group], neg))
    x = jnp.concatenate(masked, axis=0)

    eid = lax.broadcasted_iota(jnp.int32, (N_EXPERTS, tm), 0).astype(F32)
    sel = jnp.zeros((N_EXPERTS, tm), F32)
    picks, gates = [], []
    for _ in range(TOP_K):
        m = jnp.max(x, 0, keepdims=True)
        ik = jnp.min(jnp.where(x == m, eid, big), 0, keepdims=True)
        hit = eid == ik
        gates.append(jnp.sum(jnp.where(hit, scores, 0.0), 0, keepdims=True))
        picks.append(ik)
        x = jnp.where(hit, neg, x)
        sel = jnp.where(hit, 1.0, sel)

    selb = sel.astype(BF16)
    pos = cnt_ref[...] + jnp.dot(selb, tri_ref[...], preferred_element_type=F32)
    cnt_ref[...] = cnt_ref[...] + jnp.dot(selb, ones_ref[...], preferred_element_type=F32)

    gsum = gates[0]
    for k in range(1, TOP_K):
        gsum = gsum + gates[k]
    for k in range(TOP_K):
        e_ref[k:k + 1, :] = picks[k].astype(jnp.int32)
        gate_ref[k:k + 1, :] = gates[k] / gsum * ROUTED_SCALE
        rank_ref[k:k + 1, :] = jnp.sum(jnp.where(eid == picks[k], pos, 0.0), 0, keepdims=True).astype(jnp.int32)


def _route(h1b, w_router_t, bias_col):
    t = h1b.shape[0]
    tm = LANES
    tri = jnp.asarray(np.arange(tm)[:, None] < np.arange(tm)[None, :], BF16)
    ones = jnp.ones((tm, tm), BF16)
    const = lambda i: (0, 0)
    return pl.pallas_call(
        _route_kernel,
        out_shape=[
            jax.ShapeDtypeStruct((TOP_K, t), jnp.int32),
            jax.ShapeDtypeStruct((TOP_K, t), F32),
            jax.ShapeDtypeStruct((TOP_K, t), jnp.int32),
            jax.ShapeDtypeStruct((N_EXPERTS, tm), F32),
        ],
        grid=(t // tm,),
        in_specs=[
            pl.BlockSpec((tm, D_MODEL), lambda i: (i, 0)),
            pl.BlockSpec((N_EXPERTS, D_MODEL), const),
            pl.BlockSpec((N_EXPERTS, 1), const),
            pl.BlockSpec((tm, tm), const),
            pl.BlockSpec((tm, tm), const),
        ],
        out_specs=[
            pl.BlockSpec((TOP_K, tm), lambda i: (0, i)),
            pl.BlockSpec((TOP_K, tm), lambda i: (0, i)),
            pl.BlockSpec((TOP_K, tm), lambda i: (0, i)),
            pl.BlockSpec((N_EXPERTS, tm), const),
        ],
        compiler_params=pltpu.CompilerParams(
            dimension_semantics=("arbitrary",), vmem_limit_bytes=VMEM_LIMIT),
        name="route",
    )(h1b, w_router_t, bias_col, tri, ones)


def _pow2_chunks(n_max):
    return [1 << b for b in reversed(range(int(n_max).bit_length()))]


def _dispatch_kernel(pstart_ref, fill0_ref, filln_ref, nused_ref, e_ref, rank_ref, h1p_hbm, zeros_hbm,
                     xs_hbm, dest_ref, row_sem, fill_sem, *, n_blocks):
    i = pl.program_id(0)
    tm = e_ref.shape[1]
    bm = EXPERT_ROWS

    def fill_copies(e):
        n = filln_ref[e]
        start = fill0_ref[e]
        head = jnp.minimum((SUBLANES - start % SUBLANES) % SUBLANES, n)
        out = []
        for r in range(SUBLANES - 1):
            out.append((r < head,
                        pltpu.make_async_copy(zeros_hbm.at[pl.ds(0, 1), :], xs_hbm.at[pl.ds(start + r, 1), :], fill_sem)))
        body0 = start + head
        m = n - head
        for c in _pow2_chunks(bm - 1):
            if c < SUBLANES:
                continue
            off = pl.multiple_of(body0 + (m // (2 * c)) * (2 * c), SUBLANES)
            out.append(((m // c) % 2 == 1,
                        pltpu.make_async_copy(zeros_hbm.at[pl.ds(0, c), :], xs_hbm.at[pl.ds(off, c), :], fill_sem)))
        return out

    def tail_copy(j):
        return pltpu.make_async_copy(zeros_hbm, xs_hbm.at[pl.ds(pl.multiple_of(j * bm, bm), bm), :], fill_sem)

    @pl.when(i == 0)
    def _():
        def start_fill(e, c):
            for pred, cp in fill_copies(e):
                @pl.when(pred)
                def _():
                    cp.start()
            return c
        lax.fori_loop(0, N_EXPERTS, start_fill, 0)
        lax.fori_loop(nused_ref[0], n_blocks, lambda j, c: (tail_copy(j).start(), c)[1], 0)

    def issue(t, c):
        for k in range(TOP_K):
            d = pstart_ref[e_ref[k, t]] + rank_ref[k, t]
            dest_ref[k, t] = d
            pltpu.make_async_copy(h1p_hbm.at[pl.ds(i * tm + t, 1), :], xs_hbm.at[pl.ds(d, 1), :], row_sem).start()
        return c
    lax.fori_loop(0, tm, issue, 0)
    pltpu.make_async_copy(h1p_hbm.at[pl.ds(0, TOP_K * tm), :], xs_hbm.at[pl.ds(0, TOP_K * tm), :], row_sem).wait()

    @pl.when(i == 0)
    def _():
        def wait_fill(e, c):
            for pred, cp in fill_copies(e):
                @pl.when(pred)
                def _():
                    cp.wait()
            return c
        lax.fori_loop(0, N_EXPERTS, wait_fill, 0)
        lax.fori_loop(nused_ref[0], n_blocks, lambda j, c: (tail_copy(j).wait(), c)[1], 0)


def _dispatch_rows(pad_start, fill_start, fill_len, n_used, top_e, rank, h1p, n_blocks):
    t = h1p.shape[0]
    tm = LANES
    width = h1p.shape[1]
    smem_tile = pl.BlockSpec((TOP_K, tm), lambda i, *_: (0, i), memory_space=pltpu.SMEM)
    grid_spec = pltpu.PrefetchScalarGridSpec(
        num_scalar_prefetch=4,
        grid=(t // tm,),
        in_specs=[smem_tile, smem_tile, pl.BlockSpec(memory_space=pl.ANY), pl.BlockSpec(memory_space=pl.ANY)],
        out_specs=[pl.BlockSpec(memory_space=pl.ANY), smem_tile],
        scratch_shapes=[pltpu.SemaphoreType.DMA, pltpu.SemaphoreType.DMA],
    )
    return pl.pallas_call(
        functools.partial(_dispatch_kernel, n_blocks=n_blocks),
        out_shape=[
            jax.ShapeDtypeStruct((n_blocks * EXPERT_ROWS, width), jnp.uint32),
            jax.ShapeDtypeStruct((TOP_K, t), jnp.int32),
        ],
        grid_spec=grid_spec,
        compiler_params=pltpu.CompilerParams(dimension_semantics=("arbitrary",)),
        name="dispatch",
    )(pad_start, fill_start, fill_len, n_used, top_e, rank, h1p, jnp.zeros((EXPERT_ROWS, width), jnp.uint32))


def _experts_kernel(be_ref, first_ref, slot_ref, nxt_ref, nused_ref, x_ref, wg_hbm, wu_hbm, wd_hbm, y_ref,
                    wg_buf, wu_buf, wd_buf, wg_bf, wu_bf, wd_bf, sems):
    i = pl.program_id(0)

    def weight_copies(e, slot):
        return (pltpu.make_async_copy(wg_hbm.at[e], wg_buf.at[slot], sems.at[slot, 0]),
                pltpu.make_async_copy(wu_hbm.at[e], wu_buf.at[slot], sems.at[slot, 1]),
                pltpu.make_async_copy(wd_hbm.at[e], wd_buf.at[slot], sems.at[slot, 2]))

    @pl.when(i == 0)
    def _():
        for cp in weight_copies(be_ref[0], 0):
            cp.start()

    @pl.when(first_ref[i] == 1)
    def _():
        slot = slot_ref[i]
        for cp in weight_copies(be_ref[i], slot):
            cp.wait()

        @pl.when(nxt_ref[i] >= 0)
        def _():
            for cp in weight_copies(nxt_ref[i], 1 - slot):
                cp.start()

        wg_bf[...] = wg_buf[slot].astype(BF16)
        wu_bf[...] = wu_buf[slot].astype(BF16)
        wd_bf[...] = wd_buf[slot].astype(BF16)

    @pl.when(i < nused_ref[0])
    def _():
        x = _unpack_bf16_pairs(x_ref[...])
        g = jnp.dot(x, wg_bf[...], preferred_element_type=F32)
        u = jnp.dot(x, wu_bf[...], preferred_element_type=F32)
        a = (g * jax.nn.sigmoid(g) * u).astype(BF16)
        y_ref[...] = jnp.dot(a, wd_bf[...], preferred_element_type=F32)

    @pl.when(i >= nused_ref[0])
    def _():
        y_ref[...] = jnp.zeros_like(y_ref)


def _experts(block_e, first, slot, nxt, n_used, xs, w_gate, w_up, w_down):
    p = xs.shape[0]
    bm = EXPERT_ROWS
    n_blocks = p // bm

    def xrow(i, be, fi, sl, nx, nu):
        return (jnp.minimum(i, nu[0] - 1), 0)

    grid_spec = pltpu.PrefetchScalarGridSpec(
        num_scalar_prefetch=5,
        grid=(n_blocks,),
        in_specs=[
            pl.BlockSpec((bm, D_MODEL // 2), xrow),
            pl.BlockSpec(memory_space=pl.ANY),
            pl.BlockSpec(memory_space=pl.ANY),
            pl.BlockSpec(memory_space=pl.ANY),
        ],
        out_specs=pl.BlockSpec((bm, D_MODEL), lambda i, *_: (i, 0)),
        scratch_shapes=[
            pltpu.VMEM((2, D_MODEL, D_EXPERT), F32),
            pltpu.VMEM((2, D_MODEL, D_EXPERT), F32),
            pltpu.VMEM((2, D_EXPERT, D_MODEL), F32),
            pltpu.VMEM((D_MODEL, D_EXPERT), BF16),
            pltpu.VMEM((D_MODEL, D_EXPERT), BF16),
            pltpu.VMEM((D_EXPERT, D_MODEL), BF16),
            pltpu.SemaphoreType.DMA((2, 3)),
        ],
    )
    return pl.pallas_call(
        _experts_kernel,
        out_shape=jax.ShapeDtypeStruct((p, D_MODEL), F32),
        grid_spec=grid_spec,
        compiler_params=pltpu.CompilerParams(
            dimension_semantics=("arbitrary",), vmem_limit_bytes=VMEM_LIMIT),
        name="experts",
    )(block_e, first, slot, nxt, n_used, xs, w_gate, w_up, w_down)


def _combine_kernel(dest_ref, gate_ref, yb_hbm, o_ref, buf, sem):
    tm = o_ref.shape[0]

    def issue(t, c):
        for k in range(TOP_K):
            pltpu.make_async_copy(yb_hbm.at[pl.ds(dest_ref[k, t], 1), :], buf.at[pl.ds(k * tm + t, 1), :], sem).start()
        return c
    lax.fori_loop(0, tm, issue, 0)
    pltpu.make_async_copy(yb_hbm.at[pl.ds(0, TOP_K * tm), :], buf, sem).wait()

    acc = gate_ref[:, 0:1] * buf[0:tm, :]
    for k in range(1, TOP_K):
        acc = acc + gate_ref[:, k:k + 1] * buf[k * tm:(k + 1) * tm, :]
    o_ref[...] = acc


def _combine(dest, gate_tk, yb):
    t = gate_tk.shape[0]
    tm = LANES
    return pl.pallas_call(
        _combine_kernel,
        out_shape=jax.ShapeDtypeStruct((t, D_MODEL), F32),
        grid=(t // tm,),
        in_specs=[
            pl.BlockSpec((TOP_K, tm), lambda i: (0, i), memory_space=pltpu.SMEM),
            pl.BlockSpec((tm, TOP_K), lambda i: (i, 0)),
            pl.BlockSpec(memory_space=pl.ANY),
        ],
        out_specs=pl.BlockSpec((tm, D_MODEL), lambda i: (i, 0)),
        scratch_shapes=[pltpu.VMEM((TOP_K * tm, D_MODEL), F32), pltpu.SemaphoreType.DMA],
        compiler_params=pltpu.CompilerParams(
            dimension_semantics=("arbitrary",), vmem_limit_bytes=VMEM_LIMIT),
        name="combine",
    )(dest, gate_tk, yb)


def _final_kernel(h1_ref, h1b_ref, routed_ref, p_ref, wsg_ref, wsu_ref, wsd_ref, wpg_ref, wp_ref,
                  g2_ref, b2_ref, y_ref):
    hb = h1b_ref[...]
    g = jnp.dot(hb, wsg_ref[...], preferred_element_type=F32)
    u = jnp.dot(hb, wsu_ref[...], preferred_element_type=F32)
    shared = jnp.dot((g * jax.nn.sigmoid(g) * u).astype(BF16), wsd_ref[...], preferred_element_type=F32)
    gate = jax.nn.sigmoid(jnp.dot(hb, wpg_ref[...], preferred_element_type=F32))
    ple = gate * jnp.dot(p_ref[...], wp_ref[...], preferred_element_type=F32)
    r = DEEPNORM_ALPHA * h1_ref[...] + (routed_ref[...] + shared) + ple
    y_ref[...] = _layer_norm(r, g2_ref[...], b2_ref[...])


def _final(h1, h1b, routed, p_all, ws_gate, ws_up, ws_down, w_ple_gate, w_ple, g2, b2):
    t = h1.shape[0]
    tm = ROW_TILE // 2
    const = lambda i: (0, 0)
    resident = functools.partial(pl.BlockSpec, index_map=const, pipeline_mode=pl.Buffered(1))
    return pl.pallas_call(
        _final_kernel,
        out_shape=jax.ShapeDtypeStruct((t, D_MODEL), F32),
        grid=(t // tm,),
        in_specs=[
            pl.BlockSpec((tm, D_MODEL), lambda i: (i, 0)),
            pl.BlockSpec((tm, D_MODEL), lambda i: (i, 0)),
            pl.BlockSpec((tm, D_MODEL), lambda i: (i, 0)),
            pl.BlockSpec((tm, PLE_DIM), lambda i: (i, 0)),
            resident((D_MODEL, D_SHARED)),
            resident((D_MODEL, D_SHARED)),
            resident((D_SHARED, D_MODEL)),
            resident((D_MODEL, D_MODEL)),
            resident((PLE_DIM, D_MODEL)),
            pl.BlockSpec((1, D_MODEL), const),
            pl.BlockSpec((1, D_MODEL), const),
        ],
        out_specs=pl.BlockSpec((tm, D_MODEL), lambda i: (i, 0)),
        compiler_params=pltpu.CompilerParams(
            dimension_semantics=("arbitrary",), vmem_limit_bytes=VMEM_LIMIT),
        name="final",
    )(h1, h1b, routed, p_all, ws_gate, ws_up, ws_down, w_ple_gate, w_ple, g2, b2)


def _rope_tables(pos):
    half = ROT_DIM // 2
    inv_freq = ROPE_THETA ** (-jnp.arange(half, dtype=F32) * 2.0 / ROT_DIM)
    ang = pos.astype(F32)[:, None] * inv_freq[None, :]
    cos, sin = jnp.cos(ang), jnp.sin(ang)
    n = pos.shape[0]
    ones = jnp.ones((n, HEAD_DIM - ROT_DIM), F32)
    zeros = jnp.zeros((n, HEAD_DIM - ROT_DIM), F32)
    zh = jnp.zeros((n, half), F32)
    cos_t = jnp.concatenate([cos, cos, ones], -1)
    sin_lo = jnp.concatenate([-sin, zh, zeros], -1)
    sin_hi = jnp.concatenate([zh, sin, zeros], -1)
    per_head = jnp.stack([cos_t, sin_lo, sin_hi])
    return jnp.concatenate([per_head] * (LANES // HEAD_DIM), -1)


def _block_plan(counts, n_blocks):
    bm = EXPERT_ROWS
    i32 = jnp.int32
    padded = (counts + bm - 1) // bm * bm
    pad_end = jnp.cumsum(padded)
    pad_start = pad_end - padded
    n_used = pad_end[-1] // bm
    blk = jnp.arange(n_blocks)
    valid = blk < n_used
    block_e = jnp.minimum(jnp.searchsorted(pad_end, blk * bm, side='right'), N_EXPERTS - 1)
    block_e = jnp.where(valid, block_e, block_e[jnp.maximum(n_used - 1, 0)])
    prev_e = jnp.concatenate([jnp.full((1,), -1, block_e.dtype), block_e[:-1]])
    first = valid & (block_e != prev_e)
    run = jnp.cumsum(first) - 1
    run_expert = jnp.full((n_blocks + 2,), -1, block_e.dtype).at[jnp.where(first, run, n_blocks + 1)].set(block_e)
    nxt = jnp.where(first, run_expert[jnp.minimum(run + 1, n_blocks)], -1)
    return (pad_start.astype(i32), (pad_start + counts).astype(i32), (padded - counts).astype(i32),
            n_used.astype(i32).reshape(1), block_e.astype(i32), first.astype(i32), (run % 2).astype(i32),
            nxt.astype(i32))


def kernel(x_prompt, x_sample, p_prompt, p_sample, cache_k, cache_v, state_conv, ln_in_g, ln_in_b, w_in,
           attn_sinks, conv_w, gn_attn, gn_conv, w_out, ln1_g, ln1_b, w_router, router_bias, w_gate, w_up,
           w_down, ws_gate, ws_up, ws_down, w_ple_gate, w_ple, ln2_g, ln2_b):
    n_batch, seq, d = x_prompt.shape
    n_dec, dec_seq, _ = x_sample.shape
    depth = w_in.shape[0]
    wb = cache_k.shape[2]
    assert depth == 1 and dec_seq == 1 and d == D_MODEL
    assert wb == ATTN_BLOCK and seq % ATTN_BLOCK == 0 and n_dec % SAMPLE_CHUNK == 0
    t_p = n_batch * seq
    t = t_p + n_dec
    assert t % ROW_TILE == 0 and t_p % SAMPLE_CHUNK == 0

    x_all = jnp.concatenate([x_prompt.reshape(t_p, d), x_sample.reshape(n_dec, d)], 0)
    p_all = jnp.concatenate([p_prompt[0].reshape(t_p, PLE_DIM), p_sample[0].reshape(n_dec, PLE_DIM)], 0).astype(BF16)
    row = lambda v: v.reshape(1, -1).astype(F32)

    w_in0 = w_in[0]
    qkv_w = ATTN_WIDTH + 2 * KV_WIDTH
    w_in_perm = jnp.concatenate([w_in0[:, qkv_w:], w_in0[:, :qkv_w]], axis=1).astype(BF16)

    proj = _ln_inproj(x_all, row(ln_in_g), row(ln_in_b), w_in_perm)

    rope_p = _rope_tables(jnp.arange(seq))
    rope_s = _rope_tables(PAST_LEN + jnp.arange(dec_seq))[:, 0, :]
    sinks = attn_sinks[0].astype(F32)
    cw, ga, gc = conv_w[0].astype(F32), row(gn_attn[0]), row(gn_conv[0])
    mixed_p, kwin_p, vwin_p, cstate_p = _prompt_mixer(proj, sinks, rope_p, cw, ga, gc, n_batch, seq)
    mixed_s, kwin_s, vwin_s, cstate_s = _sample_mixer(
        proj, t_p, n_dec, sinks, cache_k[0].reshape(n_dec, wb, KV_WIDTH), cache_v[0].reshape(n_dec, wb, KV_WIDTH),
        jnp.swapaxes(state_conv[0], 0, 1), rope_s, cw, ga, gc)
    cstate_s = jnp.swapaxes(cstate_s, 0, 1)
    mixed = jnp.concatenate([mixed_p, mixed_s], 0)

    h1, h1b, h1p = _outproj(
        mixed, x_all, row(ln_in_g), row(ln_in_b), w_out[0].astype(BF16), row(ln1_g[0]), row(ln1_b[0]))

    top_e, gate, rank, cnt = _route(h1b, w_router[0].T.astype(BF16), router_bias[0].astype(F32).reshape(N_EXPERTS, 1))
    n_blocks = t * TOP_K // EXPERT_ROWS + N_EXPERTS
    pad_start, fill_start, fill_len, n_used, block_e, first, slot, nxt = _block_plan(cnt[:, 0].astype(jnp.int32), n_blocks)
    xs, dest = _dispatch_rows(pad_start, fill_start, fill_len, n_used, top_e, rank, h1p, n_blocks)
    yb = _experts(block_e, first, slot, nxt, n_used, xs, w_gate[0], w_up[0], w_down[0])
    routed = _combine(dest, gate.T, yb)

    y = _final(h1, h1b, routed, p_all, ws_gate[0].astype(BF16), ws_up[0].astype(BF16), ws_down[0].astype(BF16),
               w_ple_gate[0].astype(BF16), w_ple[0].astype(BF16), row(ln2_g[0]), row(ln2_b[0]))

    kv_shape = (1, -1, wb, N_KV_HEADS, HEAD_DIM)
    return (y[:t_p].reshape(n_batch, seq, d), y[t_p:].reshape(n_dec, dec_seq, d),
            kwin_p.reshape(kv_shape), vwin_p.reshape(kv_shape), cstate_p[None],
            kwin_s.reshape(kv_shape), vwin_s.reshape(kv_shape), cstate_s[None])
```

```python
import functools

import numpy as np
import jax
import jax.numpy as jnp
from jax import lax
from jax.experimental import pallas as pl
from jax.experimental.pallas import tpu as pltpu

D_MODEL = 2048
N_HEADS = 16
N_KV_HEADS = 4
HEAD_DIM = 64
GQA_GROUP = N_HEADS // N_KV_HEADS
ATTN_WIDTH = N_HEADS * HEAD_DIM
KV_WIDTH = N_KV_HEADS * HEAD_DIM
WINDOW = 128
ATTN_BLOCK = WINDOW
ROT_DIM = HEAD_DIM // 4
ROPE_THETA = 500000.0
CONV_WIDTH = D_MODEL - ATTN_WIDTH
CONV_K = 3
IN_COLS = ATTN_WIDTH + 2 * KV_WIDTH + 3 * CONV_WIDTH
N_EXPERTS = 256
TOP_K = 8
N_EXPERT_GROUPS = 8
TOPK_GROUPS = 4
D_EXPERT = 512
D_SHARED = 512
ROUTED_SCALE = 2.5
PLE_DIM = 256
LN_EPS = 1e-5
DEPTH = 1
PAST_LEN = 8192
DEEPNORM_ALPHA = (2 * DEPTH) ** 0.25

LANES = 128
SUBLANES = 8
COL_B, COL_C, COL_X, COL_Q, COL_KV = 0, CONV_WIDTH, 2 * CONV_WIDTH, 3 * CONV_WIDTH, 3 * CONV_WIDTH + ATTN_WIDTH

ROW_TILE = 640
INPROJ_COL_TILE = 1536
EXPERT_ROWS = 128
SAMPLE_CHUNK = 8
VMEM_LIMIT = 56 * 1024 * 1024

BF16 = jnp.bfloat16
F32 = jnp.float32


def _layer_norm(x, g, b):
    xc = x - jnp.mean(x, -1, keepdims=True)
    var = jnp.mean(xc * xc, -1, keepdims=True)
    return xc * lax.rsqrt(var + LN_EPS) * g + b


def _rms_norm(x, g):
    return x * lax.rsqrt(jnp.mean(x * x, -1, keepdims=True) + LN_EPS) * g


def _rope_lanes(x, cos_t, sin_lo, sin_hi):
    half = ROT_DIM // 2
    ax = x.ndim - 1
    return x * cos_t + pltpu.roll(x, LANES - half, ax) * sin_lo + pltpu.roll(x, half, ax) * sin_hi


def _rope_wide(x, cos_t, sin_lo, sin_hi):
    n = x.shape[-1] // LANES
    return jnp.concatenate(
        [_rope_lanes(x[..., c * LANES:(c + 1) * LANES], cos_t, sin_lo, sin_hi) for c in range(n)], axis=-1)


def _ln_inproj_kernel(x_ref, g_ref, b_ref, w_ref, o_ref, h_scr):
    @pl.when(pl.program_id(1) == 0)
    def _():
        h_scr[...] = _layer_norm(x_ref[...], g_ref[...], b_ref[...]).astype(BF16)

    o_ref[...] = jnp.dot(h_scr[...], w_ref[...], preferred_element_type=F32)


def _ln_inproj(x_all, g, b, w_bf16):
    t = x_all.shape[0]
    return pl.pallas_call(
        _ln_inproj_kernel,
        out_shape=jax.ShapeDtypeStruct((t, IN_COLS), F32),
        grid=(t // ROW_TILE, IN_COLS // INPROJ_COL_TILE),
        in_specs=[
            pl.BlockSpec((ROW_TILE, D_MODEL), lambda i, j: (i, 0)),
            pl.BlockSpec((1, D_MODEL), lambda i, j: (0, 0)),
            pl.BlockSpec((1, D_MODEL), lambda i, j: (0, 0)),
            pl.BlockSpec((D_MODEL, INPROJ_COL_TILE), lambda i, j: (0, j)),
        ],
        out_specs=pl.BlockSpec((ROW_TILE, INPROJ_COL_TILE), lambda i, j: (i, j)),
        scratch_shapes=[pltpu.VMEM((ROW_TILE, D_MODEL), BF16)],
        compiler_params=pltpu.CompilerParams(
            dimension_semantics=("arbitrary", "arbitrary"), vmem_limit_bytes=VMEM_LIMIT),
        name="ln_inproj",
    )(x_all, g, b, w_bf16)


def _prompt_mixer_kernel(sink_ref, b_ref, c_ref, x_ref, q_ref, kv_ref, kvp_ref, cp_ref, xp_ref,
                         rope_ref, ropep_ref, convw_ref, gna_ref, gnc_ref,
                         mixed_ref, kwin_ref, vwin_ref, cstate_ref, *, n_blocks):
    i = pl.program_id(1)
    blk = ATTN_BLOCK
    cos_t, sin_lo, sin_hi = rope_ref[0], rope_ref[1], rope_ref[2]
    qr = _rope_wide(q_ref[...], cos_t, sin_lo, sin_hi) * (HEAD_DIM ** -0.5)
    k_cur = _rope_wide(kv_ref[:, :KV_WIDTH], cos_t, sin_lo, sin_hi)
    k_prev = _rope_wide(kvp_ref[:, :KV_WIDTH], ropep_ref[0], ropep_ref[1], ropep_ref[2])
    v_cur = kv_ref[:, KV_WIDTH:]
    v_prev = kvp_ref[:, KV_WIDTH:]
    k_all = jnp.concatenate([k_prev, k_cur], axis=0).astype(BF16)
    v_all = jnp.concatenate([v_prev, v_cur], axis=0).astype(BF16)

    qi = lax.broadcasted_iota(jnp.int32, (blk, 2 * blk), 0)
    sj = lax.broadcasted_iota(jnp.int32, (blk, 2 * blk), 1)
    rel = qi + blk - sj
    mask = (rel >= 0) & (rel <= WINDOW) & ((sj >= blk) | (i > 0))
    mask = jnp.concatenate([mask] * GQA_GROUP, axis=0)

    heads = []
    for g in range(N_KV_HEADS):
        qg = jnp.concatenate(
            [qr[:, (g * GQA_GROUP + hh) * HEAD_DIM:(g * GQA_GROUP + hh + 1) * HEAD_DIM] for hh in range(GQA_GROUP)],
            axis=0).astype(BF16)
        kg = k_all[:, g * HEAD_DIM:(g + 1) * HEAD_DIM]
        vg = v_all[:, g * HEAD_DIM:(g + 1) * HEAD_DIM]
        s = lax.dot_general(qg, kg, (((1,), (1,)), ((), ())), preferred_element_type=F32)
        s = jnp.where(mask, s, -jnp.inf)
        sink = jnp.concatenate(
            [jnp.full((blk, 1), sink_ref[g * GQA_GROUP + hh], F32) for hh in range(GQA_GROUP)], axis=0)
        m = jnp.maximum(jnp.max(s, -1, keepdims=True), sink)
        e = jnp.exp(s - m)
        p = e / (jnp.sum(e, -1, keepdims=True) + jnp.exp(sink - m))
        og = jnp.dot(p.astype(BF16), vg, preferred_element_type=F32)
        heads += [og[hh * blk:(hh + 1) * blk] for hh in range(GQA_GROUP)]
    attn = jnp.concatenate(heads, axis=-1)
    mixed_ref[:, :ATTN_WIDTH] = _rms_norm(attn, gna_ref[...]).astype(BF16)

    u = c_ref[...] * x_ref[...]
    n_tail = CONV_K - 1
    tail_rows = cp_ref.shape[0]
    row_id = lax.broadcasted_iota(jnp.int32, u.shape, 0)
    acc = u * convw_ref[CONV_K - 1:CONV_K, :]
    for d in range(1, CONV_K):
        ud = pltpu.roll(u, d, 0)
        for r in range(d):
            src = tail_rows - d + r
            up = jnp.where(i > 0, cp_ref[src:src + 1, :] * xp_ref[src:src + 1, :], 0.0)
            ud = jnp.where(row_id == r, up, ud)
        acc = acc + ud * convw_ref[CONV_K - 1 - d:CONV_K - d, :]
    mixed_ref[:, ATTN_WIDTH:] = _rms_norm(b_ref[...] * acc, gnc_ref[...]).astype(BF16)

    @pl.when(i == n_blocks - 1)
    def _():
        kwin_ref[0] = k_cur
        vwin_ref[0] = v_cur
        cstate_ref[0] = u[blk - n_tail:, :]


def _prompt_mixer(proj, sinks, rope_tab, conv_w, gn_attn, gn_conv, n_batch, seq):
    blk = ATTN_BLOCK
    nb = seq // blk
    tail = 8
    wide = CONV_WIDTH // 1
    kvb = COL_KV // (2 * KV_WIDTH)

    def row(b, i, s):
        return b * nb + i

    def prev_row(b, i, s):
        return b * nb + jnp.maximum(i - 1, 0)

    def prev_tail(b, i, s):
        return jnp.maximum((b * nb + i) * (blk // tail) - 1, 0)

    grid_spec = pltpu.PrefetchScalarGridSpec(
        num_scalar_prefetch=1,
        grid=(n_batch, nb),
        in_specs=[
            pl.BlockSpec((blk, wide), lambda b, i, s: (row(b, i, s), COL_B // wide)),
            pl.BlockSpec((blk, wide), lambda b, i, s: (row(b, i, s), COL_C // wide)),
            pl.BlockSpec((blk, wide), lambda b, i, s: (row(b, i, s), COL_X // wide)),
            pl.BlockSpec((blk, ATTN_WIDTH), lambda b, i, s: (row(b, i, s), COL_Q // ATTN_WIDTH)),
            pl.BlockSpec((blk, 2 * KV_WIDTH), lambda b, i, s: (row(b, i, s), kvb)),
            pl.BlockSpec((blk, 2 * KV_WIDTH), lambda b, i, s: (prev_row(b, i, s), kvb)),
            pl.BlockSpec((tail, wide), lambda b, i, s: (prev_tail(b, i, s), COL_C // wide)),
            pl.BlockSpec((tail, wide), lambda b, i, s: (prev_tail(b, i, s), COL_X // wide)),
            pl.BlockSpec((3, blk, LANES), lambda b, i, s: (0, i, 0)),
            pl.BlockSpec((3, blk, LANES), lambda b, i, s: (0, jnp.maximum(i - 1, 0), 0)),
            pl.BlockSpec((CONV_K, CONV_WIDTH), lambda b, i, s: (0, 0)),
            pl.BlockSpec((1, ATTN_WIDTH), lambda b, i, s: (0, 0)),
            pl.BlockSpec((1, CONV_WIDTH), lambda b, i, s: (0, 0)),
        ],
        out_specs=[
            pl.BlockSpec((blk, D_MODEL), lambda b, i, s: (row(b, i, s), 0)),
            pl.BlockSpec((1, blk, KV_WIDTH), lambda b, i, s: (b, 0, 0)),
            pl.BlockSpec((1, blk, KV_WIDTH), lambda b, i, s: (b, 0, 0)),
            pl.BlockSpec((1, CONV_K - 1, CONV_WIDTH), lambda b, i, s: (b, 0, 0)),
        ],
    )
    return pl.pallas_call(
        functools.partial(_prompt_mixer_kernel, n_blocks=nb),
        out_shape=[
            jax.ShapeDtypeStruct((n_batch * seq, D_MODEL), BF16),
            jax.ShapeDtypeStruct((n_batch, blk, KV_WIDTH), F32),
            jax.ShapeDtypeStruct((n_batch, blk, KV_WIDTH), F32),
            jax.ShapeDtypeStruct((n_batch, CONV_K - 1, CONV_WIDTH), F32),
        ],
        grid_spec=grid_spec,
        compiler_params=pltpu.CompilerParams(
            dimension_semantics=("arbitrary", "arbitrary"), vmem_limit_bytes=VMEM_LIMIT),
        name="prompt_mixer",
    )(sinks, proj, proj, proj, proj, proj, proj, proj, proj, rope_tab, rope_tab, conv_w, gn_attn, gn_conv)


def _expand_groups(t):
    ax = t.ndim - 1
    lane = lax.broadcasted_iota(jnp.int32, t.shape[:-1] + (LANES,), ax)
    chunks = []
    for c in range(KV_WIDTH // LANES):
        a = t[..., c * LANES:(c + 1) * LANES]
        r = pltpu.roll(a, HEAD_DIM, ax)
        lo = jnp.where(lane < HEAD_DIM, a, r)
        hi = jnp.where(lane < HEAD_DIM, r, a)
        chunks += [lo] * (GQA_GROUP // 2) + [hi] * (GQA_GROUP // 2)
    return jnp.concatenate(chunks, axis=-1)


def _sample_mixer_kernel(sink_ref, b_ref, c_ref, x_ref, q_ref, kv_ref, ck_ref, cv_ref, st_ref,
                         rope_ref, seg_ref, segt_ref, convw_ref, gna_ref, gnc_ref,
                         mixed_ref, kwin_ref, vwin_ref, cstate_ref):
    nb, wb = ck_ref.shape[0], ck_ref.shape[1]
    cos_t, sin_lo, sin_hi = rope_ref[0:1, :], rope_ref[1:2, :], rope_ref[2:3, :]
    qr = _rope_wide(q_ref[...], cos_t, sin_lo, sin_hi) * (HEAD_DIM ** -0.5)
    k_new = _rope_wide(kv_ref[:, :KV_WIDTH], cos_t, sin_lo, sin_hi)
    v_new = kv_ref[:, KV_WIDTH:]
    ck = ck_ref[...]
    cv = cv_ref[...]

    seg = seg_ref[...]
    prod = _expand_groups(ck) * qr[:, None, :]
    s_old = jnp.dot(prod.reshape(nb * wb, ATTN_WIDTH).astype(BF16), seg,
                    preferred_element_type=F32).reshape(nb, wb, N_HEADS)
    s_new = jnp.dot((_expand_groups(k_new) * qr).astype(BF16), seg, preferred_element_type=F32)
    sink = sink_ref[...]
    m = jnp.maximum(jnp.maximum(jnp.max(s_old, axis=1), s_new), sink)
    e_old = jnp.exp(s_old - m[:, None, :])
    e_new = jnp.exp(s_new - m)
    inv = 1.0 / (jnp.sum(e_old, axis=1) + e_new + jnp.exp(sink - m))
    p_old = (e_old * inv[:, None, :]).astype(BF16)
    p_new = (e_new * inv).astype(BF16)
    segt = segt_ref[...]
    pe_old = jnp.dot(p_old.reshape(nb * wb, N_HEADS), segt, preferred_element_type=F32).reshape(nb, wb, ATTN_WIDTH)
    pe_new = jnp.dot(p_new, segt, preferred_element_type=F32)
    attn = jnp.sum(pe_old * _expand_groups(cv), axis=1) + pe_new * _expand_groups(v_new)
    mixed_ref[:, :ATTN_WIDTH] = _rms_norm(attn, gna_ref[...]).astype(BF16)

    u = c_ref[...] * x_ref[...]
    acc = u * convw_ref[CONV_K - 1:CONV_K, :]
    for j in range(CONV_K - 1):
        acc = acc + st_ref[j] * convw_ref[j:j + 1, :]
    mixed_ref[:, ATTN_WIDTH:] = _rms_norm(b_ref[...] * acc, gnc_ref[...]).astype(BF16)

    kwin_ref[:, 0:wb - 1, :] = ck_ref[:, 1:wb, :]
    vwin_ref[:, 0:wb - 1, :] = cv_ref[:, 1:wb, :]
    for n in range(nb):
        kwin_ref[n, wb - 1:wb, :] = k_new[n:n + 1, :]
        vwin_ref[n, wb - 1:wb, :] = v_new[n:n + 1, :]
    for j in range(CONV_K - 2):
        cstate_ref[j] = st_ref[j + 1]
    cstate_ref[CONV_K - 2] = u


def _sample_mixer(proj, row0, n_dec, sinks, cache_k, cache_v, state_conv, rope_row, conv_w, gn_attn, gn_conv):
    nb = SAMPLE_CHUNK
    wb = cache_k.shape[1]
    r0 = row0 // nb
    wide = CONV_WIDTH
    head_of_lane = np.arange(ATTN_WIDTH) // HEAD_DIM
    seg = jnp.asarray(head_of_lane[:, None] == np.arange(N_HEADS)[None, :], BF16)
    kvb = COL_KV // (2 * KV_WIDTH)
    return pl.pallas_call(
        _sample_mixer_kernel,
        out_shape=[
            jax.ShapeDtypeStruct((n_dec, D_MODEL), BF16),
            jax.ShapeDtypeStruct((n_dec, wb, KV_WIDTH), F32),
            jax.ShapeDtypeStruct((n_dec, wb, KV_WIDTH), F32),
            jax.ShapeDtypeStruct((CONV_K - 1, n_dec, CONV_WIDTH), F32),
        ],
        grid=(n_dec // nb,),
        in_specs=[
            pl.BlockSpec((1, N_HEADS), lambda i: (0, 0)),
            pl.BlockSpec((nb, wide), lambda i: (r0 + i, COL_B // wide)),
            pl.BlockSpec((nb, wide), lambda i: (r0 + i, COL_C // wide)),
            pl.BlockSpec((nb, wide), lambda i: (r0 + i, COL_X // wide)),
            pl.BlockSpec((nb, ATTN_WIDTH), lambda i: (r0 + i, COL_Q // ATTN_WIDTH)),
            pl.BlockSpec((nb, 2 * KV_WIDTH), lambda i: (r0 + i, kvb)),
            pl.BlockSpec((nb, wb, KV_WIDTH), lambda i: (i, 0, 0)),
            pl.BlockSpec((nb, wb, KV_WIDTH), lambda i: (i, 0, 0)),
            pl.BlockSpec((CONV_K - 1, nb, CONV_WIDTH), lambda i: (0, i, 0)),
            pl.BlockSpec((3, LANES), lambda i: (0, 0)),
            pl.BlockSpec((ATTN_WIDTH, N_HEADS), lambda i: (0, 0)),
            pl.BlockSpec((N_HEADS, ATTN_WIDTH), lambda i: (0, 0)),
            pl.BlockSpec((CONV_K, CONV_WIDTH), lambda i: (0, 0)),
            pl.BlockSpec((1, ATTN_WIDTH), lambda i: (0, 0)),
            pl.BlockSpec((1, CONV_WIDTH), lambda i: (0, 0)),
        ],
        out_specs=[
            pl.BlockSpec((nb, D_MODEL), lambda i: (i, 0)),
            pl.BlockSpec((nb, wb, KV_WIDTH), lambda i: (i, 0, 0)),
            pl.BlockSpec((nb, wb, KV_WIDTH), lambda i: (i, 0, 0)),
            pl.BlockSpec((CONV_K - 1, nb, CONV_WIDTH), lambda i: (0, i, 0)),
        ],
        compiler_params=pltpu.CompilerParams(
            dimension_semantics=("arbitrary",), vmem_limit_bytes=VMEM_LIMIT),
        name="sample_mixer",
    )(sinks.reshape(1, N_HEADS), proj, proj, proj, proj, proj, cache_k, cache_v, state_conv,
      rope_row, seg, seg.T, conv_w, gn_attn, gn_conv)


def _outproj_kernel(mixed_ref, x_ref, g0_ref, b0_ref, wout_ref, g1_ref, b1_ref, h1_ref, h1b_ref):
    h = _layer_norm(x_ref[...], g0_ref[...], b0_ref[...])
    mixed = jnp.dot(mixed_ref[...], wout_ref[...], preferred_element_type=F32)
    h1 = _layer_norm(DEEPNORM_ALPHA * h + mixed, g1_ref[...], b1_ref[...])
    h1_ref[...] = h1
    h1b_ref[...] = h1.astype(BF16)


def _outproj(mixed, x_all, g0, b0, w_out, g1, b1):
    t = x_all.shape[0]
    tm = ROW_TILE // 2
    const = lambda i: (0, 0)
    resident = functools.partial(pl.BlockSpec, index_map=const, pipeline_mode=pl.Buffered(1))
    return pl.pallas_call(
        _outproj_kernel,
        out_shape=[
            jax.ShapeDtypeStruct((t, D_MODEL), F32),
            jax.ShapeDtypeStruct((t, D_MODEL), BF16),
        ],
        grid=(t // tm,),
        in_specs=[
            pl.BlockSpec((tm, D_MODEL), lambda i: (i, 0)),
            pl.BlockSpec((tm, D_MODEL), lambda i: (i, 0)),
            pl.BlockSpec((1, D_MODEL), const),
            pl.BlockSpec((1, D_MODEL), const),
            resident((D_MODEL, D_MODEL)),
            pl.BlockSpec((1, D_MODEL), const),
            pl.BlockSpec((1, D_MODEL), const),
        ],
        out_specs=[
            pl.BlockSpec((tm, D_MODEL), lambda i: (i, 0)),
            pl.BlockSpec((tm, D_MODEL), lambda i: (i, 0)),
        ],
        compiler_params=pltpu.CompilerParams(
            dimension_semantics=("arbitrary",), vmem_limit_bytes=VMEM_LIMIT),
        name="outproj",
    )(mixed, x_all, g0, b0, w_out, g1, b1)


def _route_kernel(h_ref, wrt_ref, bias_ref, tri_ref, ones_ref, e_ref, gate_ref, rank_ref, cnt_ref):
    tm = h_ref.shape[0]
    neg = -jnp.inf
    big = float(N_EXPERTS)

    @pl.when(pl.program_id(0) == 0)
    def _():
        cnt_ref[...] = jnp.zeros_like(cnt_ref)

    logits = lax.dot_general(wrt_ref[...], h_ref[...], (((1,), (1,)), ((), ())), preferred_element_type=F32)
    scores = jax.nn.sigmoid(logits)
    biased = scores + bias_ref[...]

    per_group = N_EXPERTS // N_EXPERT_GROUPS
    rid = lax.broadcasted_iota(jnp.int32, (per_group, tm), 0).astype(F32)
    gscore = []
    for g in range(N_EXPERT_GROUPS):
        xg = biased[g * per_group:(g + 1) * per_group]
        m1 = jnp.max(xg, 0, keepdims=True)
        i1 = jnp.min(jnp.where(xg == m1, rid, big), 0, keepdims=True)
        m2 = jnp.max(jnp.where(rid == i1, neg, xg), 0, keepdims=True)
        gscore.append(m1 + m2)
    masked = []
    for g in range(N_EXPERT_GROUPS):
        beaten = jnp.zeros((1, tm), F32)
        for o in range(N_EXPERT_GROUPS):
            if o != g:
                wins = (gscore[o] > gscore[g]) | ((gscore[o] == gscore[g]) & (o < g))
                beaten = beaten + jnp.where(wins, 1.0, 0.0)
        masked.append(jnp.where(beaten < TOPK_GROUPS, biased[g * per_group:(g + 1) * per_group], neg))
    x = jnp.concatenate(masked, axis=0)

    eid = lax.broadcasted_iota(jnp.int32, (N_EXPERTS, tm), 0).astype(F32)
    sel = jnp.zeros((N_EXPERTS, tm), F32)
    picks, gates = [], []
    for _ in range(TOP_K):
        m = jnp.max(x, 0, keepdims=True)
        ik = jnp.min(jnp.where(x == m, eid, big), 0, keepdims=True)
        hit = eid == ik
        gates.append(jnp.sum(jnp.where(hit, scores, 0.0), 0, keepdims=True))
        picks.append(ik)
        x = jnp.where(hit, neg, x)
        sel = jnp.where(hit, 1.0, sel)

    selb = sel.astype(BF16)
    pos = cnt_ref[...] + jnp.dot(selb, tri_ref[...], preferred_element_type=F32)
    cnt_ref[...] = cnt_ref[...] + jnp.dot(selb, ones_ref[...], preferred_element_type=F32)

    gsum = gates[0]
    for k in range(1, TOP_K):
        gsum = gsum + gates[k]
    for k in range(TOP_K):
        e_ref[k:k + 1, :] = picks[k].astype(jnp.int32)
        gate_ref[k:k + 1, :] = gates[k] / gsum * ROUTED_SCALE
        rank_ref[k:k + 1, :] = jnp.sum(jnp.where(eid == picks[k], pos, 0.0), 0, keepdims=True).astype(jnp.int32)


def _route(h1b, w_router_t, bias_col):
    t = h1b.shape[0]
    tm = LANES
    tri = jnp.asarray(np.arange(tm)[:, None] < np.arange(tm)[None, :], BF16)
    ones = jnp.ones((tm, tm), BF16)
    const = lambda i: (0, 0)
    return pl.pallas_call(
        _route_kernel,
        out_shape=[
            jax.ShapeDtypeStruct((TOP_K, t), jnp.int32),
            jax.ShapeDtypeStruct((TOP_K, t), F32),
            jax.ShapeDtypeStruct((TOP_K, t), jnp.int32),
            jax.ShapeDtypeStruct((N_EXPERTS, tm), F32),
        ],
        grid=(t // tm,),
        in_specs=[
            pl.BlockSpec((tm, D_MODEL), lambda i: (i, 0)),
            pl.BlockSpec((N_EXPERTS, D_MODEL), const),
            pl.BlockSpec((N_EXPERTS, 1), const),
            pl.BlockSpec((tm, tm), const),
            pl.BlockSpec((tm, tm), const),
        ],
        out_specs=[
            pl.BlockSpec((TOP_K, tm), lambda i: (0, i)),
            pl.BlockSpec((TOP_K, tm), lambda i: (0, i)),
            pl.BlockSpec((TOP_K, tm), lambda i: (0, i)),
            pl.BlockSpec((N_EXPERTS, tm), const),
        ],
        compiler_params=pltpu.CompilerParams(
            dimension_semantics=("arbitrary",), vmem_limit_bytes=VMEM_LIMIT),
        name="route",
    )(h1b, w_router_t, bias_col, tri, ones)


def _pow2_chunks(n_max):
    return [1 << b for b in reversed(range(int(n_max).bit_length()))]


def _dispatch_kernel(pstart_ref, fill0_ref, filln_ref, nused_ref, e_ref, rank_ref, h1_ref, zeros_hbm,
                     xs_hbm, dest_ref, rows, row_sem, fill_sem, *, n_blocks):
    i = pl.program_id(0)
    tm = h1_ref.shape[0]
    bm = EXPERT_ROWS

    def fill_copies(e):
        n = filln_ref[e]
        start = fill0_ref[e]
        out = []
        for c in _pow2_chunks(bm - 1):
            off = start + (n // (2 * c)) * (2 * c)
            out.append(((n // c) % 2 == 1,
                        pltpu.make_async_copy(zeros_hbm.at[pl.ds(0, c)], xs_hbm.at[pl.ds(off, c)], fill_sem)))
        return out

    def tail_copy(j):
        return pltpu.make_async_copy(zeros_hbm, xs_hbm.at[pl.ds(j * bm, bm)], fill_sem)

    @pl.when(i == 0)
    def _():
        def start_fill(e, c):
            for pred, cp in fill_copies(e):
                @pl.when(pred)
                def _():
                    cp.start()
            return c
        lax.fori_loop(0, N_EXPERTS, start_fill, 0)
        lax.fori_loop(nused_ref[0], n_blocks, lambda j, c: (tail_copy(j).start(), c)[1], 0)

    for j in range(rows.shape[1]):
        rows[:, j, :] = h1_ref[:, j * LANES:(j + 1) * LANES]

    def issue(t, c):
        for k in range(TOP_K):
            idx = k * tm + t
            d = pstart_ref[e_ref[idx]] + rank_ref[idx]
            dest_ref[idx] = d
            pltpu.make_async_copy(rows.at[t], xs_hbm.at[d], row_sem).start(priority=k % 2)
        return c
    lax.fori_loop(0, tm, issue, 0)
    for _ in range(TOP_K):
        pltpu.make_async_copy(rows, xs_hbm.at[pl.ds(0, tm)], row_sem).wait()

    @pl.when(i == 0)
    def _():
        def wait_fill(e, c):
            for pred, cp in fill_copies(e):
                @pl.when(pred)
                def _():
                    cp.wait()
            return c
        lax.fori_loop(0, N_EXPERTS, wait_fill, 0)
        lax.fori_loop(nused_ref[0], n_blocks, lambda j, c: (tail_copy(j).wait(), c)[1], 0)


def _tile_major(a, tm):
    k, t = a.shape
    return a.reshape(k, t // tm, tm).swapaxes(0, 1).reshape(-1)


def _dispatch_rows(pad_start, fill_start, fill_len, n_used, top_e, rank, h1, n_blocks):
    t, width = h1.shape
    tm = LANES
    row_shape = (width // LANES, LANES)
    smem_tile = pl.BlockSpec((TOP_K * tm,), lambda i, *_: (i,), memory_space=pltpu.SMEM)
    grid_spec = pltpu.PrefetchScalarGridSpec(
        num_scalar_prefetch=4,
        grid=(t // tm,),
        in_specs=[smem_tile, smem_tile, pl.BlockSpec((tm, width), lambda i, *_: (i, 0)),
                  pl.BlockSpec(memory_space=pl.ANY)],
        out_specs=[pl.BlockSpec(memory_space=pl.ANY), smem_tile],
        scratch_shapes=[pltpu.VMEM((tm,) + row_shape, h1.dtype), pltpu.SemaphoreType.DMA, pltpu.SemaphoreType.DMA],
    )
    return pl.pallas_call(
        functools.partial(_dispatch_kernel, n_blocks=n_blocks),
        out_shape=[
            jax.ShapeDtypeStruct((n_blocks * EXPERT_ROWS,) + row_shape, h1.dtype),
            jax.ShapeDtypeStruct((TOP_K * t,), jnp.int32),
        ],
        grid_spec=grid_spec,
        compiler_params=pltpu.CompilerParams(dimension_semantics=("arbitrary",)),
        name="dispatch",
    )(pad_start, fill_start, fill_len, n_used, _tile_major(top_e, tm), _tile_major(rank, tm), h1,
      jnp.zeros((EXPERT_ROWS,) + row_shape, h1.dtype))


def _experts_kernel(be_ref, first_ref, slot_ref, nxt_ref, nused_ref, x_ref, wg_hbm, wu_hbm, wd_hbm, y_ref,
                    wg_buf, wu_buf, wd_buf, wg_bf, wu_bf, wd_bf, sems):
    i = pl.program_id(0)

    def weight_copies(e, slot):
        return (pltpu.make_async_copy(wg_hbm.at[e], wg_buf.at[slot], sems.at[slot, 0]),
                pltpu.make_async_copy(wu_hbm.at[e], wu_buf.at[slot], sems.at[slot, 1]),
                pltpu.make_async_copy(wd_hbm.at[e], wd_buf.at[slot], sems.at[slot, 2]))

    @pl.when(i == 0)
    def _():
        for cp in weight_copies(be_ref[0], 0):
            cp.start(priority=1)

    @pl.when(first_ref[i] == 1)
    def _():
        slot = slot_ref[i]
        for cp in weight_copies(be_ref[i], slot):
            cp.wait()

        @pl.when(nxt_ref[i] >= 0)
        def _():
            for cp in weight_copies(nxt_ref[i], 1 - slot):
                cp.start(priority=1)

        wg_bf[...] = wg_buf[slot].astype(BF16)
        wu_bf[...] = wu_buf[slot].astype(BF16)
        wd_bf[...] = wd_buf[slot].astype(BF16)

    @pl.when(i < nused_ref[0])
    def _():
        x = jnp.concatenate([x_ref[:, j, :] for j in range(x_ref.shape[1])], axis=1).astype(BF16)
        g = jnp.dot(x, wg_bf[...], preferred_element_type=F32)
        u = jnp.dot(x, wu_bf[...], preferred_element_type=F32)
        a = (g * jax.nn.sigmoid(g) * u).astype(BF16)
        y_ref[...] = jnp.dot(a, wd_bf[...], preferred_element_type=F32)

    @pl.when(i >= nused_ref[0])
    def _():
        y_ref[...] = jnp.zeros_like(y_ref)


def _experts(block_e, first, slot, nxt, n_used, xs, w_gate, w_up, w_down):
    p = xs.shape[0]
    bm = EXPERT_ROWS
    n_blocks = p // bm

    def xrow(i, be, fi, sl, nx, nu):
        return (jnp.minimum(i, nu[0] - 1), 0, 0)

    grid_spec = pltpu.PrefetchScalarGridSpec(
        num_scalar_prefetch=5,
        grid=(n_blocks,),
        in_specs=[
            pl.BlockSpec((bm,) + xs.shape[1:], xrow),
            pl.BlockSpec(memory_space=pl.ANY),
            pl.BlockSpec(memory_space=pl.ANY),
            pl.BlockSpec(memory_space=pl.ANY),
        ],
        out_specs=pl.BlockSpec((bm, D_MODEL), lambda i, *_: (i, 0)),
        scratch_shapes=[
            pltpu.VMEM((2, D_MODEL, D_EXPERT), F32),
            pltpu.VMEM((2, D_MODEL, D_EXPERT), F32),
            pltpu.VMEM((2, D_EXPERT, D_MODEL), F32),
            pltpu.VMEM((D_MODEL, D_EXPERT), BF16),
            pltpu.VMEM((D_MODEL, D_EXPERT), BF16),
            pltpu.VMEM((D_EXPERT, D_MODEL), BF16),
            pltpu.SemaphoreType.DMA((2, 3)),
        ],
    )
    return pl.pallas_call(
        _experts_kernel,
        out_shape=jax.ShapeDtypeStruct((p, D_MODEL), F32),
        grid_spec=grid_spec,
        compiler_params=pltpu.CompilerParams(
            dimension_semantics=("arbitrary",), vmem_limit_bytes=VMEM_LIMIT),
        name="experts",
    )(block_e, first, slot, nxt, n_used, xs, w_gate, w_up, w_down)


def _combine_kernel(dest_ref, gate_ref, yb_hbm, o_ref, buf, sem):
    tm = o_ref.shape[0]

    def issue(tb, c):
        base = pl.multiple_of(tb * SUBLANES, SUBLANES)
        for r in range(SUBLANES):
            for k in range(TOP_K):
                idx = k * tm + base + r
                pltpu.make_async_copy(yb_hbm.at[pl.ds(dest_ref[idx], 1), :], buf.at[pl.ds(idx, 1), :],
                                      sem).start(priority=k % 2)
        return c
    lax.fori_loop(0, tm // SUBLANES, issue, 0)
    pltpu.make_async_copy(yb_hbm.at[pl.ds(0, TOP_K * tm), :], buf, sem).wait()

    acc = gate_ref[:, 0:1] * buf[0:tm, :]
    for k in range(1, TOP_K):
        acc = acc + gate_ref[:, k:k + 1] * buf[k * tm:(k + 1) * tm, :]
    o_ref[...] = acc


def _combine(dest, gate_tk, yb):
    t = gate_tk.shape[0]
    tm = LANES
    return pl.pallas_call(
        _combine_kernel,
        out_shape=jax.ShapeDtypeStruct((t, D_MODEL), F32),
        grid=(t // tm,),
        in_specs=[
            pl.BlockSpec((TOP_K * tm,), lambda i: (i,), memory_space=pltpu.SMEM),
            pl.BlockSpec((tm, TOP_K), lambda i: (i, 0)),
            pl.BlockSpec(memory_space=pl.ANY),
        ],
        out_specs=pl.BlockSpec((tm, D_MODEL), lambda i: (i, 0)),
        scratch_shapes=[pltpu.VMEM((TOP_K * tm, D_MODEL), F32), pltpu.SemaphoreType.DMA],
        compiler_params=pltpu.CompilerParams(
            dimension_semantics=("arbitrary",), vmem_limit_bytes=VMEM_LIMIT),
        name="combine",
    )(dest, gate_tk, yb)


def _final_kernel(h1_ref, h1b_ref, routed_ref, p_ref, wsg_ref, wsu_ref, wsd_ref, wpg_ref, wp_ref,
                  g2_ref, b2_ref, y_ref):
    hb = h1b_ref[...]
    g = jnp.dot(hb, wsg_ref[...], preferred_element_type=F32)
    u = jnp.dot(hb, wsu_ref[...], preferred_element_type=F32)
    shared = jnp.dot((g * jax.nn.sigmoid(g) * u).astype(BF16), wsd_ref[...], preferred_element_type=F32)
    gate = jax.nn.sigmoid(jnp.dot(hb, wpg_ref[...], preferred_element_type=F32))
    ple = gate * jnp.dot(p_ref[...], wp_ref[...], preferred_element_type=F32)
    r = DEEPNORM_ALPHA * h1_ref[...] + (routed_ref[...] + shared) + ple
    y_ref[...] = _layer_norm(r, g2_ref[...], b2_ref[...])


def _final(h1, h1b, routed, p_all, ws_gate, ws_up, ws_down, w_ple_gate, w_ple, g2, b2):
    t = h1.shape[0]
    tm = ROW_TILE // 2
    const = lambda i: (0, 0)
    resident = functools.partial(pl.BlockSpec, index_map=const, pipeline_mode=pl.Buffered(1))
    return pl.pallas_call(
        _final_kernel,
        out_shape=jax.ShapeDtypeStruct((t, D_MODEL), F32),
        grid=(t // tm,),
        in_specs=[
            pl.BlockSpec((tm, D_MODEL), lambda i: (i, 0)),
            pl.BlockSpec((tm, D_MODEL), lambda i: (i, 0)),
            pl.BlockSpec((tm, D_MODEL), lambda i: (i, 0)),
            pl.BlockSpec((tm, PLE_DIM), lambda i: (i, 0)),
            resident((D_MODEL, D_SHARED)),
            resident((D_MODEL, D_SHARED)),
            resident((D_SHARED, D_MODEL)),
            resident((D_MODEL, D_MODEL)),
            resident((PLE_DIM, D_MODEL)),
            pl.BlockSpec((1, D_MODEL), const),
            pl.BlockSpec((1, D_MODEL), const),
        ],
        out_specs=pl.BlockSpec((tm, D_MODEL), lambda i: (i, 0)),
        compiler_params=pltpu.CompilerParams(
            dimension_semantics=("arbitrary",), vmem_limit_bytes=VMEM_LIMIT),
        name="final",
    )(h1, h1b, routed, p_all, ws_gate, ws_up, ws_down, w_ple_gate, w_ple, g2, b2)


def _rope_tables(pos):
    half = ROT_DIM // 2
    inv_freq = ROPE_THETA ** (-jnp.arange(half, dtype=F32) * 2.0 / ROT_DIM)
    ang = pos.astype(F32)[:, None] * inv_freq[None, :]
    cos, sin = jnp.cos(ang), jnp.sin(ang)
    n = pos.shape[0]
    ones = jnp.ones((n, HEAD_DIM - ROT_DIM), F32)
    zeros = jnp.zeros((n, HEAD_DIM - ROT_DIM), F32)
    zh = jnp.zeros((n, half), F32)
    cos_t = jnp.concatenate([cos, cos, ones], -1)
    sin_lo = jnp.concatenate([-sin, zh, zeros], -1)
    sin_hi = jnp.concatenate([zh, sin, zeros], -1)
    per_head = jnp.stack([cos_t, sin_lo, sin_hi])
    return jnp.concatenate([per_head] * (LANES // HEAD_DIM), -1)


def _block_plan(counts, n_blocks):
    bm = EXPERT_ROWS
    i32 = jnp.int32
    padded = (counts + bm - 1) // bm * bm
    pad_end = jnp.cumsum(padded)
    pad_start = pad_end - padded
    n_used = pad_end[-1] // bm
    blk = jnp.arange(n_blocks)
    valid = blk < n_used
    block_e = jnp.minimum(jnp.sum(pad_end[None, :] <= (blk * bm)[:, None], axis=1), N_EXPERTS - 1)
    block_e = jnp.where(valid, block_e, block_e[jnp.maximum(n_used - 1, 0)])
    prev_e = jnp.concatenate([jnp.full((1,), -1, block_e.dtype), block_e[:-1]])
    first = valid & (block_e != prev_e)
    run = jnp.cumsum(first) - 1
    run_expert = jnp.full((n_blocks + 2,), -1, block_e.dtype).at[jnp.where(first, run, n_blocks + 1)].set(block_e)
    nxt = jnp.where(first, run_expert[jnp.minimum(run + 1, n_blocks)], -1)
    return (pad_start.astype(i32), (pad_start + counts).astype(i32), (padded - counts).astype(i32),
            n_used.astype(i32).reshape(1), block_e.astype(i32), first.astype(i32), (run % 2).astype(i32),
            nxt.astype(i32))


def kernel(x_prompt, x_sample, p_prompt, p_sample, cache_k, cache_v, state_conv, ln_in_g, ln_in_b, w_in,
           attn_sinks, conv_w, gn_attn, gn_conv, w_out, ln1_g, ln1_b, w_router, router_bias, w_gate, w_up,
           w_down, ws_gate, ws_up, ws_down, w_ple_gate, w_ple, ln2_g, ln2_b):
    n_batch, seq, d = x_prompt.shape
    n_dec, dec_seq, _ = x_sample.shape
    depth = w_in.shape[0]
    wb = cache_k.shape[2]
    assert depth == 1 and dec_seq == 1 and d == D_MODEL
    assert wb == ATTN_BLOCK and seq % ATTN_BLOCK == 0 and n_dec % SAMPLE_CHUNK == 0
    t_p = n_batch * seq
    t = t_p + n_dec
    assert t % ROW_TILE == 0 and t_p % SAMPLE_CHUNK == 0

    x_all = jnp.concatenate([x_prompt.reshape(t_p, d), x_sample.reshape(n_dec, d)], 0)
    p_all = jnp.concatenate([p_prompt[0].reshape(t_p, PLE_DIM), p_sample[0].reshape(n_dec, PLE_DIM)], 0).astype(BF16)
    row = lambda v: v.reshape(1, -1).astype(F32)

    w_in0 = w_in[0]
    qkv_w = ATTN_WIDTH + 2 * KV_WIDTH
    w_in_perm = jnp.concatenate([w_in0[:, qkv_w:], w_in0[:, :qkv_w]], axis=1).astype(BF16)

    proj = _ln_inproj(x_all, row(ln_in_g), row(ln_in_b), w_in_perm)

    rope_p = _rope_tables(jnp.arange(seq))
    rope_s = _rope_tables(PAST_LEN + jnp.arange(dec_seq))[:, 0, :]
    sinks = attn_sinks[0].astype(F32)
    cw, ga, gc = conv_w[0].astype(F32), row(gn_attn[0]), row(gn_conv[0])
    mixed_p, kwin_p, vwin_p, cstate_p = _prompt_mixer(proj, sinks, rope_p, cw, ga, gc, n_batch, seq)
    mixed_s, kwin_s, vwin_s, cstate_s = _sample_mixer(
        proj, t_p, n_dec, sinks, cache_k[0].reshape(n_dec, wb, KV_WIDTH), cache_v[0].reshape(n_dec, wb, KV_WIDTH),
        jnp.swapaxes(state_conv[0], 0, 1), rope_s, cw, ga, gc)
    cstate_s = jnp.swapaxes(cstate_s, 0, 1)
    mixed = jnp.concatenate([mixed_p, mixed_s], 0)

    h1, h1b = _outproj(
        mixed, x_all, row(ln_in_g), row(ln_in_b), w_out[0].astype(BF16), row(ln1_g[0]), row(ln1_b[0]))

    top_e, gate, rank, cnt = _route(h1b, w_router[0].T.astype(BF16), router_bias[0].astype(F32).reshape(N_EXPERTS, 1))
    n_blocks = t * TOP_K // EXPERT_ROWS + N_EXPERTS
    pad_start, fill_start, fill_len, n_used, block_e, first, slot, nxt = _block_plan(cnt[:, 0].astype(jnp.int32), n_blocks)
    xs, dest = _dispatch_rows(pad_start, fill_start, fill_len, n_used, top_e, rank, h1, n_blocks)
    yb = _experts(block_e, first, slot, nxt, n_used, xs, w_gate[0], w_up[0], w_down[0])
    routed = _combine(dest, gate.T, yb)

    y = _final(h1, h1b, routed, p_all, ws_gate[0].astype(BF16), ws_up[0].astype(BF16), ws_down[0].astype(BF16),
               w_ple_gate[0].astype(BF16), w_ple[0].astype(BF16), row(ln2_g[0]), row(ln2_b[0]))

    kv_shape = (1, -1, wb, N_KV_HEADS, HEAD_DIM)
    return (y[:t_p].reshape(n_batch, seq, d), y[t_p:].reshape(n_dec, dec_seq, d),
            kwin_p.reshape(kv_shape), vwin_p.reshape(kv_shape), cstate_p[None],
            kwin_s.reshape(kv_shape), vwin_s.reshape(kv_shape), cstate_s[None])
```

```python
import functools

import numpy as np
import jax
import jax.numpy as jnp
from jax import lax
from jax.experimental import pallas as pl
from jax.experimental.pallas import tpu as pltpu

D_MODEL = 2048
N_HEADS = 16
N_KV_HEADS = 4
HEAD_DIM = 64
GQA_GROUP = N_HEADS // N_KV_HEADS
ATTN_WIDTH = N_HEADS * HEAD_DIM
KV_WIDTH = N_KV_HEADS * HEAD_DIM
WINDOW = 128
ATTN_BLOCK = WINDOW
ROT_DIM = HEAD_DIM // 4
ROPE_THETA = 500000.0
CONV_WIDTH = D_MODEL - ATTN_WIDTH
CONV_K = 3
IN_COLS = ATTN_WIDTH + 2 * KV_WIDTH + 3 * CONV_WIDTH
N_EXPERTS = 256
TOP_K = 8
N_EXPERT_GROUPS = 8
TOPK_GROUPS = 4
D_EXPERT = 512
D_SHARED = 512
ROUTED_SCALE = 2.5
PLE_DIM = 256
LN_EPS = 1e-5
DEPTH = 1
PAST_LEN = 8192
DEEPNORM_ALPHA = (2 * DEPTH) ** 0.25

LANES = 128
SUBLANES = 8
COL_B, COL_C, COL_X, COL_Q, COL_KV = 0, CONV_WIDTH, 2 * CONV_WIDTH, 3 * CONV_WIDTH, 3 * CONV_WIDTH + ATTN_WIDTH

ROW_TILE = 640
INPROJ_COL_TILE = 1536
FINAL_TILE = 256
EXPERT_ROWS = 128
SAMPLE_CHUNK = 8
VMEM_LIMIT = 56 * 1024 * 1024

BF16 = jnp.bfloat16
F32 = jnp.float32


def _layer_norm(x, g, b):
    xc = x - jnp.mean(x, -1, keepdims=True)
    var = jnp.mean(xc * xc, -1, keepdims=True)
    return xc * lax.rsqrt(var + LN_EPS) * g + b


def _rms_norm(x, g):
    return x * lax.rsqrt(jnp.mean(x * x, -1, keepdims=True) + LN_EPS) * g


def _rope_lanes(x, cos_t, sin_lo, sin_hi):
    half = ROT_DIM // 2
    ax = x.ndim - 1
    return x * cos_t + pltpu.roll(x, LANES - half, ax) * sin_lo + pltpu.roll(x, half, ax) * sin_hi


def _rope_wide(x, cos_t, sin_lo, sin_hi):
    n = x.shape[-1] // LANES
    return jnp.concatenate(
        [_rope_lanes(x[..., c * LANES:(c + 1) * LANES], cos_t, sin_lo, sin_hi) for c in range(n)], axis=-1)


def _ln_inproj_kernel(x_ref, g_ref, b_ref, w_ref, o_ref, h_scr):
    @pl.when(pl.program_id(1) == 0)
    def _():
        h_scr[...] = _layer_norm(x_ref[...], g_ref[...], b_ref[...]).astype(BF16)

    o_ref[...] = jnp.dot(h_scr[...], w_ref[...], preferred_element_type=F32)


def _ln_inproj(x_all, g, b, w_bf16):
    t = x_all.shape[0]
    return pl.pallas_call(
        _ln_inproj_kernel,
        out_shape=jax.ShapeDtypeStruct((t, IN_COLS), F32),
        grid=(t // ROW_TILE, IN_COLS // INPROJ_COL_TILE),
        in_specs=[
            pl.BlockSpec((ROW_TILE, D_MODEL), lambda i, j: (i, 0)),
            pl.BlockSpec((1, D_MODEL), lambda i, j: (0, 0)),
            pl.BlockSpec((1, D_MODEL), lambda i, j: (0, 0)),
            pl.BlockSpec((D_MODEL, INPROJ_COL_TILE), lambda i, j: (0, j)),
        ],
        out_specs=pl.BlockSpec((ROW_TILE, INPROJ_COL_TILE), lambda i, j: (i, j)),
        scratch_shapes=[pltpu.VMEM((ROW_TILE, D_MODEL), BF16)],
        compiler_params=pltpu.CompilerParams(
            dimension_semantics=("arbitrary", "arbitrary"), vmem_limit_bytes=VMEM_LIMIT),
        name="ln_inproj",
    )(x_all, g, b, w_bf16)


def _prompt_mixer_kernel(sink_ref, b_ref, c_ref, x_ref, q_ref, kv_ref, kvp_ref, cp_ref, xp_ref,
                         rope_ref, ropep_ref, convw_ref, gna_ref, gnc_ref,
                         mixed_ref, kwin_ref, vwin_ref, cstate_ref, *, n_blocks):
    i = pl.program_id(1)
    blk = ATTN_BLOCK
    cos_t, sin_lo, sin_hi = rope_ref[0], rope_ref[1], rope_ref[2]
    qr = _rope_wide(q_ref[...], cos_t, sin_lo, sin_hi) * (HEAD_DIM ** -0.5)
    k_cur = _rope_wide(kv_ref[:, :KV_WIDTH], cos_t, sin_lo, sin_hi)
    k_prev = _rope_wide(kvp_ref[:, :KV_WIDTH], ropep_ref[0], ropep_ref[1], ropep_ref[2])
    v_cur = kv_ref[:, KV_WIDTH:]
    v_prev = kvp_ref[:, KV_WIDTH:]
    k_all = jnp.concatenate([k_prev, k_cur], axis=0).astype(BF16)
    v_all = jnp.concatenate([v_prev, v_cur], axis=0).astype(BF16)

    qi = lax.broadcasted_iota(jnp.int32, (blk, 2 * blk), 0)
    sj = lax.broadcasted_iota(jnp.int32, (blk, 2 * blk), 1)
    rel = qi + blk - sj
    mask = (rel >= 0) & (rel <= WINDOW) & ((sj >= blk) | (i > 0))
    mask = jnp.concatenate([mask] * GQA_GROUP, axis=0)

    heads = []
    for g in range(N_KV_HEADS):
        qg = jnp.concatenate(
            [qr[:, (g * GQA_GROUP + hh) * HEAD_DIM:(g * GQA_GROUP + hh + 1) * HEAD_DIM] for hh in range(GQA_GROUP)],
            axis=0).astype(BF16)
        kg = k_all[:, g * HEAD_DIM:(g + 1) * HEAD_DIM]
        vg = v_all[:, g * HEAD_DIM:(g + 1) * HEAD_DIM]
        s = lax.dot_general(qg, kg, (((1,), (1,)), ((), ())), preferred_element_type=F32)
        s = jnp.where(mask, s, -jnp.inf)
        sink = jnp.concatenate(
            [jnp.full((blk, 1), sink_ref[g * GQA_GROUP + hh], F32) for hh in range(GQA_GROUP)], axis=0)
        m = jnp.maximum(jnp.max(s, -1, keepdims=True), sink)
        e = jnp.exp(s - m)
        p = e / (jnp.sum(e, -1, keepdims=True) + jnp.exp(sink - m))
        og = jnp.dot(p.astype(BF16), vg, preferred_element_type=F32)
        heads += [og[hh * blk:(hh + 1) * blk] for hh in range(GQA_GROUP)]
    attn = jnp.concatenate(heads, axis=-1)
    mixed_ref[:, :ATTN_WIDTH] = _rms_norm(attn, gna_ref[...]).astype(BF16)

    u = c_ref[...] * x_ref[...]
    n_tail = CONV_K - 1
    tail_rows = cp_ref.shape[0]
    row_id = lax.broadcasted_iota(jnp.int32, u.shape, 0)
    acc = u * convw_ref[CONV_K - 1:CONV_K, :]
    for d in range(1, CONV_K):
        ud = pltpu.roll(u, d, 0)
        for r in range(d):
            src = tail_rows - d + r
            up = jnp.where(i > 0, cp_ref[src:src + 1, :] * xp_ref[src:src + 1, :], 0.0)
            ud = jnp.where(row_id == r, up, ud)
        acc = acc + ud * convw_ref[CONV_K - 1 - d:CONV_K - d, :]
    mixed_ref[:, ATTN_WIDTH:] = _rms_norm(b_ref[...] * acc, gnc_ref[...]).astype(BF16)

    @pl.when(i == n_blocks - 1)
    def _():
        kwin_ref[0] = k_cur
        vwin_ref[0] = v_cur
        cstate_ref[0] = u[blk - n_tail:, :]


def _prompt_mixer(proj, sinks, rope_tab, conv_w, gn_attn, gn_conv, n_batch, seq):
    blk = ATTN_BLOCK
    nb = seq // blk
    tail = 8
    wide = CONV_WIDTH // 1
    kvb = COL_KV // (2 * KV_WIDTH)

    def row(b, i, s):
        return b * nb + i

    def prev_row(b, i, s):
        return b * nb + jnp.maximum(i - 1, 0)

    def prev_tail(b, i, s):
        return jnp.maximum((b * nb + i) * (blk // tail) - 1, 0)

    grid_spec = pltpu.PrefetchScalarGridSpec(
        num_scalar_prefetch=1,
        grid=(n_batch, nb),
        in_specs=[
            pl.BlockSpec((blk, wide), lambda b, i, s: (row(b, i, s), COL_B // wide)),
            pl.BlockSpec((blk, wide), lambda b, i, s: (row(b, i, s), COL_C // wide)),
            pl.BlockSpec((blk, wide), lambda b, i, s: (row(b, i, s), COL_X // wide)),
            pl.BlockSpec((blk, ATTN_WIDTH), lambda b, i, s: (row(b, i, s), COL_Q // ATTN_WIDTH)),
            pl.BlockSpec((blk, 2 * KV_WIDTH), lambda b, i, s: (row(b, i, s), kvb)),
            pl.BlockSpec((blk, 2 * KV_WIDTH), lambda b, i, s: (prev_row(b, i, s), kvb)),
            pl.BlockSpec((tail, wide), lambda b, i, s: (prev_tail(b, i, s), COL_C // wide)),
            pl.BlockSpec((tail, wide), lambda b, i, s: (prev_tail(b, i, s), COL_X // wide)),
            pl.BlockSpec((3, blk, LANES), lambda b, i, s: (0, i, 0)),
            pl.BlockSpec((3, blk, LANES), lambda b, i, s: (0, jnp.maximum(i - 1, 0), 0)),
            pl.BlockSpec((CONV_K, CONV_WIDTH), lambda b, i, s: (0, 0)),
            pl.BlockSpec((1, ATTN_WIDTH), lambda b, i, s: (0, 0)),
            pl.BlockSpec((1, CONV_WIDTH), lambda b, i, s: (0, 0)),
        ],
        out_specs=[
            pl.BlockSpec((blk, D_MODEL), lambda b, i, s: (row(b, i, s), 0)),
            pl.BlockSpec((1, blk, KV_WIDTH), lambda b, i, s: (b, 0, 0)),
            pl.BlockSpec((1, blk, KV_WIDTH), lambda b, i, s: (b, 0, 0)),
            pl.BlockSpec((1, CONV_K - 1, CONV_WIDTH), lambda b, i, s: (b, 0, 0)),
        ],
    )
    return pl.pallas_call(
        functools.partial(_prompt_mixer_kernel, n_blocks=nb),
        out_shape=[
            jax.ShapeDtypeStruct((n_batch * seq, D_MODEL), BF16),
            jax.ShapeDtypeStruct((n_batch, blk, KV_WIDTH), F32),
            jax.ShapeDtypeStruct((n_batch, blk, KV_WIDTH), F32),
            jax.ShapeDtypeStruct((n_batch, CONV_K - 1, CONV_WIDTH), F32),
        ],
        grid_spec=grid_spec,
        compiler_params=pltpu.CompilerParams(
            dimension_semantics=("arbitrary", "arbitrary"), vmem_limit_bytes=VMEM_LIMIT),
        name="prompt_mixer",
    )(sinks, proj, proj, proj, proj, proj, proj, proj, proj, rope_tab, rope_tab, conv_w, gn_attn, gn_conv)


def _expand_groups(t):
    ax = t.ndim - 1
    lane = lax.broadcasted_iota(jnp.int32, t.shape[:-1] + (LANES,), ax)
    chunks = []
    for c in range(KV_WIDTH // LANES):
        a = t[..., c * LANES:(c + 1) * LANES]
        r = pltpu.roll(a, HEAD_DIM, ax)
        lo = jnp.where(lane < HEAD_DIM, a, r)
        hi = jnp.where(lane < HEAD_DIM, r, a)
        chunks += [lo] * (GQA_GROUP // 2) + [hi] * (GQA_GROUP // 2)
    return jnp.concatenate(chunks, axis=-1)


def _sample_mixer_kernel(sink_ref, b_ref, c_ref, x_ref, q_ref, kv_ref, ck_ref, cv_ref, st_ref,
                         rope_ref, seg_ref, segt_ref, convw_ref, gna_ref, gnc_ref,
                         mixed_ref, kwin_ref, vwin_ref, cstate_ref):
    nb, wb = ck_ref.shape[0], ck_ref.shape[1]
    cos_t, sin_lo, sin_hi = rope_ref[0:1, :], rope_ref[1:2, :], rope_ref[2:3, :]
    qr = _rope_wide(q_ref[...], cos_t, sin_lo, sin_hi) * (HEAD_DIM ** -0.5)
    k_new = _rope_wide(kv_ref[:, :KV_WIDTH], cos_t, sin_lo, sin_hi)
    v_new = kv_ref[:, KV_WIDTH:]
    ck = ck_ref[...]
    cv = cv_ref[...]

    seg = seg_ref[...]
    prod = _expand_groups(ck) * qr[:, None, :]
    s_old = jnp.dot(prod.reshape(nb * wb, ATTN_WIDTH).astype(BF16), seg,
                    preferred_element_type=F32).reshape(nb, wb, N_HEADS)
    s_new = jnp.dot((_expand_groups(k_new) * qr).astype(BF16), seg, preferred_element_type=F32)
    sink = sink_ref[...]
    m = jnp.maximum(jnp.maximum(jnp.max(s_old, axis=1), s_new), sink)
    e_old = jnp.exp(s_old - m[:, None, :])
    e_new = jnp.exp(s_new - m)
    inv = 1.0 / (jnp.sum(e_old, axis=1) + e_new + jnp.exp(sink - m))
    p_old = (e_old * inv[:, None, :]).astype(BF16)
    p_new = (e_new * inv).astype(BF16)
    segt = segt_ref[...]
    pe_old = jnp.dot(p_old.reshape(nb * wb, N_HEADS), segt, preferred_element_type=F32).reshape(nb, wb, ATTN_WIDTH)
    pe_new = jnp.dot(p_new, segt, preferred_element_type=F32)
    attn = jnp.sum(pe_old * _expand_groups(cv), axis=1) + pe_new * _expand_groups(v_new)
    mixed_ref[:, :ATTN_WIDTH] = _rms_norm(attn, gna_ref[...]).astype(BF16)

    u = c_ref[...] * x_ref[...]
    acc = u * convw_ref[CONV_K - 1:CONV_K, :]
    for j in range(CONV_K - 1):
        acc = acc + st_ref[j] * convw_ref[j:j + 1, :]
    mixed_ref[:, ATTN_WIDTH:] = _rms_norm(b_ref[...] * acc, gnc_ref[...]).astype(BF16)

    kwin_ref[:, 0:wb - 1, :] = ck_ref[:, 1:wb, :]
    vwin_ref[:, 0:wb - 1, :] = cv_ref[:, 1:wb, :]
    for n in range(nb):
        kwin_ref[n, wb - 1:wb, :] = k_new[n:n + 1, :]
        vwin_ref[n, wb - 1:wb, :] = v_new[n:n + 1, :]
    for j in range(CONV_K - 2):
        cstate_ref[j] = st_ref[j + 1]
    cstate_ref[CONV_K - 2] = u


def _sample_mixer(proj, row0, n_dec, sinks, cache_k, cache_v, state_conv, rope_row, conv_w, gn_attn, gn_conv):
    nb = SAMPLE_CHUNK
    wb = cache_k.shape[1]
    r0 = row0 // nb
    wide = CONV_WIDTH
    head_of_lane = np.arange(ATTN_WIDTH) // HEAD_DIM
    seg = jnp.asarray(head_of_lane[:, None] == np.arange(N_HEADS)[None, :], BF16)
    kvb = COL_KV // (2 * KV_WIDTH)
    return pl.pallas_call(
        _sample_mixer_kernel,
        out_shape=[
            jax.ShapeDtypeStruct((n_dec, D_MODEL), BF16),
            jax.ShapeDtypeStruct((n_dec, wb, KV_WIDTH), F32),
            jax.ShapeDtypeStruct((n_dec, wb, KV_WIDTH), F32),
            jax.ShapeDtypeStruct((CONV_K - 1, n_dec, CONV_WIDTH), F32),
        ],
        grid=(n_dec // nb,),
        in_specs=[
            pl.BlockSpec((1, N_HEADS), lambda i: (0, 0)),
            pl.BlockSpec((nb, wide), lambda i: (r0 + i, COL_B // wide)),
            pl.BlockSpec((nb, wide), lambda i: (r0 + i, COL_C // wide)),
            pl.BlockSpec((nb, wide), lambda i: (r0 + i, COL_X // wide)),
            pl.BlockSpec((nb, ATTN_WIDTH), lambda i: (r0 + i, COL_Q // ATTN_WIDTH)),
            pl.BlockSpec((nb, 2 * KV_WIDTH), lambda i: (r0 + i, kvb)),
            pl.BlockSpec((nb, wb, KV_WIDTH), lambda i: (i, 0, 0)),
            pl.BlockSpec((nb, wb, KV_WIDTH), lambda i: (i, 0, 0)),
            pl.BlockSpec((CONV_K - 1, nb, CONV_WIDTH), lambda i: (0, i, 0)),
            pl.BlockSpec((3, LANES), lambda i: (0, 0)),
            pl.BlockSpec((ATTN_WIDTH, N_HEADS), lambda i: (0, 0)),
            pl.BlockSpec((N_HEADS, ATTN_WIDTH), lambda i: (0, 0)),
            pl.BlockSpec((CONV_K, CONV_WIDTH), lambda i: (0, 0)),
            pl.BlockSpec((1, ATTN_WIDTH), lambda i: (0, 0)),
            pl.BlockSpec((1, CONV_WIDTH), lambda i: (0, 0)),
        ],
        out_specs=[
            pl.BlockSpec((nb, D_MODEL), lambda i: (i, 0)),
            pl.BlockSpec((nb, wb, KV_WIDTH), lambda i: (i, 0, 0)),
            pl.BlockSpec((nb, wb, KV_WIDTH), lambda i: (i, 0, 0)),
            pl.BlockSpec((CONV_K - 1, nb, CONV_WIDTH), lambda i: (0, i, 0)),
        ],
        compiler_params=pltpu.CompilerParams(
            dimension_semantics=("arbitrary",), vmem_limit_bytes=VMEM_LIMIT),
        name="sample_mixer",
    )(sinks.reshape(1, N_HEADS), proj, proj, proj, proj, proj, cache_k, cache_v, state_conv,
      rope_row, seg, seg.T, conv_w, gn_attn, gn_conv)


def _outproj_kernel(mixed_ref, x_ref, g0_ref, b0_ref, wout_ref, g1_ref, b1_ref, h1_ref, h1b_ref):
    h = _layer_norm(x_ref[...], g0_ref[...], b0_ref[...])
    mixed = jnp.dot(mixed_ref[...], wout_ref[...], preferred_element_type=F32)
    h1 = _layer_norm(DEEPNORM_ALPHA * h + mixed, g1_ref[...], b1_ref[...])
    h1_ref[...] = h1
    h1b_ref[...] = h1.astype(BF16)


def _outproj(mixed, x_all, g0, b0, w_out, g1, b1):
    t = x_all.shape[0]
    tm = ROW_TILE // 2
    const = lambda i: (0, 0)
    resident = functools.partial(pl.BlockSpec, index_map=const, pipeline_mode=pl.Buffered(1))
    return pl.pallas_call(
        _outproj_kernel,
        out_shape=[
            jax.ShapeDtypeStruct((t, D_MODEL), F32),
            jax.ShapeDtypeStruct((t, D_MODEL), BF16),
        ],
        grid=(t // tm,),
        in_specs=[
            pl.BlockSpec((tm, D_MODEL), lambda i: (i, 0)),
            pl.BlockSpec((tm, D_MODEL), lambda i: (i, 0)),
            pl.BlockSpec((1, D_MODEL), const),
            pl.BlockSpec((1, D_MODEL), const),
            resident((D_MODEL, D_MODEL)),
            pl.BlockSpec((1, D_MODEL), const),
            pl.BlockSpec((1, D_MODEL), const),
        ],
        out_specs=[
            pl.BlockSpec((tm, D_MODEL), lambda i: (i, 0)),
            pl.BlockSpec((tm, D_MODEL), lambda i: (i, 0)),
        ],
        compiler_params=pltpu.CompilerParams(
            dimension_semantics=("arbitrary",), vmem_limit_bytes=VMEM_LIMIT),
        name="outproj",
    )(mixed, x_all, g0, b0, w_out, g1, b1)


def _route_kernel(h_ref, wrt_ref, bias_ref, tri_ref, ones_ref, e_ref, gate_ref, rank_ref, cnt_ref):
    tm = h_ref.shape[0]
    neg = -jnp.inf
    big = float(N_EXPERTS)

    @pl.when(pl.program_id(0) == 0)
    def _():
        cnt_ref[...] = jnp.zeros_like(cnt_ref)

    logits = lax.dot_general(wrt_ref[...], h_ref[...], (((1,), (1,)), ((), ())), preferred_element_type=F32)
    scores = jax.nn.sigmoid(logits)
    biased = scores + bias_ref[...]

    per_group = N_EXPERTS // N_EXPERT_GROUPS
    rid = lax.broadcasted_iota(jnp.int32, (per_group, tm), 0).astype(F32)
    gscore = []
    for g in range(N_EXPERT_GROUPS):
        xg = biased[g * per_group:(g + 1) * per_group]
        m1 = jnp.max(xg, 0, keepdims=True)
        i1 = jnp.min(jnp.where(xg == m1, rid, big), 0, keepdims=True)
        m2 = jnp.max(jnp.where(rid == i1, neg, xg), 0, keepdims=True)
        gscore.append(m1 + m2)
    masked = []
    for g in range(N_EXPERT_GROUPS):
        beaten = jnp.zeros((1, tm), F32)
        for o in range(N_EXPERT_GROUPS):
            if o != g:
                wins = (gscore[o] > gscore[g]) | ((gscore[o] == gscore[g]) & (o < g))
                beaten = beaten + jnp.where(wins, 1.0, 0.0)
        masked.append(jnp.where(beaten < TOPK_GROUPS, biased[g * per_group:(g + 1) * per_group], neg))
    x = jnp.concatenate(masked, axis=0)

    eid = lax.broadcasted_iota(jnp.int32, (N_EXPERTS, tm), 0).astype(F32)
    sel = jnp.zeros((N_EXPERTS, tm), F32)
    picks, gates = [], []
    for _ in range(TOP_K):
        m = jnp.max(x, 0, keepdims=True)
        ik = jnp.min(jnp.where(x == m, eid, big), 0, keepdims=True)
        hit = eid == ik
        gates.append(jnp.sum(jnp.where(hit, scores, 0.0), 0, keepdims=True))
        picks.append(ik)
        x = jnp.where(hit, neg, x)
        sel = jnp.where(hit, 1.0, sel)

    selb = sel.astype(BF16)
    pos = cnt_ref[...] + jnp.dot(selb, tri_ref[...], preferred_element_type=F32)
    cnt_ref[...] = cnt_ref[...] + jnp.dot(selb, ones_ref[...], preferred_element_type=F32)

    gsum = gates[0]
    for k in range(1, TOP_K):
        gsum = gsum + gates[k]
    for k in range(TOP_K):
        e_ref[k:k + 1, :] = picks[k].astype(jnp.int32)
        gate_ref[k:k + 1, :] = gates[k] / gsum * ROUTED_SCALE
        rank_ref[k:k + 1, :] = jnp.sum(jnp.where(eid == picks[k], pos, 0.0), 0, keepdims=True).astype(jnp.int32)


def _route(h1b, w_router_t, bias_col):
    t = h1b.shape[0]
    tm = LANES
    tri = jnp.asarray(np.arange(tm)[:, None] < np.arange(tm)[None, :], BF16)
    ones = jnp.ones((tm, tm), BF16)
    const = lambda i: (0, 0)
    return pl.pallas_call(
        _route_kernel,
        out_shape=[
            jax.ShapeDtypeStruct((TOP_K, t), jnp.int32),
            jax.ShapeDtypeStruct((TOP_K, t), F32),
            jax.ShapeDtypeStruct((TOP_K, t), jnp.int32),
            jax.ShapeDtypeStruct((N_EXPERTS, tm), F32),
        ],
        grid=(t // tm,),
        in_specs=[
            pl.BlockSpec((tm, D_MODEL), lambda i: (i, 0)),
            pl.BlockSpec((N_EXPERTS, D_MODEL), const),
            pl.BlockSpec((N_EXPERTS, 1), const),
            pl.BlockSpec((tm, tm), const),
            pl.BlockSpec((tm, tm), const),
        ],
        out_specs=[
            pl.BlockSpec((TOP_K, tm), lambda i: (0, i)),
            pl.BlockSpec((TOP_K, tm), lambda i: (0, i)),
            pl.BlockSpec((TOP_K, tm), lambda i: (0, i)),
            pl.BlockSpec((N_EXPERTS, tm), const),
        ],
        compiler_params=pltpu.CompilerParams(
            dimension_semantics=("arbitrary",), vmem_limit_bytes=VMEM_LIMIT),
        name="route",
    )(h1b, w_router_t, bias_col, tri, ones)


def _pow2_chunks(n_max):
    return [1 << b for b in reversed(range(int(n_max).bit_length()))]


def _dispatch_kernel(pstart_ref, fill0_ref, filln_ref, nused_ref, e_ref, rank_ref, h1_ref,
                     xs_hbm, dest_ref, rows, zeros_vmem, row_sem, fill_sem, *, n_blocks):
    i = pl.program_id(0)
    tm = h1_ref.shape[0]
    bm = EXPERT_ROWS

    def fill_copies(e):
        n = filln_ref[e]
        start = fill0_ref[e]
        out = []
        for c in _pow2_chunks(bm - 1):
            off = start + (n // (2 * c)) * (2 * c)
            out.append(((n // c) % 2 == 1,
                        pltpu.make_async_copy(zeros_vmem.at[pl.ds(0, c)], xs_hbm.at[pl.ds(off, c)], fill_sem)))
        return out

    def tail_copy(j):
        return pltpu.make_async_copy(zeros_vmem, xs_hbm.at[pl.ds(j * bm, bm)], fill_sem)

    @pl.when(i == 0)
    def _():
        zeros_vmem[...] = jnp.zeros_like(zeros_vmem)

        def start_fill(e, c):
            for pred, cp in fill_copies(e):
                @pl.when(pred)
                def _():
                    cp.start()
            return c
        lax.fori_loop(0, N_EXPERTS, start_fill, 0)
        lax.fori_loop(nused_ref[0], n_blocks, lambda j, c: (tail_copy(j).start(), c)[1], 0)

    for j in range(rows.shape[1]):
        rows[:, j, :] = h1_ref[:, j * LANES:(j + 1) * LANES]

    def issue(t, c):
        for k in range(TOP_K):
            idx = k * tm + t
            d = pstart_ref[e_ref[idx]] + rank_ref[idx]
            dest_ref[idx] = d
            pltpu.make_async_copy(rows.at[t], xs_hbm.at[d], row_sem).start(priority=k % 2)
        return c
    lax.fori_loop(0, tm, issue, 0)
    for _ in range(TOP_K):
        pltpu.make_async_copy(rows, xs_hbm.at[pl.ds(0, tm)], row_sem).wait()

    @pl.when(i == 0)
    def _():
        def wait_fill(e, c):
            for pred, cp in fill_copies(e):
                @pl.when(pred)
                def _():
                    cp.wait()
            return c
        lax.fori_loop(0, N_EXPERTS, wait_fill, 0)
        lax.fori_loop(nused_ref[0], n_blocks, lambda j, c: (tail_copy(j).wait(), c)[1], 0)


def _tile_major(a, tm):
    k, t = a.shape
    return a.reshape(k, t // tm, tm).swapaxes(0, 1).reshape(-1)


def _dispatch_rows(pad_start, fill_start, fill_len, n_used, top_e, rank, h1, n_blocks):
    t, width = h1.shape
    tm = LANES
    row_shape = (width // LANES, LANES)
    smem_tile = pl.BlockSpec((TOP_K * tm,), lambda i, *_: (i,), memory_space=pltpu.SMEM)
    grid_spec = pltpu.PrefetchScalarGridSpec(
        num_scalar_prefetch=4,
        grid=(t // tm,),
        in_specs=[smem_tile, smem_tile, pl.BlockSpec((tm, width), lambda i, *_: (i, 0))],
        out_specs=[pl.BlockSpec(memory_space=pl.ANY), smem_tile],
        scratch_shapes=[pltpu.VMEM((tm,) + row_shape, h1.dtype), pltpu.VMEM((EXPERT_ROWS,) + row_shape, h1.dtype),
                        pltpu.SemaphoreType.DMA, pltpu.SemaphoreType.DMA],
    )
    return pl.pallas_call(
        functools.partial(_dispatch_kernel, n_blocks=n_blocks),
        out_shape=[
            jax.ShapeDtypeStruct((n_blocks * EXPERT_ROWS,) + row_shape, h1.dtype),
            jax.ShapeDtypeStruct((TOP_K * t,), jnp.int32),
        ],
        grid_spec=grid_spec,
        compiler_params=pltpu.CompilerParams(dimension_semantics=("arbitrary",)),
        name="dispatch",
    )(pad_start, fill_start, fill_len, n_used, _tile_major(top_e, tm), _tile_major(rank, tm), h1)


def _experts_kernel(be_ref, first_ref, slot_ref, nxt_ref, nused_ref, x_ref, wg_hbm, wu_hbm, wd_hbm, y_ref,
                    wg_buf, wu_buf, wd_buf, wg_bf, wu_bf, wd_bf, sems):
    i = pl.program_id(0)

    def weight_copies(e, slot):
        return (pltpu.make_async_copy(wg_hbm.at[e], wg_buf.at[slot], sems.at[slot, 0]),
                pltpu.make_async_copy(wu_hbm.at[e], wu_buf.at[slot], sems.at[slot, 1]),
                pltpu.make_async_copy(wd_hbm.at[e], wd_buf.at[slot], sems.at[slot, 2]))

    @pl.when(i == 0)
    def _():
        for cp in weight_copies(be_ref[0], 0):
            cp.start(priority=1)

    @pl.when(first_ref[i] == 1)
    def _():
        slot = slot_ref[i]
        for cp in weight_copies(be_ref[i], slot):
            cp.wait()

        @pl.when(nxt_ref[i] >= 0)
        def _():
            for cp in weight_copies(nxt_ref[i], 1 - slot):
                cp.start(priority=1)

        wg_bf[...] = wg_buf[slot].astype(BF16)
        wu_bf[...] = wu_buf[slot].astype(BF16)
        wd_bf[...] = wd_buf[slot].astype(BF16)

    @pl.when(i < nused_ref[0])
    def _():
        x = jnp.concatenate([x_ref[:, j, :] for j in range(x_ref.shape[1])], axis=1).astype(BF16)
        g = jnp.dot(x, wg_bf[...], preferred_element_type=F32)
        u = jnp.dot(x, wu_bf[...], preferred_element_type=F32)
        a = (g * jax.nn.sigmoid(g) * u).astype(BF16)
        y_ref[...] = jnp.dot(a, wd_bf[...], preferred_element_type=F32)

    @pl.when(i >= nused_ref[0])
    def _():
        y_ref[...] = jnp.zeros_like(y_ref)


def _experts(block_e, first, slot, nxt, n_used, xs, w_gate, w_up, w_down):
    p = xs.shape[0]
    bm = EXPERT_ROWS
    n_blocks = p // bm

    def xrow(i, be, fi, sl, nx, nu):
        return (jnp.minimum(i, nu[0] - 1), 0, 0)

    grid_spec = pltpu.PrefetchScalarGridSpec(
        num_scalar_prefetch=5,
        grid=(n_blocks,),
        in_specs=[
            pl.BlockSpec((bm,) + xs.shape[1:], xrow),
            pl.BlockSpec(memory_space=pl.ANY),
            pl.BlockSpec(memory_space=pl.ANY),
            pl.BlockSpec(memory_space=pl.ANY),
        ],
        out_specs=pl.BlockSpec((bm, D_MODEL), lambda i, *_: (i, 0)),
        scratch_shapes=[
            pltpu.VMEM((2, D_MODEL, D_EXPERT), F32),
            pltpu.VMEM((2, D_MODEL, D_EXPERT), F32),
            pltpu.VMEM((2, D_EXPERT, D_MODEL), F32),
            pltpu.VMEM((D_MODEL, D_EXPERT), BF16),
            pltpu.VMEM((D_MODEL, D_EXPERT), BF16),
            pltpu.VMEM((D_EXPERT, D_MODEL), BF16),
            pltpu.SemaphoreType.DMA((2, 3)),
        ],
    )
    return pl.pallas_call(
        _experts_kernel,
        out_shape=jax.ShapeDtypeStruct((p, D_MODEL), F32),
        grid_spec=grid_spec,
        compiler_params=pltpu.CompilerParams(
            dimension_semantics=("arbitrary",), vmem_limit_bytes=VMEM_LIMIT),
        name="experts",
    )(block_e, first, slot, nxt, n_used, xs, w_gate, w_up, w_down)


def _combine_kernel(dest_ref, gate_ref, yb_hbm, o_ref, buf, sem):
    tm = o_ref.shape[0]

    def issue(tb, c):
        base = pl.multiple_of(tb * SUBLANES, SUBLANES)
        for r in range(SUBLANES):
            for k in range(TOP_K):
                idx = k * tm + base + r
                pltpu.make_async_copy(yb_hbm.at[pl.ds(dest_ref[idx], 1), :], buf.at[pl.ds(idx, 1), :],
                                      sem).start(priority=k % 2)
        return c
    lax.fori_loop(0, tm // SUBLANES, issue, 0)
    pltpu.make_async_copy(yb_hbm.at[pl.ds(0, TOP_K * tm), :], buf, sem).wait()

    acc = gate_ref[:, 0:1] * buf[0:tm, :]
    for k in range(1, TOP_K):
        acc = acc + gate_ref[:, k:k + 1] * buf[k * tm:(k + 1) * tm, :]
    o_ref[...] = acc


def _combine(dest, gate_tk, yb):
    t = gate_tk.shape[0]
    tm = LANES
    return pl.pallas_call(
        _combine_kernel,
        out_shape=jax.ShapeDtypeStruct((t, D_MODEL), F32),
        grid=(t // tm,),
        in_specs=[
            pl.BlockSpec((TOP_K * tm,), lambda i: (i,), memory_space=pltpu.SMEM),
            pl.BlockSpec((tm, TOP_K), lambda i: (i, 0)),
            pl.BlockSpec(memory_space=pl.ANY),
        ],
        out_specs=pl.BlockSpec((tm, D_MODEL), lambda i: (i, 0)),
        scratch_shapes=[pltpu.VMEM((TOP_K * tm, D_MODEL), F32), pltpu.SemaphoreType.DMA],
        compiler_params=pltpu.CompilerParams(
            dimension_semantics=("arbitrary",), vmem_limit_bytes=VMEM_LIMIT),
        name="combine",
    )(dest, gate_tk, yb)


def _final_kernel(h1_ref, h1b_ref, routed_ref, p_ref, wsg_ref, wsu_ref, wsd_ref, wpg_ref, wp_ref,
                  g2_ref, b2_ref, y_ref):
    hb = h1b_ref[...]
    g = jnp.dot(hb, wsg_ref[...], preferred_element_type=F32)
    u = jnp.dot(hb, wsu_ref[...], preferred_element_type=F32)
    shared = jnp.dot((g * jax.nn.sigmoid(g) * u).astype(BF16), wsd_ref[...], preferred_element_type=F32)
    gate = jax.nn.sigmoid(jnp.dot(hb, wpg_ref[...], preferred_element_type=F32))
    ple = gate * jnp.dot(p_ref[...], wp_ref[...], preferred_element_type=F32)
    r = DEEPNORM_ALPHA * h1_ref[...] + (routed_ref[...] + shared) + ple
    y_ref[...] = _layer_norm(r, g2_ref[...], b2_ref[...])


def _final(h1, h1b, routed, p_all, ws_gate, ws_up, ws_down, w_ple_gate, w_ple, g2, b2, *, row0, n_rows, tm):
    assert row0 % tm == 0 and n_rows % tm == 0
    blk0 = row0 // tm
    const = lambda i: (0, 0)
    rows = lambda i: (blk0 + i, 0)
    resident = functools.partial(pl.BlockSpec, index_map=const, pipeline_mode=pl.Buffered(1))
    return pl.pallas_call(
        _final_kernel,
        out_shape=jax.ShapeDtypeStruct((n_rows, D_MODEL), F32),
        grid=(n_rows // tm,),
        in_specs=[
            pl.BlockSpec((tm, D_MODEL), rows),
            pl.BlockSpec((tm, D_MODEL), rows),
            pl.BlockSpec((tm, D_MODEL), rows),
            pl.BlockSpec((tm, PLE_DIM), rows),
            resident((D_MODEL, D_SHARED)),
            resident((D_MODEL, D_SHARED)),
            resident((D_SHARED, D_MODEL)),
            resident((D_MODEL, D_MODEL)),
            resident((PLE_DIM, D_MODEL)),
            pl.BlockSpec((1, D_MODEL), const),
            pl.BlockSpec((1, D_MODEL), const),
        ],
        out_specs=pl.BlockSpec((tm, D_MODEL), lambda i: (i, 0)),
        compiler_params=pltpu.CompilerParams(
            dimension_semantics=("arbitrary",), vmem_limit_bytes=VMEM_LIMIT),
        name="final",
    )(h1, h1b, routed, p_all, ws_gate, ws_up, ws_down, w_ple_gate, w_ple, g2, b2)


def _rope_tables(pos):
    half = ROT_DIM // 2
    inv_freq = ROPE_THETA ** (-jnp.arange(half, dtype=F32) * 2.0 / ROT_DIM)
    ang = pos.astype(F32)[:, None] * inv_freq[None, :]
    cos, sin = jnp.cos(ang), jnp.sin(ang)
    n = pos.shape[0]
    ones = jnp.ones((n, HEAD_DIM - ROT_DIM), F32)
    zeros = jnp.zeros((n, HEAD_DIM - ROT_DIM), F32)
    zh = jnp.zeros((n, half), F32)
    cos_t = jnp.concatenate([cos, cos, ones], -1)
    sin_lo = jnp.concatenate([-sin, zh, zeros], -1)
    sin_hi = jnp.concatenate([zh, sin, zeros], -1)
    per_head = jnp.stack([cos_t, sin_lo, sin_hi])
    return jnp.concatenate([per_head] * (LANES // HEAD_DIM), -1)


def _block_plan(counts, n_blocks):
    bm = EXPERT_ROWS
    i32 = jnp.int32
    padded = (counts + bm - 1) // bm * bm
    pad_end = jnp.cumsum(padded)
    pad_start = pad_end - padded
    n_used = pad_end[-1] // bm
    blk = jnp.arange(n_blocks)
    valid = blk < n_used
    block_e = jnp.minimum(jnp.sum(pad_end[None, :] <= (blk * bm)[:, None], axis=1), N_EXPERTS - 1)
    block_e = jnp.where(valid, block_e, block_e[jnp.maximum(n_used - 1, 0)])
    prev_e = jnp.concatenate([jnp.full((1,), -1, block_e.dtype), block_e[:-1]])
    first = valid & (block_e != prev_e)
    run = jnp.cumsum(first) - 1
    run_expert = jnp.full((n_blocks + 2,), -1, block_e.dtype).at[jnp.where(first, run, n_blocks + 1)].set(block_e)
    nxt = jnp.where(first, run_expert[jnp.minimum(run + 1, n_blocks)], -1)
    return (pad_start.astype(i32), (pad_start + counts).astype(i32), (padded - counts).astype(i32),
            n_used.astype(i32).reshape(1), block_e.astype(i32), first.astype(i32), (run % 2).astype(i32),
            nxt.astype(i32))


def kernel(x_prompt, x_sample, p_prompt, p_sample, cache_k, cache_v, state_conv, ln_in_g, ln_in_b, w_in,
           attn_sinks, conv_w, gn_attn, gn_conv, w_out, ln1_g, ln1_b, w_router, router_bias, w_gate, w_up,
           w_down, ws_gate, ws_up, ws_down, w_ple_gate, w_ple, ln2_g, ln2_b):
    n_batch, seq, d = x_prompt.shape
    n_dec, dec_seq, _ = x_sample.shape
    depth = w_in.shape[0]
    wb = cache_k.shape[2]
    assert depth == 1 and dec_seq == 1 and d == D_MODEL
    assert wb == ATTN_BLOCK and seq % ATTN_BLOCK == 0 and n_dec % SAMPLE_CHUNK == 0
    t_p = n_batch * seq
    t = t_p + n_dec
    assert t % ROW_TILE == 0 and t_p % SAMPLE_CHUNK == 0

    x_all = jnp.concatenate([x_prompt.reshape(t_p, d), x_sample.reshape(n_dec, d)], 0)
    p_all = jnp.concatenate([p_prompt[0].reshape(t_p, PLE_DIM), p_sample[0].reshape(n_dec, PLE_DIM)], 0).astype(BF16)
    row = lambda v: v.reshape(1, -1).astype(F32)

    w_in0 = w_in[0]
    qkv_w = ATTN_WIDTH + 2 * KV_WIDTH
    w_in_perm = jnp.concatenate([w_in0[:, qkv_w:], w_in0[:, :qkv_w]], axis=1).astype(BF16)

    proj = _ln_inproj(x_all, row(ln_in_g), row(ln_in_b), w_in_perm)

    rope_p = _rope_tables(jnp.arange(seq))
    rope_s = _rope_tables(PAST_LEN + jnp.arange(dec_seq))[:, 0, :]
    sinks = attn_sinks[0].astype(F32)
    cw, ga, gc = conv_w[0].astype(F32), row(gn_attn[0]), row(gn_conv[0])
    mixed_p, kwin_p, vwin_p, cstate_p = _prompt_mixer(proj, sinks, rope_p, cw, ga, gc, n_batch, seq)
    mixed_s, kwin_s, vwin_s, cstate_s = _sample_mixer(
        proj, t_p, n_dec, sinks, cache_k[0].reshape(n_dec, wb, KV_WIDTH), cache_v[0].reshape(n_dec, wb, KV_WIDTH),
        jnp.swapaxes(state_conv[0], 0, 1), rope_s, cw, ga, gc)
    cstate_s = jnp.swapaxes(cstate_s, 0, 1)
    mixed = jnp.concatenate([mixed_p, mixed_s], 0)

    h1, h1b = _outproj(
        mixed, x_all, row(ln_in_g), row(ln_in_b), w_out[0].astype(BF16), row(ln1_g[0]), row(ln1_b[0]))

    top_e, gate, rank, cnt = _route(h1b, w_router[0].T.astype(BF16), router_bias[0].astype(F32).reshape(N_EXPERTS, 1))
    n_blocks = t * TOP_K // EXPERT_ROWS + N_EXPERTS
    pad_start, fill_start, fill_len, n_used, block_e, first, slot, nxt = _block_plan(cnt[:, 0].astype(jnp.int32), n_blocks)
    xs, dest = _dispatch_rows(pad_start, fill_start, fill_len, n_used, top_e, rank, h1, n_blocks)
    yb = _experts(block_e, first, slot, nxt, n_used, xs, w_gate[0], w_up[0], w_down[0])
    routed = _combine(dest, gate.T, yb)

    final_args = (h1, h1b, routed, p_all, ws_gate[0].astype(BF16), ws_up[0].astype(BF16), ws_down[0].astype(BF16),
                  w_ple_gate[0].astype(BF16), w_ple[0].astype(BF16), row(ln2_g[0]), row(ln2_b[0]))
    y_p = _final(*final_args, row0=0, n_rows=t_p, tm=FINAL_TILE)
    y_s = _final(*final_args, row0=t_p, n_rows=n_dec, tm=n_dec)

    kv_shape = (1, -1, wb, N_KV_HEADS, HEAD_DIM)
    return (y_p.reshape(n_batch, seq, d), y_s.reshape(n_dec, dec_seq, d),
            kwin_p.reshape(kv_shape), vwin_p.reshape(kv_shape), cstate_p[None],
            kwin_s.reshape(kv_shape), vwin_s.reshape(kv_shape), cstate_s[None])
```

```python
import functools

import numpy as np
import jax
import jax.numpy as jnp
from jax import lax
from jax.experimental import pallas as pl
from jax.experimental.pallas import tpu as pltpu

D_MODEL = 2048
N_HEADS = 16
N_KV_HEADS = 4
HEAD_DIM = 64
GQA_GROUP = N_HEADS // N_KV_HEADS
ATTN_WIDTH = N_HEADS * HEAD_DIM
KV_WIDTH = N_KV_HEADS * HEAD_DIM
WINDOW = 128
ATTN_BLOCK = WINDOW
ROT_DIM = HEAD_DIM // 4
ROPE_THETA = 500000.0
CONV_WIDTH = D_MODEL - ATTN_WIDTH
CONV_K = 3
IN_COLS = ATTN_WIDTH + 2 * KV_WIDTH + 3 * CONV_WIDTH
N_EXPERTS = 256
TOP_K = 8
N_EXPERT_GROUPS = 8
TOPK_GROUPS = 4
D_EXPERT = 512
D_SHARED = 512
ROUTED_SCALE = 2.5
PLE_DIM = 256
LN_EPS = 1e-5
DEPTH = 1
PAST_LEN = 8192
DEEPNORM_ALPHA = (2 * DEPTH) ** 0.25

LANES = 128
SUBLANES = 8
COL_B, COL_C, COL_X, COL_Q, COL_KV = 0, CONV_WIDTH, 2 * CONV_WIDTH, 3 * CONV_WIDTH, 3 * CONV_WIDTH + ATTN_WIDTH

ROW_TILE = 640
INPROJ_COL_TILE = 1536
FINAL_TILE = 256
EXPERT_ROWS = 128
SAMPLE_CHUNK = 8
VMEM_LIMIT = 56 * 1024 * 1024

BF16 = jnp.bfloat16
F32 = jnp.float32


def _layer_norm(x, g, b):
    xc = x - jnp.mean(x, -1, keepdims=True)
    var = jnp.mean(xc * xc, -1, keepdims=True)
    return xc * lax.rsqrt(var + LN_EPS) * g + b


def _rms_norm(x, g):
    return x * lax.rsqrt(jnp.mean(x * x, -1, keepdims=True) + LN_EPS) * g


def _rope_lanes(x, cos_t, sin_lo, sin_hi):
    half = ROT_DIM // 2
    ax = x.ndim - 1
    return x * cos_t + pltpu.roll(x, LANES - half, ax) * sin_lo + pltpu.roll(x, half, ax) * sin_hi


def _rope_wide(x, cos_t, sin_lo, sin_hi):
    n = x.shape[-1] // LANES
    return jnp.concatenate(
        [_rope_lanes(x[..., c * LANES:(c + 1) * LANES], cos_t, sin_lo, sin_hi) for c in range(n)], axis=-1)


def _ln_inproj_kernel(x_ref, g_ref, b_ref, w_ref, o_ref, h_scr):
    @pl.when(pl.program_id(1) == 0)
    def _():
        h_scr[...] = _layer_norm(x_ref[...], g_ref[...], b_ref[...]).astype(BF16)

    o_ref[...] = jnp.dot(h_scr[...], w_ref[...], preferred_element_type=F32)


def _ln_inproj(x_all, g, b, w_bf16):
    t = x_all.shape[0]
    return pl.pallas_call(
        _ln_inproj_kernel,
        out_shape=jax.ShapeDtypeStruct((t, IN_COLS), F32),
        grid=(t // ROW_TILE, IN_COLS // INPROJ_COL_TILE),
        in_specs=[
            pl.BlockSpec((ROW_TILE, D_MODEL), lambda i, j: (i, 0)),
            pl.BlockSpec((1, D_MODEL), lambda i, j: (0, 0)),
            pl.BlockSpec((1, D_MODEL), lambda i, j: (0, 0)),
            pl.BlockSpec((D_MODEL, INPROJ_COL_TILE), lambda i, j: (0, j)),
        ],
        out_specs=pl.BlockSpec((ROW_TILE, INPROJ_COL_TILE), lambda i, j: (i, j)),
        scratch_shapes=[pltpu.VMEM((ROW_TILE, D_MODEL), BF16)],
        compiler_params=pltpu.CompilerParams(
            dimension_semantics=("arbitrary", "arbitrary"), vmem_limit_bytes=VMEM_LIMIT),
        name="ln_inproj",
    )(x_all, g, b, w_bf16)


def _prompt_mixer_kernel(sink_ref, b_ref, c_ref, x_ref, q_ref, kv_ref, kvp_ref, cp_ref, xp_ref,
                         rope_ref, ropep_ref, convw_ref, gna_ref, gnc_ref,
                         mixed_ref, kwin_ref, vwin_ref, cstate_ref, *, n_blocks):
    i = pl.program_id(1)
    blk = ATTN_BLOCK
    cos_t, sin_lo, sin_hi = rope_ref[0], rope_ref[1], rope_ref[2]
    qr = _rope_wide(q_ref[...], cos_t, sin_lo, sin_hi) * (HEAD_DIM ** -0.5)
    k_cur = _rope_wide(kv_ref[:, :KV_WIDTH], cos_t, sin_lo, sin_hi)
    k_prev = _rope_wide(kvp_ref[:, :KV_WIDTH], ropep_ref[0], ropep_ref[1], ropep_ref[2])
    v_cur = kv_ref[:, KV_WIDTH:]
    v_prev = kvp_ref[:, KV_WIDTH:]
    k_all = jnp.concatenate([k_prev, k_cur], axis=0).astype(BF16)
    v_all = jnp.concatenate([v_prev, v_cur], axis=0).astype(BF16)

    qi = lax.broadcasted_iota(jnp.int32, (blk, 2 * blk), 0)
    sj = lax.broadcasted_iota(jnp.int32, (blk, 2 * blk), 1)
    rel = qi + blk - sj
    mask = (rel >= 0) & (rel <= WINDOW) & ((sj >= blk) | (i > 0))
    mask = jnp.concatenate([mask] * GQA_GROUP, axis=0)

    heads = []
    for g in range(N_KV_HEADS):
        qg = jnp.concatenate(
            [qr[:, (g * GQA_GROUP + hh) * HEAD_DIM:(g * GQA_GROUP + hh + 1) * HEAD_DIM] for hh in range(GQA_GROUP)],
            axis=0).astype(BF16)
        kg = k_all[:, g * HEAD_DIM:(g + 1) * HEAD_DIM]
        vg = v_all[:, g * HEAD_DIM:(g + 1) * HEAD_DIM]
        s = lax.dot_general(qg, kg, (((1,), (1,)), ((), ())), preferred_element_type=F32)
        s = jnp.where(mask, s, -jnp.inf)
        sink = jnp.concatenate(
            [jnp.full((blk, 1), sink_ref[g * GQA_GROUP + hh], F32) for hh in range(GQA_GROUP)], axis=0)
        m = jnp.maximum(jnp.max(s, -1, keepdims=True), sink)
        e = jnp.exp(s - m)
        p = e / (jnp.sum(e, -1, keepdims=True) + jnp.exp(sink - m))
        og = jnp.dot(p.astype(BF16), vg, preferred_element_type=F32)
        heads += [og[hh * blk:(hh + 1) * blk] for hh in range(GQA_GROUP)]
    attn = jnp.concatenate(heads, axis=-1)
    mixed_ref[:, :ATTN_WIDTH] = _rms_norm(attn, gna_ref[...]).astype(BF16)

    u = c_ref[...] * x_ref[...]
    n_tail = CONV_K - 1
    tail_rows = cp_ref.shape[0]
    row_id = lax.broadcasted_iota(jnp.int32, u.shape, 0)
    acc = u * convw_ref[CONV_K - 1:CONV_K, :]
    for d in range(1, CONV_K):
        ud = pltpu.roll(u, d, 0)
        for r in range(d):
            src = tail_rows - d + r
            up = jnp.where(i > 0, cp_ref[src:src + 1, :] * xp_ref[src:src + 1, :], 0.0)
            ud = jnp.where(row_id == r, up, ud)
        acc = acc + ud * convw_ref[CONV_K - 1 - d:CONV_K - d, :]
    mixed_ref[:, ATTN_WIDTH:] = _rms_norm(b_ref[...] * acc, gnc_ref[...]).astype(BF16)

    @pl.when(i == n_blocks - 1)
    def _():
        kwin_ref[0] = k_cur
        vwin_ref[0] = v_cur
        cstate_ref[0] = u[blk - n_tail:, :]


def _prompt_mixer(proj, sinks, rope_tab, conv_w, gn_attn, gn_conv, n_batch, seq):
    blk = ATTN_BLOCK
    nb = seq // blk
    tail = 8
    wide = CONV_WIDTH // 1
    kvb = COL_KV // (2 * KV_WIDTH)

    def row(b, i, s):
        return b * nb + i

    def prev_row(b, i, s):
        return b * nb + jnp.maximum(i - 1, 0)

    def prev_tail(b, i, s):
        return jnp.maximum((b * nb + i) * (blk // tail) - 1, 0)

    grid_spec = pltpu.PrefetchScalarGridSpec(
        num_scalar_prefetch=1,
        grid=(n_batch, nb),
        in_specs=[
            pl.BlockSpec((blk, wide), lambda b, i, s: (row(b, i, s), COL_B // wide)),
            pl.BlockSpec((blk, wide), lambda b, i, s: (row(b, i, s), COL_C // wide)),
            pl.BlockSpec((blk, wide), lambda b, i, s: (row(b, i, s), COL_X // wide)),
            pl.BlockSpec((blk, ATTN_WIDTH), lambda b, i, s: (row(b, i, s), COL_Q // ATTN_WIDTH)),
            pl.BlockSpec((blk, 2 * KV_WIDTH), lambda b, i, s: (row(b, i, s), kvb)),
            pl.BlockSpec((blk, 2 * KV_WIDTH), lambda b, i, s: (prev_row(b, i, s), kvb)),
            pl.BlockSpec((tail, wide), lambda b, i, s: (prev_tail(b, i, s), COL_C // wide)),
            pl.BlockSpec((tail, wide), lambda b, i, s: (prev_tail(b, i, s), COL_X // wide)),
            pl.BlockSpec((3, blk, LANES), lambda b, i, s: (0, i, 0)),
            pl.BlockSpec((3, blk, LANES), lambda b, i, s: (0, jnp.maximum(i - 1, 0), 0)),
            pl.BlockSpec((CONV_K, CONV_WIDTH), lambda b, i, s: (0, 0)),
            pl.BlockSpec((1, ATTN_WIDTH), lambda b, i, s: (0, 0)),
            pl.BlockSpec((1, CONV_WIDTH), lambda b, i, s: (0, 0)),
        ],
        out_specs=[
            pl.BlockSpec((blk, D_MODEL), lambda b, i, s: (row(b, i, s), 0)),
            pl.BlockSpec((1, blk, KV_WIDTH), lambda b, i, s: (b, 0, 0)),
            pl.BlockSpec((1, blk, KV_WIDTH), lambda b, i, s: (b, 0, 0)),
            pl.BlockSpec((1, CONV_K - 1, CONV_WIDTH), lambda b, i, s: (b, 0, 0)),
        ],
    )
    return pl.pallas_call(
        functools.partial(_prompt_mixer_kernel, n_blocks=nb),
        out_shape=[
            jax.ShapeDtypeStruct((n_batch * seq, D_MODEL), BF16),
            jax.ShapeDtypeStruct((n_batch, blk, KV_WIDTH), F32),
            jax.ShapeDtypeStruct((n_batch, blk, KV_WIDTH), F32),
            jax.ShapeDtypeStruct((n_batch, CONV_K - 1, CONV_WIDTH), F32),
        ],
        grid_spec=grid_spec,
        compiler_params=pltpu.CompilerParams(
            dimension_semantics=("arbitrary", "arbitrary"), vmem_limit_bytes=VMEM_LIMIT),
        name="prompt_mixer",
    )(sinks, proj, proj, proj, proj, proj, proj, proj, proj, rope_tab, rope_tab, conv_w, gn_attn, gn_conv)


def _expand_groups(t):
    ax = t.ndim - 1
    lane = lax.broadcasted_iota(jnp.int32, t.shape[:-1] + (LANES,), ax)
    chunks = []
    for c in range(KV_WIDTH // LANES):
        a = t[..., c * LANES:(c + 1) * LANES]
        r = pltpu.roll(a, HEAD_DIM, ax)
        lo = jnp.where(lane < HEAD_DIM, a, r)
        hi = jnp.where(lane < HEAD_DIM, r, a)
        chunks += [lo] * (GQA_GROUP // 2) + [hi] * (GQA_GROUP // 2)
    return jnp.concatenate(chunks, axis=-1)


def _sample_mixer_kernel(sink_ref, b_ref, c_ref, x_ref, q_ref, kv_ref, ck_ref, cv_ref, st_ref,
                         rope_ref, seg_ref, segt_ref, convw_ref, gna_ref, gnc_ref,
                         mixed_ref, kwin_ref, vwin_ref, cstate_ref):
    nb, wb = ck_ref.shape[0], ck_ref.shape[1]
    cos_t, sin_lo, sin_hi = rope_ref[0:1, :], rope_ref[1:2, :], rope_ref[2:3, :]
    qr = _rope_wide(q_ref[...], cos_t, sin_lo, sin_hi) * (HEAD_DIM ** -0.5)
    k_new = _rope_wide(kv_ref[:, :KV_WIDTH], cos_t, sin_lo, sin_hi)
    v_new = kv_ref[:, KV_WIDTH:]
    ck = ck_ref[...]
    cv = cv_ref[...]

    seg = seg_ref[...]
    prod = _expand_groups(ck) * qr[:, None, :]
    s_old = jnp.dot(prod.reshape(nb * wb, ATTN_WIDTH).astype(BF16), seg,
                    preferred_element_type=F32).reshape(nb, wb, N_HEADS)
    s_new = jnp.dot((_expand_groups(k_new) * qr).astype(BF16), seg, preferred_element_type=F32)
    sink = sink_ref[...]
    m = jnp.maximum(jnp.maximum(jnp.max(s_old, axis=1), s_new), sink)
    e_old = jnp.exp(s_old - m[:, None, :])
    e_new = jnp.exp(s_new - m)
    inv = 1.0 / (jnp.sum(e_old, axis=1) + e_new + jnp.exp(sink - m))
    p_old = (e_old * inv[:, None, :]).astype(BF16)
    p_new = (e_new * inv).astype(BF16)
    segt = segt_ref[...]
    pe_old = jnp.dot(p_old.reshape(nb * wb, N_HEADS), segt, preferred_element_type=F32).reshape(nb, wb, ATTN_WIDTH)
    pe_new = jnp.dot(p_new, segt, preferred_element_type=F32)
    attn = jnp.sum(pe_old * _expand_groups(cv), axis=1) + pe_new * _expand_groups(v_new)
    mixed_ref[:, :ATTN_WIDTH] = _rms_norm(attn, gna_ref[...]).astype(BF16)

    u = c_ref[...] * x_ref[...]
    acc = u * convw_ref[CONV_K - 1:CONV_K, :]
    for j in range(CONV_K - 1):
        acc = acc + st_ref[j] * convw_ref[j:j + 1, :]
    mixed_ref[:, ATTN_WIDTH:] = _rms_norm(b_ref[...] * acc, gnc_ref[...]).astype(BF16)

    kwin_ref[:, 0:wb - 1, :] = ck_ref[:, 1:wb, :]
    vwin_ref[:, 0:wb - 1, :] = cv_ref[:, 1:wb, :]
    for n in range(nb):
        kwin_ref[n, wb - 1:wb, :] = k_new[n:n + 1, :]
        vwin_ref[n, wb - 1:wb, :] = v_new[n:n + 1, :]
    for j in range(CONV_K - 2):
        cstate_ref[j] = st_ref[j + 1]
    cstate_ref[CONV_K - 2] = u


def _sample_mixer(proj, row0, n_dec, sinks, cache_k, cache_v, state_conv, rope_row, conv_w, gn_attn, gn_conv):
    nb = SAMPLE_CHUNK
    wb = cache_k.shape[1]
    r0 = row0 // nb
    wide = CONV_WIDTH
    head_of_lane = np.arange(ATTN_WIDTH) // HEAD_DIM
    seg = jnp.asarray(head_of_lane[:, None] == np.arange(N_HEADS)[None, :], BF16)
    kvb = COL_KV // (2 * KV_WIDTH)
    return pl.pallas_call(
        _sample_mixer_kernel,
        out_shape=[
            jax.ShapeDtypeStruct((n_dec, D_MODEL), BF16),
            jax.ShapeDtypeStruct((n_dec, wb, KV_WIDTH), F32),
            jax.ShapeDtypeStruct((n_dec, wb, KV_WIDTH), F32),
            jax.ShapeDtypeStruct((CONV_K - 1, n_dec, CONV_WIDTH), F32),
        ],
        grid=(n_dec // nb,),
        in_specs=[
            pl.BlockSpec((1, N_HEADS), lambda i: (0, 0)),
            pl.BlockSpec((nb, wide), lambda i: (r0 + i, COL_B // wide)),
            pl.BlockSpec((nb, wide), lambda i: (r0 + i, COL_C // wide)),
            pl.BlockSpec((nb, wide), lambda i: (r0 + i, COL_X // wide)),
            pl.BlockSpec((nb, ATTN_WIDTH), lambda i: (r0 + i, COL_Q // ATTN_WIDTH)),
            pl.BlockSpec((nb, 2 * KV_WIDTH), lambda i: (r0 + i, kvb)),
            pl.BlockSpec((nb, wb, KV_WIDTH), lambda i: (i, 0, 0)),
            pl.BlockSpec((nb, wb, KV_WIDTH), lambda i: (i, 0, 0)),
            pl.BlockSpec((CONV_K - 1, nb, CONV_WIDTH), lambda i: (0, i, 0)),
            pl.BlockSpec((3, LANES), lambda i: (0, 0)),
            pl.BlockSpec((ATTN_WIDTH, N_HEADS), lambda i: (0, 0)),
            pl.BlockSpec((N_HEADS, ATTN_WIDTH), lambda i: (0, 0)),
            pl.BlockSpec((CONV_K, CONV_WIDTH), lambda i: (0, 0)),
            pl.BlockSpec((1, ATTN_WIDTH), lambda i: (0, 0)),
            pl.BlockSpec((1, CONV_WIDTH), lambda i: (0, 0)),
        ],
        out_specs=[
            pl.BlockSpec((nb, D_MODEL), lambda i: (i, 0)),
            pl.BlockSpec((nb, wb, KV_WIDTH), lambda i: (i, 0, 0)),
            pl.BlockSpec((nb, wb, KV_WIDTH), lambda i: (i, 0, 0)),
            pl.BlockSpec((CONV_K - 1, nb, CONV_WIDTH), lambda i: (0, i, 0)),
        ],
        compiler_params=pltpu.CompilerParams(
            dimension_semantics=("arbitrary",), vmem_limit_bytes=VMEM_LIMIT),
        name="sample_mixer",
    )(sinks.reshape(1, N_HEADS), proj, proj, proj, proj, proj, cache_k, cache_v, state_conv,
      rope_row, seg, seg.T, conv_w, gn_attn, gn_conv)


def _outproj_kernel(mixed_ref, x_ref, g0_ref, b0_ref, wout_ref, g1_ref, b1_ref, h1_ref, h1b_ref):
    h = _layer_norm(x_ref[...], g0_ref[...], b0_ref[...])
    mixed = jnp.dot(mixed_ref[...], wout_ref[...], preferred_element_type=F32)
    h1 = _layer_norm(DEEPNORM_ALPHA * h + mixed, g1_ref[...], b1_ref[...])
    h1_ref[...] = h1
    h1b_ref[...] = h1.astype(BF16)


def _outproj(mixed, x_all, g0, b0, w_out, g1, b1):
    t = x_all.shape[0]
    tm = ROW_TILE // 2
    const = lambda i: (0, 0)
    resident = functools.partial(pl.BlockSpec, index_map=const, pipeline_mode=pl.Buffered(1))
    return pl.pallas_call(
        _outproj_kernel,
        out_shape=[
            jax.ShapeDtypeStruct((t, D_MODEL), F32),
            jax.ShapeDtypeStruct((t, D_MODEL), BF16),
        ],
        grid=(t // tm,),
        in_specs=[
            pl.BlockSpec((tm, D_MODEL), lambda i: (i, 0)),
            pl.BlockSpec((tm, D_MODEL), lambda i: (i, 0)),
            pl.BlockSpec((1, D_MODEL), const),
            pl.BlockSpec((1, D_MODEL), const),
            resident((D_MODEL, D_MODEL)),
            pl.BlockSpec((1, D_MODEL), const),
            pl.BlockSpec((1, D_MODEL), const),
        ],
        out_specs=[
            pl.BlockSpec((tm, D_MODEL), lambda i: (i, 0)),
            pl.BlockSpec((tm, D_MODEL), lambda i: (i, 0)),
        ],
        compiler_params=pltpu.CompilerParams(
            dimension_semantics=("arbitrary",), vmem_limit_bytes=VMEM_LIMIT),
        name="outproj",
    )(mixed, x_all, g0, b0, w_out, g1, b1)


def _route_kernel(h_ref, wrt_ref, bias_ref, tri_ref, ones_ref, e_ref, gate_ref, rank_ref, cnt_ref):
    tm = h_ref.shape[0]
    neg = -jnp.inf
    big = float(N_EXPERTS)

    @pl.when(pl.program_id(0) == 0)
    def _():
        cnt_ref[...] = jnp.zeros_like(cnt_ref)

    logits = lax.dot_general(wrt_ref[...], h_ref[...], (((1,), (1,)), ((), ())), preferred_element_type=F32)
    scores = jax.nn.sigmoid(logits)
    biased = scores + bias_ref[...]

    per_group = N_EXPERTS // N_EXPERT_GROUPS
    rid = lax.broadcasted_iota(jnp.int32, (per_group, tm), 0).astype(F32)
    gscore = []
    for g in range(N_EXPERT_GROUPS):
        xg = biased[g * per_group:(g + 1) * per_group]
        m1 = jnp.max(xg, 0, keepdims=True)
        i1 = jnp.min(jnp.where(xg == m1, rid, big), 0, keepdims=True)
        m2 = jnp.max(jnp.where(rid == i1, neg, xg), 0, keepdims=True)
        gscore.append(m1 + m2)
    masked = []
    for g in range(N_EXPERT_GROUPS):
        beaten = jnp.zeros((1, tm), F32)
        for o in range(N_EXPERT_GROUPS):
            if o != g:
                wins = (gscore[o] > gscore[g]) | ((gscore[o] == gscore[g]) & (o < g))
                beaten = beaten + jnp.where(wins, 1.0, 0.0)
        masked.append(jnp.where(beaten < TOPK_GROUPS, biased[g * per_group:(g + 1) * per_group], neg))
    x = jnp.concatenate(masked, axis=0)

    eid = lax.broadcasted_iota(jnp.int32, (N_EXPERTS, tm), 0).astype(F32)
    sel = jnp.zeros((N_EXPERTS, tm), F32)
    picks, gates = [], []
    for _ in range(TOP_K):
        m = jnp.max(x, 0, keepdims=True)
        ik = jnp.min(jnp.where(x == m, eid, big), 0, keepdims=True)
        hit = eid == ik
        gates.append(jnp.sum(jnp.where(hit, scores, 0.0), 0, keepdims=True))
        picks.append(ik)
        x = jnp.where(hit, neg, x)
        sel = jnp.where(hit, 1.0, sel)

    selb = sel.astype(BF16)
    pos = cnt_ref[...] + jnp.dot(selb, tri_ref[...], preferred_element_type=F32)
    cnt_ref[...] = cnt_ref[...] + jnp.dot(selb, ones_ref[...], preferred_element_type=F32)

    gsum = gates[0]
    for k in range(1, TOP_K):
        gsum = gsum + gates[k]
    for k in range(TOP_K):
        e_ref[k:k + 1, :] = picks[k].astype(jnp.int32)
        gate_ref[k:k + 1, :] = gates[k] / gsum * ROUTED_SCALE
        rank_ref[k:k + 1, :] = jnp.sum(jnp.where(eid == picks[k], pos, 0.0), 0, keepdims=True).astype(jnp.int32)


def _route(h1b, w_router_t, bias_col):
    t = h1b.shape[0]
    tm = LANES
    tri = jnp.asarray(np.arange(tm)[:, None] < np.arange(tm)[None, :], BF16)
    ones = jnp.ones((tm, tm), BF16)
    const = lambda i: (0, 0)
    return pl.pallas_call(
        _route_kernel,
        out_shape=[
            jax.ShapeDtypeStruct((TOP_K, t), jnp.int32),
            jax.ShapeDtypeStruct((TOP_K, t), F32),
            jax.ShapeDtypeStruct((TOP_K, t), jnp.int32),
            jax.ShapeDtypeStruct((N_EXPERTS, tm), F32),
        ],
        grid=(t // tm,),
        in_specs=[
            pl.BlockSpec((tm, D_MODEL), lambda i: (i, 0)),
            pl.BlockSpec((N_EXPERTS, D_MODEL), const),
            pl.BlockSpec((N_EXPERTS, 1), const),
            pl.BlockSpec((tm, tm), const),
            pl.BlockSpec((tm, tm), const),
        ],
        out_specs=[
            pl.BlockSpec((TOP_K, tm), lambda i: (0, i)),
            pl.BlockSpec((TOP_K, tm), lambda i: (0, i)),
            pl.BlockSpec((TOP_K, tm), lambda i: (0, i)),
            pl.BlockSpec((N_EXPERTS, tm), const),
        ],
        compiler_params=pltpu.CompilerParams(
            dimension_semantics=("arbitrary",), vmem_limit_bytes=VMEM_LIMIT),
        name="route",
    )(h1b, w_router_t, bias_col, tri, ones)


def _pow2_chunks(n_max):
    return [1 << b for b in reversed(range(int(n_max).bit_length()))]


def _dispatch_kernel(pstart_ref, fill0_ref, filln_ref, nused_ref, e_ref, rank_ref, h1_ref,
                     xs_hbm, dest_ref, zeros_vmem, row_sem, fill_sem, *, n_blocks):
    i = pl.program_id(0)
    tm = h1_ref.shape[0]
    bm = EXPERT_ROWS

    def fill_copies(e):
        n = filln_ref[e]
        start = fill0_ref[e]
        head = jnp.minimum((SUBLANES - start % SUBLANES) % SUBLANES, n)
        out = []
        for r in range(SUBLANES - 1):
            out.append((r < head, pltpu.make_async_copy(
                zeros_vmem.at[pl.ds(0, 1), :], xs_hbm.at[pl.ds(start + r, 1), :], fill_sem)))
        body0 = start + head
        m = n - head
        for c in _pow2_chunks(bm - 1):
            if c < SUBLANES:
                continue
            off = pl.multiple_of(body0 + (m // (2 * c)) * (2 * c), SUBLANES)
            out.append(((m // c) % 2 == 1, pltpu.make_async_copy(
                zeros_vmem.at[pl.ds(0, c), :], xs_hbm.at[pl.ds(off, c), :], fill_sem)))
        return out

    def tail_copy(j):
        return pltpu.make_async_copy(zeros_vmem, xs_hbm.at[pl.ds(pl.multiple_of(j * bm, bm), bm), :], fill_sem)

    @pl.when(i == 0)
    def _():
        zeros_vmem[...] = jnp.zeros_like(zeros_vmem)

        def start_fill(e, c):
            for pred, cp in fill_copies(e):
                @pl.when(pred)
                def _():
                    cp.start(priority=1)
            return c
        lax.fori_loop(0, N_EXPERTS, start_fill, 0)

        def start_tail(j, c):
            tail_copy(j).start(priority=1)
            return c
        lax.fori_loop(nused_ref[0], n_blocks, start_tail, 0)

    def issue(tb, c):
        base = pl.multiple_of(tb * SUBLANES, SUBLANES)
        for r in range(SUBLANES):
            for k in range(TOP_K):
                idx = k * tm + base + r
                d = pstart_ref[e_ref[idx]] + rank_ref[idx]
                dest_ref[idx] = d
                pltpu.make_async_copy(h1_ref.at[pl.ds(base + r, 1), :], xs_hbm.at[pl.ds(d, 1), :], row_sem).start()
        return c
    lax.fori_loop(0, tm // SUBLANES, issue, 0)
    for _ in range(TOP_K):
        pltpu.make_async_copy(h1_ref, xs_hbm.at[pl.ds(0, tm), :], row_sem).wait()

    @pl.when(i == pl.num_programs(0) - 1)
    def _():
        def wait_fill(e, c):
            for pred, cp in fill_copies(e):
                @pl.when(pred)
                def _():
                    cp.wait()
            return c
        lax.fori_loop(0, N_EXPERTS, wait_fill, 0)

        def wait_tail(j, c):
            tail_copy(j).wait()
            return c
        lax.fori_loop(nused_ref[0], n_blocks, wait_tail, 0)


def _tile_major(a, tm):
    k, t = a.shape
    return a.reshape(k, t // tm, tm).swapaxes(0, 1).reshape(-1)


def _dispatch_rows(pad_start, fill_start, fill_len, n_used, top_e, rank, h1, n_blocks):
    t, width = h1.shape
    tm = LANES
    smem_tile = pl.BlockSpec((TOP_K * tm,), lambda i, *_: (i,), memory_space=pltpu.SMEM)
    grid_spec = pltpu.PrefetchScalarGridSpec(
        num_scalar_prefetch=4,
        grid=(t // tm,),
        in_specs=[smem_tile, smem_tile, pl.BlockSpec((tm, width), lambda i, *_: (i, 0))],
        out_specs=[pl.BlockSpec(memory_space=pl.ANY), smem_tile],
        scratch_shapes=[pltpu.VMEM((EXPERT_ROWS, width), h1.dtype),
                        pltpu.SemaphoreType.DMA, pltpu.SemaphoreType.DMA],
    )
    return pl.pallas_call(
        functools.partial(_dispatch_kernel, n_blocks=n_blocks),
        out_shape=[
            jax.ShapeDtypeStruct((n_blocks * EXPERT_ROWS, width), h1.dtype),
            jax.ShapeDtypeStruct((TOP_K * t,), jnp.int32),
        ],
        grid_spec=grid_spec,
        compiler_params=pltpu.CompilerParams(dimension_semantics=("arbitrary",)),
        name="dispatch",
    )(pad_start, fill_start, fill_len, n_used, _tile_major(top_e, tm), _tile_major(rank, tm), h1)


def _experts_kernel(be_ref, first_ref, slot_ref, nxt_ref, nused_ref, x_ref, wg_hbm, wu_hbm, wd_hbm, y_ref,
                    wg_buf, wu_buf, wd_buf, wg_bf, wu_bf, wd_bf, sems):
    i = pl.program_id(0)

    def weight_copies(e, slot):
        return (pltpu.make_async_copy(wg_hbm.at[e], wg_buf.at[slot], sems.at[slot, 0]),
                pltpu.make_async_copy(wu_hbm.at[e], wu_buf.at[slot], sems.at[slot, 1]),
                pltpu.make_async_copy(wd_hbm.at[e], wd_buf.at[slot], sems.at[slot, 2]))

    @pl.when(i == 0)
    def _():
        for cp in weight_copies(be_ref[0], 0):
            cp.start(priority=1)

    @pl.when(first_ref[i] == 1)
    def _():
        slot = slot_ref[i]
        for cp in weight_copies(be_ref[i], slot):
            cp.wait()

        @pl.when(nxt_ref[i] >= 0)
        def _():
            for cp in weight_copies(nxt_ref[i], 1 - slot):
                cp.start(priority=1)

        wg_bf[...] = wg_buf[slot].astype(BF16)
        wu_bf[...] = wu_buf[slot].astype(BF16)
        wd_bf[...] = wd_buf[slot].astype(BF16)

    @pl.when(i < nused_ref[0])
    def _():
        x = x_ref[...].astype(BF16)
        g = jnp.dot(x, wg_bf[...], preferred_element_type=F32)
        u = jnp.dot(x, wu_bf[...], preferred_element_type=F32)
        a = (g * jax.nn.sigmoid(g) * u).astype(BF16)
        y_ref[...] = jnp.dot(a, wd_bf[...], preferred_element_type=F32)

    @pl.when(i >= nused_ref[0])
    def _():
        y_ref[...] = jnp.zeros_like(y_ref)


def _experts(block_e, first, slot, nxt, n_used, xs, w_gate, w_up, w_down):
    p = xs.shape[0]
    bm = EXPERT_ROWS
    n_blocks = p // bm

    def xrow(i, be, fi, sl, nx, nu):
        return (jnp.minimum(i, nu[0] - 1), 0)

    grid_spec = pltpu.PrefetchScalarGridSpec(
        num_scalar_prefetch=5,
        grid=(n_blocks,),
        in_specs=[
            pl.BlockSpec((bm,) + xs.shape[1:], xrow),
            pl.BlockSpec(memory_space=pl.ANY),
            pl.BlockSpec(memory_space=pl.ANY),
            pl.BlockSpec(memory_space=pl.ANY),
        ],
        out_specs=pl.BlockSpec((bm, D_MODEL), lambda i, *_: (i, 0)),
        scratch_shapes=[
            pltpu.VMEM((2, D_MODEL, D_EXPERT), F32),
            pltpu.VMEM((2, D_MODEL, D_EXPERT), F32),
            pltpu.VMEM((2, D_EXPERT, D_MODEL), F32),
            pltpu.VMEM((D_MODEL, D_EXPERT), BF16),
            pltpu.VMEM((D_MODEL, D_EXPERT), BF16),
            pltpu.VMEM((D_EXPERT, D_MODEL), BF16),
            pltpu.SemaphoreType.DMA((2, 3)),
        ],
    )
    return pl.pallas_call(
        _experts_kernel,
        out_shape=jax.ShapeDtypeStruct((p, D_MODEL), F32),
        grid_spec=grid_spec,
        compiler_params=pltpu.CompilerParams(
            dimension_semantics=("arbitrary",), vmem_limit_bytes=VMEM_LIMIT),
        name="experts",
    )(block_e, first, slot, nxt, n_used, xs, w_gate, w_up, w_down)


def _combine_kernel(dest_ref, dest_next_ref, gate_ref, yb_hbm, o_ref, buf, sems):
    i = pl.program_id(0)
    n = pl.num_programs(0)
    tm = o_ref.shape[0]

    def start_rows(idx_ref, slot):
        def issue(tb, c):
            base = pl.multiple_of(tb * SUBLANES, SUBLANES)
            for r in range(SUBLANES):
                for k in range(TOP_K):
                    idx = k * tm + base + r
                    pltpu.make_async_copy(yb_hbm.at[pl.ds(idx_ref[idx], 1), :], buf.at[slot, pl.ds(idx, 1), :],
                                          sems.at[slot]).start(priority=k % 2)
            return c
        lax.fori_loop(0, tm // SUBLANES, issue, 0)

    @pl.when(i == 0)
    def _():
        start_rows(dest_ref, 0)

    slot = i % 2
    for s in range(2):
        @pl.when((i + 1 < n) & (slot == s))
        def _():
            start_rows(dest_next_ref, 1 - s)

    pltpu.make_async_copy(yb_hbm.at[pl.ds(0, TOP_K * tm), :], buf.at[slot], sems.at[slot]).wait()

    acc = gate_ref[:, 0:1] * buf[slot, 0:tm, :]
    for k in range(1, TOP_K):
        acc = acc + gate_ref[:, k:k + 1] * buf[slot, k * tm:(k + 1) * tm, :]
    o_ref[...] = acc


def _combine(dest, gate_tk, yb):
    t = gate_tk.shape[0]
    tm = LANES
    n_tiles = t // tm
    return pl.pallas_call(
        _combine_kernel,
        out_shape=jax.ShapeDtypeStruct((t, D_MODEL), F32),
        grid=(t // tm,),
        in_specs=[
            pl.BlockSpec((TOP_K * tm,), lambda i: (i,), memory_space=pltpu.SMEM),
            pl.BlockSpec((TOP_K * tm,), lambda i: (jnp.minimum(i + 1, n_tiles - 1),), memory_space=pltpu.SMEM),
            pl.BlockSpec((tm, TOP_K), lambda i: (i, 0)),
            pl.BlockSpec(memory_space=pl.ANY),
        ],
        out_specs=pl.BlockSpec((tm, D_MODEL), lambda i: (i, 0)),
        scratch_shapes=[pltpu.VMEM((2, TOP_K * tm, D_MODEL), F32), pltpu.SemaphoreType.DMA((2,))],
        compiler_params=pltpu.CompilerParams(
            dimension_semantics=("arbitrary",), vmem_limit_bytes=VMEM_LIMIT),
        name="combine",
    )(dest, dest, gate_tk, yb)


def _final_kernel(h1_ref, h1b_ref, routed_ref, p_ref, wsg_ref, wsu_ref, wsd_ref, wpg_ref, wp_ref,
                  g2_ref, b2_ref, y_ref):
    hb = h1b_ref[...]
    g = jnp.dot(hb, wsg_ref[...], preferred_element_type=F32)
    u = jnp.dot(hb, wsu_ref[...], preferred_element_type=F32)
    shared = jnp.dot((g * jax.nn.sigmoid(g) * u).astype(BF16), wsd_ref[...], preferred_element_type=F32)
    gate = jax.nn.sigmoid(jnp.dot(hb, wpg_ref[...], preferred_element_type=F32))
    ple = gate * jnp.dot(p_ref[...], wp_ref[...], preferred_element_type=F32)
    r = DEEPNORM_ALPHA * h1_ref[...] + (routed_ref[...] + shared) + ple
    y_ref[...] = _layer_norm(r, g2_ref[...], b2_ref[...])


def _final(h1, h1b, routed, p_all, ws_gate, ws_up, ws_down, w_ple_gate, w_ple, g2, b2, *, row0, n_rows, tm):
    assert row0 % tm == 0 and n_rows % tm == 0
    blk0 = row0 // tm
    const = lambda i: (0, 0)
    rows = lambda i: (blk0 + i, 0)
    resident = functools.partial(pl.BlockSpec, index_map=const, pipeline_mode=pl.Buffered(1))
    return pl.pallas_call(
        _final_kernel,
        out_shape=jax.ShapeDtypeStruct((n_rows, D_MODEL), F32),
        grid=(n_rows // tm,),
        in_specs=[
            pl.BlockSpec((tm, D_MODEL), rows),
            pl.BlockSpec((tm, D_MODEL), rows),
            pl.BlockSpec((tm, D_MODEL), rows),
            pl.BlockSpec((tm, PLE_DIM), rows),
            resident((D_MODEL, D_SHARED)),
            resident((D_MODEL, D_SHARED)),
            resident((D_SHARED, D_MODEL)),
            resident((D_MODEL, D_MODEL)),
            resident((PLE_DIM, D_MODEL)),
            pl.BlockSpec((1, D_MODEL), const),
            pl.BlockSpec((1, D_MODEL), const),
        ],
        out_specs=pl.BlockSpec((tm, D_MODEL), lambda i: (i, 0)),
        compiler_params=pltpu.CompilerParams(
            dimension_semantics=("arbitrary",), vmem_limit_bytes=VMEM_LIMIT),
        name="final",
    )(h1, h1b, routed, p_all, ws_gate, ws_up, ws_down, w_ple_gate, w_ple, g2, b2)


def _rope_tables(pos):
    half = ROT_DIM // 2
    inv_freq = ROPE_THETA ** (-jnp.arange(half, dtype=F32) * 2.0 / ROT_DIM)
    ang = pos.astype(F32)[:, None] * inv_freq[None, :]
    cos, sin = jnp.cos(ang), jnp.sin(ang)
    n = pos.shape[0]
    ones = jnp.ones((n, HEAD_DIM - ROT_DIM), F32)
    zeros = jnp.zeros((n, HEAD_DIM - ROT_DIM), F32)
    zh = jnp.zeros((n, half), F32)
    cos_t = jnp.concatenate([cos, cos, ones], -1)
    sin_lo = jnp.concatenate([-sin, zh, zeros], -1)
    sin_hi = jnp.concatenate([zh, sin, zeros], -1)
    per_head = jnp.stack([cos_t, sin_lo, sin_hi])
    return jnp.concatenate([per_head] * (LANES // HEAD_DIM), -1)


def _block_plan(counts, n_blocks):
    bm = EXPERT_ROWS
    i32 = jnp.int32
    padded = (counts + bm - 1) // bm * bm
    pad_end = jnp.cumsum(padded)
    pad_start = pad_end - padded
    n_used = pad_end[-1] // bm
    blk = jnp.arange(n_blocks)
    valid = blk < n_used
    block_e = jnp.minimum(jnp.sum(pad_end[None, :] <= (blk * bm)[:, None], axis=1), N_EXPERTS - 1)
    block_e = jnp.where(valid, block_e, block_e[jnp.maximum(n_used - 1, 0)])
    prev_e = jnp.concatenate([jnp.full((1,), -1, block_e.dtype), block_e[:-1]])
    first = valid & (block_e != prev_e)
    run = jnp.cumsum(first) - 1
    run_expert = jnp.full((n_blocks + 2,), -1, block_e.dtype).at[jnp.where(first, run, n_blocks + 1)].set(block_e)
    nxt = jnp.where(first, run_expert[jnp.minimum(run + 1, n_blocks)], -1)
    return (pad_start.astype(i32), (pad_start + counts).astype(i32), (padded - counts).astype(i32),
            n_used.astype(i32).reshape(1), block_e.astype(i32), first.astype(i32), (run % 2).astype(i32),
            nxt.astype(i32))


def kernel(x_prompt, x_sample, p_prompt, p_sample, cache_k, cache_v, state_conv, ln_in_g, ln_in_b, w_in,
           attn_sinks, conv_w, gn_attn, gn_conv, w_out, ln1_g, ln1_b, w_router, router_bias, w_gate, w_up,
           w_down, ws_gate, ws_up, ws_down, w_ple_gate, w_ple, ln2_g, ln2_b):
    n_batch, seq, d = x_prompt.shape
    n_dec, dec_seq, _ = x_sample.shape
    depth = w_in.shape[0]
    wb = cache_k.shape[2]
    assert depth == 1 and dec_seq == 1 and d == D_MODEL
    assert wb == ATTN_BLOCK and seq % ATTN_BLOCK == 0 and n_dec % SAMPLE_CHUNK == 0
    t_p = n_batch * seq
    t = t_p + n_dec
    assert t % ROW_TILE == 0 and t_p % SAMPLE_CHUNK == 0

    x_all = jnp.concatenate([x_prompt.reshape(t_p, d), x_sample.reshape(n_dec, d)], 0)
    p_all = jnp.concatenate([p_prompt[0].reshape(t_p, PLE_DIM), p_sample[0].reshape(n_dec, PLE_DIM)], 0).astype(BF16)
    row = lambda v: v.reshape(1, -1).astype(F32)

    w_in0 = w_in[0]
    qkv_w = ATTN_WIDTH + 2 * KV_WIDTH
    w_in_perm = jnp.concatenate([w_in0[:, qkv_w:], w_in0[:, :qkv_w]], axis=1).astype(BF16)

    proj = _ln_inproj(x_all, row(ln_in_g), row(ln_in_b), w_in_perm)

    rope_p = _rope_tables(jnp.arange(seq))
    rope_s = _rope_tables(PAST_LEN + jnp.arange(dec_seq))[:, 0, :]
    sinks = attn_sinks[0].astype(F32)
    cw, ga, gc = conv_w[0].astype(F32), row(gn_attn[0]), row(gn_conv[0])
    mixed_p, kwin_p, vwin_p, cstate_p = _prompt_mixer(proj, sinks, rope_p, cw, ga, gc, n_batch, seq)
    mixed_s, kwin_s, vwin_s, cstate_s = _sample_mixer(
        proj, t_p, n_dec, sinks, cache_k[0].reshape(n_dec, wb, KV_WIDTH), cache_v[0].reshape(n_dec, wb, KV_WIDTH),
        jnp.swapaxes(state_conv[0], 0, 1), rope_s, cw, ga, gc)
    cstate_s = jnp.swapaxes(cstate_s, 0, 1)
    mixed = jnp.concatenate([mixed_p, mixed_s], 0)

    h1, h1b = _outproj(
        mixed, x_all, row(ln_in_g), row(ln_in_b), w_out[0].astype(BF16), row(ln1_g[0]), row(ln1_b[0]))

    top_e, gate, rank, cnt = _route(h1b, w_router[0].T.astype(BF16), router_bias[0].astype(F32).reshape(N_EXPERTS, 1))
    n_blocks = t * TOP_K // EXPERT_ROWS + N_EXPERTS
    pad_start, fill_start, fill_len, n_used, block_e, first, slot, nxt = _block_plan(cnt[:, 0].astype(jnp.int32), n_blocks)
    xs, dest = _dispatch_rows(pad_start, fill_start, fill_len, n_used, top_e, rank, h1, n_blocks)
    yb = _experts(block_e, first, slot, nxt, n_used, xs, w_gate[0], w_up[0], w_down[0])
    routed = _combine(dest, gate.T, yb)

    final_args = (h1, h1b, routed, p_all, ws_gate[0].astype(BF16), ws_up[0].astype(BF16), ws_down[0].astype(BF16),
                  w_ple_gate[0].astype(BF16), w_ple[0].astype(BF16), row(ln2_g[0]), row(ln2_b[0]))
    y_p = _final(*final_args, row0=0, n_rows=t_p, tm=FINAL_TILE)
    y_s = _final(*final_args, row0=t_p, n_rows=n_dec, tm=n_dec)

    kv_shape = (1, -1, wb, N_KV_HEADS, HEAD_DIM)
    return (y_p.reshape(n_batch, seq, d), y_s.reshape(n_dec, dec_seq, d),
            kwin_p.reshape(kv_shape), vwin_p.reshape(kv_shape), cstate_p[None],
            kwin_s.reshape(kv_shape), vwin_s.reshape(kv_shape), cstate_s[None])
```

```python
import functools

import numpy as np
import jax
import jax.numpy as jnp
from jax import lax
from jax.experimental import pallas as pl
from jax.experimental.pallas import tpu as pltpu

D_MODEL = 2048
N_HEADS = 16
N_KV_HEADS = 4
HEAD_DIM = 64
GQA_GROUP = N_HEADS // N_KV_HEADS
ATTN_WIDTH = N_HEADS * HEAD_DIM
KV_WIDTH = N_KV_HEADS * HEAD_DIM
WINDOW = 128
ATTN_BLOCK = WINDOW
ROT_DIM = HEAD_DIM // 4
ROPE_THETA = 500000.0
CONV_WIDTH = D_MODEL - ATTN_WIDTH
CONV_K = 3
IN_COLS = ATTN_WIDTH + 2 * KV_WIDTH + 3 * CONV_WIDTH
N_EXPERTS = 256
TOP_K = 8
N_EXPERT_GROUPS = 8
TOPK_GROUPS = 4
D_EXPERT = 512
D_SHARED = 512
ROUTED_SCALE = 2.5
PLE_DIM = 256
LN_EPS = 1e-5
DEPTH = 1
PAST_LEN = 8192
DEEPNORM_ALPHA = (2 * DEPTH) ** 0.25

LANES = 128
SUBLANES = 8
COL_B, COL_C, COL_X, COL_Q, COL_KV = 0, CONV_WIDTH, 2 * CONV_WIDTH, 3 * CONV_WIDTH, 3 * CONV_WIDTH + ATTN_WIDTH

ROW_TILE = 640
INPROJ_COL_TILE = 1536
FINAL_TILE = 256
EXPERT_ROWS = 128
SAMPLE_CHUNK = 8
VMEM_LIMIT = 56 * 1024 * 1024

BF16 = jnp.bfloat16
F32 = jnp.float32


def _layer_norm(x, g, b):
    xc = x - jnp.mean(x, -1, keepdims=True)
    var = jnp.mean(xc * xc, -1, keepdims=True)
    return xc * lax.rsqrt(var + LN_EPS) * g + b


def _rms_norm(x, g):
    return x * lax.rsqrt(jnp.mean(x * x, -1, keepdims=True) + LN_EPS) * g


def _rope_lanes(x, cos_t, sin_lo, sin_hi):
    half = ROT_DIM // 2
    ax = x.ndim - 1
    return x * cos_t + pltpu.roll(x, LANES - half, ax) * sin_lo + pltpu.roll(x, half, ax) * sin_hi


def _rope_wide(x, cos_t, sin_lo, sin_hi):
    n = x.shape[-1] // LANES
    return jnp.concatenate(
        [_rope_lanes(x[..., c * LANES:(c + 1) * LANES], cos_t, sin_lo, sin_hi) for c in range(n)], axis=-1)


def _ln_inproj_kernel(x_ref, g_ref, b_ref, w_ref, o_ref, h_scr):
    @pl.when(pl.program_id(1) == 0)
    def _():
        h_scr[...] = _layer_norm(x_ref[...], g_ref[...], b_ref[...]).astype(BF16)

    o_ref[...] = jnp.dot(h_scr[...], w_ref[...], preferred_element_type=F32)


def _ln_inproj(x_all, g, b, w_bf16):
    t = x_all.shape[0]
    return pl.pallas_call(
        _ln_inproj_kernel,
        out_shape=jax.ShapeDtypeStruct((t, IN_COLS), F32),
        grid=(t // ROW_TILE, IN_COLS // INPROJ_COL_TILE),
        in_specs=[
            pl.BlockSpec((ROW_TILE, D_MODEL), lambda i, j: (i, 0)),
            pl.BlockSpec((1, D_MODEL), lambda i, j: (0, 0)),
            pl.BlockSpec((1, D_MODEL), lambda i, j: (0, 0)),
            pl.BlockSpec((D_MODEL, INPROJ_COL_TILE), lambda i, j: (0, j)),
        ],
        out_specs=pl.BlockSpec((ROW_TILE, INPROJ_COL_TILE), lambda i, j: (i, j)),
        scratch_shapes=[pltpu.VMEM((ROW_TILE, D_MODEL), BF16)],
        compiler_params=pltpu.CompilerParams(
            dimension_semantics=("arbitrary", "arbitrary"), vmem_limit_bytes=VMEM_LIMIT),
        name="ln_inproj",
    )(x_all, g, b, w_bf16)


def _prompt_mixer_kernel(sink_ref, b_ref, c_ref, x_ref, q_ref, kv_ref, kvp_ref, cp_ref, xp_ref,
                         rope_ref, ropep_ref, convw_ref, gna_ref, gnc_ref,
                         mixed_ref, kwin_ref, vwin_ref, cstate_ref, *, n_blocks):
    i = pl.program_id(1)
    blk = ATTN_BLOCK
    cos_t, sin_lo, sin_hi = rope_ref[0], rope_ref[1], rope_ref[2]
    qr = _rope_wide(q_ref[...], cos_t, sin_lo, sin_hi) * (HEAD_DIM ** -0.5)
    k_cur = _rope_wide(kv_ref[:, :KV_WIDTH], cos_t, sin_lo, sin_hi)
    k_prev = _rope_wide(kvp_ref[:, :KV_WIDTH], ropep_ref[0], ropep_ref[1], ropep_ref[2])
    v_cur = kv_ref[:, KV_WIDTH:]
    v_prev = kvp_ref[:, KV_WIDTH:]
    k_all = jnp.concatenate([k_prev, k_cur], axis=0).astype(BF16)
    v_all = jnp.concatenate([v_prev, v_cur], axis=0).astype(BF16)

    qi = lax.broadcasted_iota(jnp.int32, (blk, 2 * blk), 0)
    sj = lax.broadcasted_iota(jnp.int32, (blk, 2 * blk), 1)
    rel = qi + blk - sj
    mask = (rel >= 0) & (rel <= WINDOW) & ((sj >= blk) | (i > 0))
    mask = jnp.concatenate([mask] * GQA_GROUP, axis=0)

    heads = []
    for g in range(N_KV_HEADS):
        qg = jnp.concatenate(
            [qr[:, (g * GQA_GROUP + hh) * HEAD_DIM:(g * GQA_GROUP + hh + 1) * HEAD_DIM] for hh in range(GQA_GROUP)],
            axis=0).astype(BF16)
        kg = k_all[:, g * HEAD_DIM:(g + 1) * HEAD_DIM]
        vg = v_all[:, g * HEAD_DIM:(g + 1) * HEAD_DIM]
        s = lax.dot_general(qg, kg, (((1,), (1,)), ((), ())), preferred_element_type=F32)
        s = jnp.where(mask, s, -jnp.inf)
        sink = jnp.concatenate(
            [jnp.full((blk, 1), sink_ref[g * GQA_GROUP + hh], F32) for hh in range(GQA_GROUP)], axis=0)
        m = jnp.maximum(jnp.max(s, -1, keepdims=True), sink)
        e = jnp.exp(s - m)
        p = e / (jnp.sum(e, -1, keepdims=True) + jnp.exp(sink - m))
        og = jnp.dot(p.astype(BF16), vg, preferred_element_type=F32)
        heads += [og[hh * blk:(hh + 1) * blk] for hh in range(GQA_GROUP)]
    attn = jnp.concatenate(heads, axis=-1)
    mixed_ref[:, :ATTN_WIDTH] = _rms_norm(attn, gna_ref[...]).astype(BF16)

    u = c_ref[...] * x_ref[...]
    n_tail = CONV_K - 1
    tail_rows = cp_ref.shape[0]
    row_id = lax.broadcasted_iota(jnp.int32, u.shape, 0)
    acc = u * convw_ref[CONV_K - 1:CONV_K, :]
    for d in range(1, CONV_K):
        ud = pltpu.roll(u, d, 0)
        for r in range(d):
            src = tail_rows - d + r
            up = jnp.where(i > 0, cp_ref[src:src + 1, :] * xp_ref[src:src + 1, :], 0.0)
            ud = jnp.where(row_id == r, up, ud)
        acc = acc + ud * convw_ref[CONV_K - 1 - d:CONV_K - d, :]
    mixed_ref[:, ATTN_WIDTH:] = _rms_norm(b_ref[...] * acc, gnc_ref[...]).astype(BF16)

    @pl.when(i == n_blocks - 1)
    def _():
        kwin_ref[0] = k_cur
        vwin_ref[0] = v_cur
        cstate_ref[0] = u[blk - n_tail:, :]


def _prompt_mixer(proj, sinks, rope_tab, conv_w, gn_attn, gn_conv, n_batch, seq):
    blk = ATTN_BLOCK
    nb = seq // blk
    tail = 8
    wide = CONV_WIDTH // 1
    kvb = COL_KV // (2 * KV_WIDTH)

    def row(b, i, s):
        return b * nb + i

    def prev_row(b, i, s):
        return b * nb + jnp.maximum(i - 1, 0)

    def prev_tail(b, i, s):
        return jnp.maximum((b * nb + i) * (blk // tail) - 1, 0)

    grid_spec = pltpu.PrefetchScalarGridSpec(
        num_scalar_prefetch=1,
        grid=(n_batch, nb),
        in_specs=[
            pl.BlockSpec((blk, wide), lambda b, i, s: (row(b, i, s), COL_B // wide)),
            pl.BlockSpec((blk, wide), lambda b, i, s: (row(b, i, s), COL_C // wide)),
            pl.BlockSpec((blk, wide), lambda b, i, s: (row(b, i, s), COL_X // wide)),
            pl.BlockSpec((blk, ATTN_WIDTH), lambda b, i, s: (row(b, i, s), COL_Q // ATTN_WIDTH)),
            pl.BlockSpec((blk, 2 * KV_WIDTH), lambda b, i, s: (row(b, i, s), kvb)),
            pl.BlockSpec((blk, 2 * KV_WIDTH), lambda b, i, s: (prev_row(b, i, s), kvb)),
            pl.BlockSpec((tail, wide), lambda b, i, s: (prev_tail(b, i, s), COL_C // wide)),
            pl.BlockSpec((tail, wide), lambda b, i, s: (prev_tail(b, i, s), COL_X // wide)),
            pl.BlockSpec((3, blk, LANES), lambda b, i, s: (0, i, 0)),
            pl.BlockSpec((3, blk, LANES), lambda b, i, s: (0, jnp.maximum(i - 1, 0), 0)),
            pl.BlockSpec((CONV_K, CONV_WIDTH), lambda b, i, s: (0, 0)),
            pl.BlockSpec((1, ATTN_WIDTH), lambda b, i, s: (0, 0)),
            pl.BlockSpec((1, CONV_WIDTH), lambda b, i, s: (0, 0)),
        ],
        out_specs=[
            pl.BlockSpec((blk, D_MODEL), lambda b, i, s: (row(b, i, s), 0)),
            pl.BlockSpec((1, blk, KV_WIDTH), lambda b, i, s: (b, 0, 0)),
            pl.BlockSpec((1, blk, KV_WIDTH), lambda b, i, s: (b, 0, 0)),
            pl.BlockSpec((1, CONV_K - 1, CONV_WIDTH), lambda b, i, s: (b, 0, 0)),
        ],
    )
    return pl.pallas_call(
        functools.partial(_prompt_mixer_kernel, n_blocks=nb),
        out_shape=[
            jax.ShapeDtypeStruct((n_batch * seq, D_MODEL), BF16),
            jax.ShapeDtypeStruct((n_batch, blk, KV_WIDTH), F32),
            jax.ShapeDtypeStruct((n_batch, blk, KV_WIDTH), F32),
            jax.ShapeDtypeStruct((n_batch, CONV_K - 1, CONV_WIDTH), F32),
        ],
        grid_spec=grid_spec,
        compiler_params=pltpu.CompilerParams(
            dimension_semantics=("arbitrary", "arbitrary"), vmem_limit_bytes=VMEM_LIMIT),
        name="prompt_mixer",
    )(sinks, proj, proj, proj, proj, proj, proj, proj, proj, rope_tab, rope_tab, conv_w, gn_attn, gn_conv)


def _expand_groups(t):
    ax = t.ndim - 1
    lane = lax.broadcasted_iota(jnp.int32, t.shape[:-1] + (LANES,), ax)
    chunks = []
    for c in range(KV_WIDTH // LANES):
        a = t[..., c * LANES:(c + 1) * LANES]
        r = pltpu.roll(a, HEAD_DIM, ax)
        lo = jnp.where(lane < HEAD_DIM, a, r)
        hi = jnp.where(lane < HEAD_DIM, r, a)
        chunks += [lo] * (GQA_GROUP // 2) + [hi] * (GQA_GROUP // 2)
    return jnp.concatenate(chunks, axis=-1)


def _sample_mixer_kernel(sink_ref, b_ref, c_ref, x_ref, q_ref, kv_ref, ck_ref, cv_ref, st_ref,
                         rope_ref, seg_ref, segt_ref, convw_ref, gna_ref, gnc_ref,
                         mixed_ref, kwin_ref, vwin_ref, cstate_ref):
    nb, wb = ck_ref.shape[0], ck_ref.shape[1]
    cos_t, sin_lo, sin_hi = rope_ref[0:1, :], rope_ref[1:2, :], rope_ref[2:3, :]
    qr = _rope_wide(q_ref[...], cos_t, sin_lo, sin_hi) * (HEAD_DIM ** -0.5)
    k_new = _rope_wide(kv_ref[:, :KV_WIDTH], cos_t, sin_lo, sin_hi)
    v_new = kv_ref[:, KV_WIDTH:]
    ck = ck_ref[...]
    cv = cv_ref[...]

    seg = seg_ref[...]
    prod = _expand_groups(ck) * qr[:, None, :]
    s_old = jnp.dot(prod.reshape(nb * wb, ATTN_WIDTH).astype(BF16), seg,
                    preferred_element_type=F32).reshape(nb, wb, N_HEADS)
    s_new = jnp.dot((_expand_groups(k_new) * qr).astype(BF16), seg, preferred_element_type=F32)
    sink = sink_ref[...]
    m = jnp.maximum(jnp.maximum(jnp.max(s_old, axis=1), s_new), sink)
    e_old = jnp.exp(s_old - m[:, None, :])
    e_new = jnp.exp(s_new - m)
    inv = 1.0 / (jnp.sum(e_old, axis=1) + e_new + jnp.exp(sink - m))
    p_old = (e_old * inv[:, None, :]).astype(BF16)
    p_new = (e_new * inv).astype(BF16)
    segt = segt_ref[...]
    pe_old = jnp.dot(p_old.reshape(nb * wb, N_HEADS), segt, preferred_element_type=F32).reshape(nb, wb, ATTN_WIDTH)
    pe_new = jnp.dot(p_new, segt, preferred_element_type=F32)
    attn = jnp.sum(pe_old * _expand_groups(cv), axis=1) + pe_new * _expand_groups(v_new)
    mixed_ref[:, :ATTN_WIDTH] = _rms_norm(attn, gna_ref[...]).astype(BF16)

    u = c_ref[...] * x_ref[...]
    acc = u * convw_ref[CONV_K - 1:CONV_K, :]
    for j in range(CONV_K - 1):
        acc = acc + st_ref[j] * convw_ref[j:j + 1, :]
    mixed_ref[:, ATTN_WIDTH:] = _rms_norm(b_ref[...] * acc, gnc_ref[...]).astype(BF16)

    kwin_ref[:, 0:wb - 1, :] = ck_ref[:, 1:wb, :]
    vwin_ref[:, 0:wb - 1, :] = cv_ref[:, 1:wb, :]
    for n in range(nb):
        kwin_ref[n, wb - 1:wb, :] = k_new[n:n + 1, :]
        vwin_ref[n, wb - 1:wb, :] = v_new[n:n + 1, :]
    for j in range(CONV_K - 2):
        cstate_ref[j] = st_ref[j + 1]
    cstate_ref[CONV_K - 2] = u


def _sample_mixer(proj, row0, n_dec, sinks, cache_k, cache_v, state_conv, rope_row, conv_w, gn_attn, gn_conv):
    nb = SAMPLE_CHUNK
    wb = cache_k.shape[1]
    r0 = row0 // nb
    wide = CONV_WIDTH
    head_of_lane = np.arange(ATTN_WIDTH) // HEAD_DIM
    seg = jnp.asarray(head_of_lane[:, None] == np.arange(N_HEADS)[None, :], BF16)
    kvb = COL_KV // (2 * KV_WIDTH)
    return pl.pallas_call(
        _sample_mixer_kernel,
        out_shape=[
            jax.ShapeDtypeStruct((n_dec, D_MODEL), BF16),
            jax.ShapeDtypeStruct((n_dec, wb, KV_WIDTH), F32),
            jax.ShapeDtypeStruct((n_dec, wb, KV_WIDTH), F32),
            jax.ShapeDtypeStruct((CONV_K - 1, n_dec, CONV_WIDTH), F32),
        ],
        grid=(n_dec // nb,),
        in_specs=[
            pl.BlockSpec((1, N_HEADS), lambda i: (0, 0)),
            pl.BlockSpec((nb, wide), lambda i: (r0 + i, COL_B // wide)),
            pl.BlockSpec((nb, wide), lambda i: (r0 + i, COL_C // wide)),
            pl.BlockSpec((nb, wide), lambda i: (r0 + i, COL_X // wide)),
            pl.BlockSpec((nb, ATTN_WIDTH), lambda i: (r0 + i, COL_Q // ATTN_WIDTH)),
            pl.BlockSpec((nb, 2 * KV_WIDTH), lambda i: (r0 + i, kvb)),
            pl.BlockSpec((nb, wb, KV_WIDTH), lambda i: (i, 0, 0)),
            pl.BlockSpec((nb, wb, KV_WIDTH), lambda i: (i, 0, 0)),
            pl.BlockSpec((CONV_K - 1, nb, CONV_WIDTH), lambda i: (0, i, 0)),
            pl.BlockSpec((3, LANES), lambda i: (0, 0)),
            pl.BlockSpec((ATTN_WIDTH, N_HEADS), lambda i: (0, 0)),
            pl.BlockSpec((N_HEADS, ATTN_WIDTH), lambda i: (0, 0)),
            pl.BlockSpec((CONV_K, CONV_WIDTH), lambda i: (0, 0)),
            pl.BlockSpec((1, ATTN_WIDTH), lambda i: (0, 0)),
            pl.BlockSpec((1, CONV_WIDTH), lambda i: (0, 0)),
        ],
        out_specs=[
            pl.BlockSpec((nb, D_MODEL), lambda i: (i, 0)),
            pl.BlockSpec((nb, wb, KV_WIDTH), lambda i: (i, 0, 0)),
            pl.BlockSpec((nb, wb, KV_WIDTH), lambda i: (i, 0, 0)),
            pl.BlockSpec((CONV_K - 1, nb, CONV_WIDTH), lambda i: (0, i, 0)),
        ],
        compiler_params=pltpu.CompilerParams(
            dimension_semantics=("arbitrary",), vmem_limit_bytes=VMEM_LIMIT),
        name="sample_mixer",
    )(sinks.reshape(1, N_HEADS), proj, proj, proj, proj, proj, cache_k, cache_v, state_conv,
      rope_row, seg, seg.T, conv_w, gn_attn, gn_conv)


def _outproj_kernel(mixed_ref, x_ref, g0_ref, b0_ref, wout_ref, g1_ref, b1_ref, h1_ref, h1b_ref):
    h = _layer_norm(x_ref[...], g0_ref[...], b0_ref[...])
    mixed = jnp.dot(mixed_ref[...], wout_ref[...], preferred_element_type=F32)
    h1 = _layer_norm(DEEPNORM_ALPHA * h + mixed, g1_ref[...], b1_ref[...])
    h1_ref[...] = h1
    h1b_ref[...] = h1.astype(BF16)


def _outproj(mixed, x_all, g0, b0, w_out, g1, b1):
    t = x_all.shape[0]
    tm = ROW_TILE // 2
    const = lambda i: (0, 0)
    resident = functools.partial(pl.BlockSpec, index_map=const, pipeline_mode=pl.Buffered(1))
    return pl.pallas_call(
        _outproj_kernel,
        out_shape=[
            jax.ShapeDtypeStruct((t, D_MODEL), F32),
            jax.ShapeDtypeStruct((t, D_MODEL), BF16),
        ],
        grid=(t // tm,),
        in_specs=[
            pl.BlockSpec((tm, D_MODEL), lambda i: (i, 0)),
            pl.BlockSpec((tm, D_MODEL), lambda i: (i, 0)),
            pl.BlockSpec((1, D_MODEL), const),
            pl.BlockSpec((1, D_MODEL), const),
            resident((D_MODEL, D_MODEL)),
            pl.BlockSpec((1, D_MODEL), const),
            pl.BlockSpec((1, D_MODEL), const),
        ],
        out_specs=[
            pl.BlockSpec((tm, D_MODEL), lambda i: (i, 0)),
            pl.BlockSpec((tm, D_MODEL), lambda i: (i, 0)),
        ],
        compiler_params=pltpu.CompilerParams(
            dimension_semantics=("arbitrary",), vmem_limit_bytes=VMEM_LIMIT),
        name="outproj",
    )(mixed, x_all, g0, b0, w_out, g1, b1)


def _route_kernel(h_ref, wrt_ref, bias_ref, tri_ref, ones_ref, e_ref, gate_ref, rank_ref, cnt_ref):
    tm = h_ref.shape[0]
    neg = -jnp.inf
    big = float(N_EXPERTS)

    @pl.when(pl.program_id(0) == 0)
    def _():
        cnt_ref[...] = jnp.zeros_like(cnt_ref)

    logits = lax.dot_general(wrt_ref[...], h_ref[...], (((1,), (1,)), ((), ())), preferred_element_type=F32)
    scores = jax.nn.sigmoid(logits)
    biased = scores + bias_ref[...]

    per_group = N_EXPERTS // N_EXPERT_GROUPS
    rid = lax.broadcasted_iota(jnp.int32, (per_group, tm), 0).astype(F32)
    gscore = []
    for g in range(N_EXPERT_GROUPS):
        xg = biased[g * per_group:(g + 1) * per_group]
        m1 = jnp.max(xg, 0, keepdims=True)
        i1 = jnp.min(jnp.where(xg == m1, rid, big), 0, keepdims=True)
        m2 = jnp.max(jnp.where(rid == i1, neg, xg), 0, keepdims=True)
        gscore.append(m1 + m2)
    masked = []
    for g in range(N_EXPERT_GROUPS):
        beaten = jnp.zeros((1, tm), F32)
        for o in range(N_EXPERT_GROUPS):
            if o != g:
                wins = (gscore[o] > gscore[g]) | ((gscore[o] == gscore[g]) & (o < g))
                beaten = beaten + jnp.where(wins, 1.0, 0.0)
        masked.append(jnp.where(beaten < TOPK_GROUPS, biased[g * per_group:(g + 1) * per_group], neg))
    x = jnp.concatenate(masked, axis=0)

    eid = lax.broadcasted_iota(jnp.int32, (N_EXPERTS, tm), 0).astype(F32)
    sel = jnp.zeros((N_EXPERTS, tm), F32)
    picks, gates = [], []
    for _ in range(TOP_K):
        m = jnp.max(x, 0, keepdims=True)
        ik = jnp.min(jnp.where(x == m, eid, big), 0, keepdims=True)
        hit = eid == ik
        gates.append(jnp.sum(jnp.where(hit, scores, 0.0), 0, keepdims=True))
        picks.append(ik)
        x = jnp.where(hit, neg, x)
        sel = jnp.where(hit, 1.0, sel)

    selb = sel.astype(BF16)
    pos = cnt_ref[...] + jnp.dot(selb, tri_ref[...], preferred_element_type=F32)
    cnt_ref[...] = cnt_ref[...] + jnp.dot(selb, ones_ref[...], preferred_element_type=F32)

    gsum = gates[0]
    for k in range(1, TOP_K):
        gsum = gsum + gates[k]
    for k in range(TOP_K):
        e_ref[k:k + 1, :] = picks[k].astype(jnp.int32)
        gate_ref[k:k + 1, :] = gates[k] / gsum * ROUTED_SCALE
        rank_ref[k:k + 1, :] = jnp.sum(jnp.where(eid == picks[k], pos, 0.0), 0, keepdims=True).astype(jnp.int32)


def _route(h1b, w_router_t, bias_col):
    t = h1b.shape[0]
    tm = LANES
    tri = jnp.asarray(np.arange(tm)[:, None] < np.arange(tm)[None, :], BF16)
    ones = jnp.ones((tm, tm), BF16)
    const = lambda i: (0, 0)
    return pl.pallas_call(
        _route_kernel,
        out_shape=[
            jax.ShapeDtypeStruct((TOP_K, t), jnp.int32),
            jax.ShapeDtypeStruct((TOP_K, t), F32),
            jax.ShapeDtypeStruct((TOP_K, t), jnp.int32),
            jax.ShapeDtypeStruct((N_EXPERTS, tm), F32),
        ],
        grid=(t // tm,),
        in_specs=[
            pl.BlockSpec((tm, D_MODEL), lambda i: (i, 0)),
            pl.BlockSpec((N_EXPERTS, D_MODEL), const),
            pl.BlockSpec((N_EXPERTS, 1), const),
            pl.BlockSpec((tm, tm), const),
            pl.BlockSpec((tm, tm), const),
        ],
        out_specs=[
            pl.BlockSpec((TOP_K, tm), lambda i: (0, i)),
            pl.BlockSpec((TOP_K, tm), lambda i: (0, i)),
            pl.BlockSpec((TOP_K, tm), lambda i: (0, i)),
            pl.BlockSpec((N_EXPERTS, tm), const),
        ],
        compiler_params=pltpu.CompilerParams(
            dimension_semantics=("arbitrary",), vmem_limit_bytes=VMEM_LIMIT),
        name="route",
    )(h1b, w_router_t, bias_col, tri, ones)


def _pow2_chunks(n_max):
    return [1 << b for b in reversed(range(int(n_max).bit_length()))]


def _dest_kernel(pstart_ref, e_ref, rank_ref, dest_ref):
    e = e_ref[...]

    def body(x, acc):
        return jnp.where(e == x, pstart_ref[x], acc)
    dest_ref[...] = lax.fori_loop(0, N_EXPERTS, body, jnp.zeros_like(e)) + rank_ref[...]


def _dest_rows(pad_start, top_e, rank):
    whole = pl.BlockSpec(top_e.shape, lambda i, ps: (0, 0))
    return pl.pallas_call(
        _dest_kernel,
        out_shape=jax.ShapeDtypeStruct(top_e.shape, jnp.int32),
        grid_spec=pltpu.PrefetchScalarGridSpec(num_scalar_prefetch=1, grid=(1,), in_specs=[whole, whole],
                                               out_specs=whole),
        name="dest_rows",
    )(pad_start, top_e, rank)


def _dispatch_kernel(fill0_ref, filln_ref, nused_ref, dest_ref, h1_ref,
                     xs_hbm, zeros_vmem, row_sem, fill_sem, *, n_blocks):
    i = pl.program_id(0)
    tm = h1_ref.shape[0]
    bm = EXPERT_ROWS

    def fill_copies(e):
        n = filln_ref[e]
        start = fill0_ref[e]
        head = jnp.minimum((SUBLANES - start % SUBLANES) % SUBLANES, n)
        out = []
        for r in range(SUBLANES - 1):
            out.append((r < head, pltpu.make_async_copy(
                zeros_vmem.at[pl.ds(0, 1), :], xs_hbm.at[pl.ds(start + r, 1), :], fill_sem)))
        body0 = start + head
        m = n - head
        for c in _pow2_chunks(bm - 1):
            if c < SUBLANES:
                continue
            off = pl.multiple_of(body0 + (m // (2 * c)) * (2 * c), SUBLANES)
            out.append(((m // c) % 2 == 1, pltpu.make_async_copy(
                zeros_vmem.at[pl.ds(0, c), :], xs_hbm.at[pl.ds(off, c), :], fill_sem)))
        return out

    def tail_copy(j):
        return pltpu.make_async_copy(zeros_vmem, xs_hbm.at[pl.ds(pl.multiple_of(j * bm, bm), bm), :], fill_sem)

    @pl.when(i == 0)
    def _():
        zeros_vmem[...] = jnp.zeros_like(zeros_vmem)

        def start_fill(e, c):
            for pred, cp in fill_copies(e):
                @pl.when(pred)
                def _():
                    cp.start(priority=1)
            return c
        lax.fori_loop(0, N_EXPERTS, start_fill, 0)

        def start_tail(j, c):
            tail_copy(j).start(priority=1)
            return c
        lax.fori_loop(nused_ref[0], n_blocks, start_tail, 0)

    for t in range(tm):
        for k in range(TOP_K):
            pltpu.make_async_copy(h1_ref.at[pl.ds(t, 1), :], xs_hbm.at[pl.ds(dest_ref[k * tm + t], 1), :],
                                  row_sem).start()
    for _ in range(TOP_K):
        pltpu.make_async_copy(h1_ref, xs_hbm.at[pl.ds(0, tm), :], row_sem).wait()

    @pl.when(i == pl.num_programs(0) - 1)
    def _():
        def wait_fill(e, c):
            for pred, cp in fill_copies(e):
                @pl.when(pred)
                def _():
                    cp.wait()
            return c
        lax.fori_loop(0, N_EXPERTS, wait_fill, 0)

        def wait_tail(j, c):
            tail_copy(j).wait()
            return c
        lax.fori_loop(nused_ref[0], n_blocks, wait_tail, 0)


def _tile_major(a, tm):
    k, t = a.shape
    return a.reshape(k, t // tm, tm).swapaxes(0, 1).reshape(-1)


def _dispatch_rows(fill_start, fill_len, n_used, dest, h1, n_blocks):
    t, width = h1.shape
    tm = LANES
    grid_spec = pltpu.PrefetchScalarGridSpec(
        num_scalar_prefetch=3,
        grid=(t // tm,),
        in_specs=[pl.BlockSpec((TOP_K * tm,), lambda i, *_: (i,), memory_space=pltpu.SMEM),
                  pl.BlockSpec((tm, width), lambda i, *_: (i, 0))],
        out_specs=pl.BlockSpec(memory_space=pl.ANY),
        scratch_shapes=[pltpu.VMEM((EXPERT_ROWS, width), h1.dtype),
                        pltpu.SemaphoreType.DMA, pltpu.SemaphoreType.DMA],
    )
    return pl.pallas_call(
        functools.partial(_dispatch_kernel, n_blocks=n_blocks),
        out_shape=jax.ShapeDtypeStruct((n_blocks * EXPERT_ROWS, width), h1.dtype),
        grid_spec=grid_spec,
        compiler_params=pltpu.CompilerParams(dimension_semantics=("arbitrary",)),
        name="dispatch",
    )(fill_start, fill_len, n_used, dest, h1)


def _experts_kernel(be_ref, first_ref, slot_ref, nxt_ref, nused_ref, x_ref, wg_hbm, wu_hbm, wd_hbm, y_ref,
                    wg_buf, wu_buf, wd_buf, wg_bf, wu_bf, wd_bf, sems):
    i = pl.program_id(0)

    def weight_copies(e, slot):
        return (pltpu.make_async_copy(wg_hbm.at[e], wg_buf.at[slot], sems.at[slot, 0]),
                pltpu.make_async_copy(wu_hbm.at[e], wu_buf.at[slot], sems.at[slot, 1]),
                pltpu.make_async_copy(wd_hbm.at[e], wd_buf.at[slot], sems.at[slot, 2]))

    @pl.when(i == 0)
    def _():
        for cp in weight_copies(be_ref[0], 0):
            cp.start(priority=1)

    @pl.when(first_ref[i] == 1)
    def _():
        slot = slot_ref[i]
        for cp in weight_copies(be_ref[i], slot):
            cp.wait()

        @pl.when(nxt_ref[i] >= 0)
        def _():
            for cp in weight_copies(nxt_ref[i], 1 - slot):
                cp.start(priority=1)

        wg_bf[...] = wg_buf[slot].astype(BF16)
        wu_bf[...] = wu_buf[slot].astype(BF16)
        wd_bf[...] = wd_buf[slot].astype(BF16)

    @pl.when(i < nused_ref[0])
    def _():
        x = x_ref[...].astype(BF16)
        g = jnp.dot(x, wg_bf[...], preferred_element_type=F32)
        u = jnp.dot(x, wu_bf[...], preferred_element_type=F32)
        a = (g * jax.nn.sigmoid(g) * u).astype(BF16)
        y_ref[...] = jnp.dot(a, wd_bf[...], preferred_element_type=F32)

    @pl.when(i >= nused_ref[0])
    def _():
        y_ref[...] = jnp.zeros_like(y_ref)


def _experts(block_e, first, slot, nxt, n_used, xs, w_gate, w_up, w_down):
    p = xs.shape[0]
    bm = EXPERT_ROWS
    n_blocks = p // bm

    def xrow(i, be, fi, sl, nx, nu):
        return (jnp.minimum(i, nu[0] - 1), 0)

    grid_spec = pltpu.PrefetchScalarGridSpec(
        num_scalar_prefetch=5,
        grid=(n_blocks,),
        in_specs=[
            pl.BlockSpec((bm,) + xs.shape[1:], xrow),
            pl.BlockSpec(memory_space=pl.ANY),
            pl.BlockSpec(memory_space=pl.ANY),
            pl.BlockSpec(memory_space=pl.ANY),
        ],
        out_specs=pl.BlockSpec((bm, D_MODEL), lambda i, *_: (i, 0)),
        scratch_shapes=[
            pltpu.VMEM((2, D_MODEL, D_EXPERT), F32),
            pltpu.VMEM((2, D_MODEL, D_EXPERT), F32),
            pltpu.VMEM((2, D_EXPERT, D_MODEL), F32),
            pltpu.VMEM((D_MODEL, D_EXPERT), BF16),
            pltpu.VMEM((D_MODEL, D_EXPERT), BF16),
            pltpu.VMEM((D_EXPERT, D_MODEL), BF16),
            pltpu.SemaphoreType.DMA((2, 3)),
        ],
    )
    return pl.pallas_call(
        _experts_kernel,
        out_shape=jax.ShapeDtypeStruct((p, D_MODEL), F32),
        grid_spec=grid_spec,
        compiler_params=pltpu.CompilerParams(
            dimension_semantics=("arbitrary",), vmem_limit_bytes=VMEM_LIMIT),
        name="experts",
    )(block_e, first, slot, nxt, n_used, xs, w_gate, w_up, w_down)


def _combine_kernel(dest_ref, gate_ref, yb_hbm, o_ref, buf, sems):
    i = pl.program_id(0)
    n_tiles = pl.num_programs(0) - 1
    tm = o_ref.shape[0]
    n_rows = TOP_K * tm

    for s in range(2):
        @pl.when((i < n_tiles) & (i % 2 == s))
        def _():
            for idx in range(n_rows):
                pltpu.make_async_copy(yb_hbm.at[pl.ds(dest_ref[idx], 1), :], buf.at[s, pl.ds(idx, 1), :],
                                      sems.at[s]).start(priority=idx % 2)

    @pl.when(i >= 1)
    def _():
        slot = (i - 1) % 2
        pltpu.make_async_copy(yb_hbm.at[pl.ds(0, n_rows), :], buf.at[slot], sems.at[slot]).wait()
        acc = gate_ref[:, 0:1] * buf[slot, 0:tm, :]
        for k in range(1, TOP_K):
            acc = acc + gate_ref[:, k:k + 1] * buf[slot, k * tm:(k + 1) * tm, :]
        o_ref[...] = acc


def _combine(dest, gate_tk, yb):
    t = gate_tk.shape[0]
    tm = LANES
    n_tiles = t // tm
    return pl.pallas_call(
        _combine_kernel,
        out_shape=jax.ShapeDtypeStruct((t, D_MODEL), F32),
        grid=(n_tiles + 1,),
        in_specs=[
            pl.BlockSpec((TOP_K * tm,), lambda i: (jnp.minimum(i, n_tiles - 1),), memory_space=pltpu.SMEM),
            pl.BlockSpec((tm, TOP_K), lambda i: (jnp.maximum(i - 1, 0), 0)),
            pl.BlockSpec(memory_space=pl.ANY),
        ],
        out_specs=pl.BlockSpec((tm, D_MODEL), lambda i: (jnp.maximum(i - 1, 0), 0)),
        scratch_shapes=[pltpu.VMEM((2, TOP_K * tm, D_MODEL), F32), pltpu.SemaphoreType.DMA((2,))],
        compiler_params=pltpu.CompilerParams(
            dimension_semantics=("arbitrary",), vmem_limit_bytes=VMEM_LIMIT),
        name="combine",
    )(dest, gate_tk, yb)


def _final_kernel(h1_ref, h1b_ref, routed_ref, p_ref, wsg_ref, wsu_ref, wsd_ref, wpg_ref, wp_ref,
                  g2_ref, b2_ref, y_ref):
    hb = h1b_ref[...]
    g = jnp.dot(hb, wsg_ref[...], preferred_element_type=F32)
    u = jnp.dot(hb, wsu_ref[...], preferred_element_type=F32)
    shared = jnp.dot((g * jax.nn.sigmoid(g) * u).astype(BF16), wsd_ref[...], preferred_element_type=F32)
    gate = jax.nn.sigmoid(jnp.dot(hb, wpg_ref[...], preferred_element_type=F32))
    ple = gate * jnp.dot(p_ref[...], wp_ref[...], preferred_element_type=F32)
    r = DEEPNORM_ALPHA * h1_ref[...] + (routed_ref[...] + shared) + ple
    y_ref[...] = _layer_norm(r, g2_ref[...], b2_ref[...])


def _final(h1, h1b, routed, p_all, ws_gate, ws_up, ws_down, w_ple_gate, w_ple, g2, b2, *, row0, n_rows, tm):
    assert row0 % tm == 0 and n_rows % tm == 0
    blk0 = row0 // tm
    const = lambda i: (0, 0)
    rows = lambda i: (blk0 + i, 0)
    resident = functools.partial(pl.BlockSpec, index_map=const, pipeline_mode=pl.Buffered(1))
    return pl.pallas_call(
        _final_kernel,
        out_shape=jax.ShapeDtypeStruct((n_rows, D_MODEL), F32),
        grid=(n_rows // tm,),
        in_specs=[
            pl.BlockSpec((tm, D_MODEL), rows),
            pl.BlockSpec((tm, D_MODEL), rows),
            pl.BlockSpec((tm, D_MODEL), rows),
            pl.BlockSpec((tm, PLE_DIM), rows),
            resident((D_MODEL, D_SHARED)),
            resident((D_MODEL, D_SHARED)),
            resident((D_SHARED, D_MODEL)),
            resident((D_MODEL, D_MODEL)),
            resident((PLE_DIM, D_MODEL)),
            pl.BlockSpec((1, D_MODEL), const),
            pl.BlockSpec((1, D_MODEL), const),
        ],
        out_specs=pl.BlockSpec((tm, D_MODEL), lambda i: (i, 0)),
        compiler_params=pltpu.CompilerParams(
            dimension_semantics=("arbitrary",), vmem_limit_bytes=VMEM_LIMIT),
        name="final",
    )(h1, h1b, routed, p_all, ws_gate, ws_up, ws_down, w_ple_gate, w_ple, g2, b2)


def _rope_tables(pos):
    half = ROT_DIM // 2
    inv_freq = ROPE_THETA ** (-jnp.arange(half, dtype=F32) * 2.0 / ROT_DIM)
    ang = pos.astype(F32)[:, None] * inv_freq[None, :]
    cos, sin = jnp.cos(ang), jnp.sin(ang)
    n = pos.shape[0]
    ones = jnp.ones((n, HEAD_DIM - ROT_DIM), F32)
    zeros = jnp.zeros((n, HEAD_DIM - ROT_DIM), F32)
    zh = jnp.zeros((n, half), F32)
    cos_t = jnp.concatenate([cos, cos, ones], -1)
    sin_lo = jnp.concatenate([-sin, zh, zeros], -1)
    sin_hi = jnp.concatenate([zh, sin, zeros], -1)
    per_head = jnp.stack([cos_t, sin_lo, sin_hi])
    return jnp.concatenate([per_head] * (LANES // HEAD_DIM), -1)


def _block_plan(counts, n_blocks):
    bm = EXPERT_ROWS
    i32 = jnp.int32
    padded = (counts + bm - 1) // bm * bm
    pad_end = jnp.cumsum(padded)
    pad_start = pad_end - padded
    n_used = pad_end[-1] // bm
    blk = jnp.arange(n_blocks)
    valid = blk < n_used
    block_e = jnp.minimum(jnp.sum(pad_end[None, :] <= (blk * bm)[:, None], axis=1), N_EXPERTS - 1)
    block_e = jnp.where(valid, block_e, block_e[jnp.maximum(n_used - 1, 0)])
    prev_e = jnp.concatenate([jnp.full((1,), -1, block_e.dtype), block_e[:-1]])
    first = valid & (block_e != prev_e)
    run = jnp.cumsum(first) - 1
    run_expert = jnp.full((n_blocks + 2,), -1, block_e.dtype).at[jnp.where(first, run, n_blocks + 1)].set(block_e)
    nxt = jnp.where(first, run_expert[jnp.minimum(run + 1, n_blocks)], -1)
    return (pad_start.astype(i32), (pad_start + counts).astype(i32), (padded - counts).astype(i32),
            n_used.astype(i32).reshape(1), block_e.astype(i32), first.astype(i32), (run % 2).astype(i32),
            nxt.astype(i32))


def kernel(x_prompt, x_sample, p_prompt, p_sample, cache_k, cache_v, state_conv, ln_in_g, ln_in_b, w_in,
           attn_sinks, conv_w, gn_attn, gn_conv, w_out, ln1_g, ln1_b, w_router, router_bias, w_gate, w_up,
           w_down, ws_gate, ws_up, ws_down, w_ple_gate, w_ple, ln2_g, ln2_b):
    n_batch, seq, d = x_prompt.shape
    n_dec, dec_seq, _ = x_sample.shape
    depth = w_in.shape[0]
    wb = cache_k.shape[2]
    assert depth == 1 and dec_seq == 1 and d == D_MODEL
    assert wb == ATTN_BLOCK and seq % ATTN_BLOCK == 0 and n_dec % SAMPLE_CHUNK == 0
    t_p = n_batch * seq
    t = t_p + n_dec
    assert t % ROW_TILE == 0 and t_p % SAMPLE_CHUNK == 0

    x_all = jnp.concatenate([x_prompt.reshape(t_p, d), x_sample.reshape(n_dec, d)], 0)
    p_all = jnp.concatenate([p_prompt[0].reshape(t_p, PLE_DIM), p_sample[0].reshape(n_dec, PLE_DIM)], 0).astype(BF16)
    row = lambda v: v.reshape(1, -1).astype(F32)

    w_in0 = w_in[0]
    qkv_w = ATTN_WIDTH + 2 * KV_WIDTH
    w_in_perm = jnp.concatenate([w_in0[:, qkv_w:], w_in0[:, :qkv_w]], axis=1).astype(BF16)

    proj = _ln_inproj(x_all, row(ln_in_g), row(ln_in_b), w_in_perm)

    rope_p = _rope_tables(jnp.arange(seq))
    rope_s = _rope_tables(PAST_LEN + jnp.arange(dec_seq))[:, 0, :]
    sinks = attn_sinks[0].astype(F32)
    cw, ga, gc = conv_w[0].astype(F32), row(gn_attn[0]), row(gn_conv[0])
    mixed_p, kwin_p, vwin_p, cstate_p = _prompt_mixer(proj, sinks, rope_p, cw, ga, gc, n_batch, seq)
    mixed_s, kwin_s, vwin_s, cstate_s = _sample_mixer(
        proj, t_p, n_dec, sinks, cache_k[0].reshape(n_dec, wb, KV_WIDTH), cache_v[0].reshape(n_dec, wb, KV_WIDTH),
        jnp.swapaxes(state_conv[0], 0, 1), rope_s, cw, ga, gc)
    cstate_s = jnp.swapaxes(cstate_s, 0, 1)
    mixed = jnp.concatenate([mixed_p, mixed_s], 0)

    h1, h1b = _outproj(
        mixed, x_all, row(ln_in_g), row(ln_in_b), w_out[0].astype(BF16), row(ln1_g[0]), row(ln1_b[0]))

    top_e, gate, rank, cnt = _route(h1b, w_router[0].T.astype(BF16), router_bias[0].astype(F32).reshape(N_EXPERTS, 1))
    n_blocks = t * TOP_K // EXPERT_ROWS + N_EXPERTS
    pad_start, fill_start, fill_len, n_used, block_e, first, slot, nxt = _block_plan(cnt[:, 0].astype(jnp.int32), n_blocks)
    dest = _tile_major(_dest_rows(pad_start, top_e, rank), LANES)
    xs = _dispatch_rows(fill_start, fill_len, n_used, dest, h1, n_blocks)
    yb = _experts(block_e, first, slot, nxt, n_used, xs, w_gate[0], w_up[0], w_down[0])
    routed = _combine(dest, gate.T, yb)

    final_args = (h1, h1b, routed, p_all, ws_gate[0].astype(BF16), ws_up[0].astype(BF16), ws_down[0].astype(BF16),
                  w_ple_gate[0].astype(BF16), w_ple[0].astype(BF16), row(ln2_g[0]), row(ln2_b[0]))
    y_p = _final(*final_args, row0=0, n_rows=t_p, tm=FINAL_TILE)
    y_s = _final(*final_args, row0=t_p, n_rows=n_dec, tm=n_dec)

    kv_shape = (1, -1, wb, N_KV_HEADS, HEAD_DIM)
    return (y_p.reshape(n_batch, seq, d), y_s.reshape(n_dec, dec_seq, d),
            kwin_p.reshape(kv_shape), vwin_p.reshape(kv_shape), cstate_p[None],
            kwin_s.reshape(kv_shape), vwin_s.reshape(kv_shape), cstate_s[None])
```

```python
import functools

import numpy as np
import jax
import jax.numpy as jnp
from jax import lax
from jax.experimental import pallas as pl
from jax.experimental.pallas import tpu as pltpu

D_MODEL = 2048
N_HEADS = 16
N_KV_HEADS = 4
HEAD_DIM = 64
GQA_GROUP = N_HEADS // N_KV_HEADS
ATTN_WIDTH = N_HEADS * HEAD_DIM
KV_WIDTH = N_KV_HEADS * HEAD_DIM
WINDOW = 128
ATTN_BLOCK = WINDOW
ROT_DIM = HEAD_DIM // 4
ROPE_THETA = 500000.0
CONV_WIDTH = D_MODEL - ATTN_WIDTH
CONV_K = 3
IN_COLS = ATTN_WIDTH + 2 * KV_WIDTH + 3 * CONV_WIDTH
N_EXPERTS = 256
TOP_K = 8
N_EXPERT_GROUPS = 8
TOPK_GROUPS = 4
D_EXPERT = 512
D_SHARED = 512
ROUTED_SCALE = 2.5
PLE_DIM = 256
LN_EPS = 1e-5
DEPTH = 1
PAST_LEN = 8192
DEEPNORM_ALPHA = (2 * DEPTH) ** 0.25

LANES = 128
SUBLANES = 8
COL_B, COL_C, COL_X, COL_Q, COL_KV = 0, CONV_WIDTH, 2 * CONV_WIDTH, 3 * CONV_WIDTH, 3 * CONV_WIDTH + ATTN_WIDTH

ROW_TILE = 640
INPROJ_COL_TILE = 1536
FINAL_TILE = 256
EXPERT_ROWS = 128
WEIGHT_SLOTS = 3
SAMPLE_CHUNK = 8
MIXER_BLOCKS_PER_STEP = 2
VMEM_LIMIT = 56 * 1024 * 1024

BF16 = jnp.bfloat16
F32 = jnp.float32


def _layer_norm(x, g, b):
    xc = x - jnp.mean(x, -1, keepdims=True)
    var = jnp.mean(xc * xc, -1, keepdims=True)
    return xc * lax.rsqrt(var + LN_EPS) * g + b


def _rms_norm(x, g):
    return x * lax.rsqrt(jnp.mean(x * x, -1, keepdims=True) + LN_EPS) * g


def _rope_lanes(x, cos_t, sin_lo, sin_hi):
    half = ROT_DIM // 2
    ax = x.ndim - 1
    return x * cos_t + pltpu.roll(x, LANES - half, ax) * sin_lo + pltpu.roll(x, half, ax) * sin_hi


def _rope_wide(x, cos_t, sin_lo, sin_hi):
    n = x.shape[-1] // LANES
    return jnp.concatenate(
        [_rope_lanes(x[..., c * LANES:(c + 1) * LANES], cos_t, sin_lo, sin_hi) for c in range(n)], axis=-1)


def _ln_inproj_kernel(x_ref, g_ref, b_ref, w_ref, o_ref, h_scr):
    @pl.when(pl.program_id(1) == 0)
    def _():
        h_scr[...] = _layer_norm(x_ref[...], g_ref[...], b_ref[...]).astype(BF16)

    o_ref[...] = jnp.dot(h_scr[...], w_ref[...], preferred_element_type=F32)


def _ln_inproj(x_all, g, b, w_bf16):
    t = x_all.shape[0]
    return pl.pallas_call(
        _ln_inproj_kernel,
        out_shape=jax.ShapeDtypeStruct((t, IN_COLS), F32),
        grid=(t // ROW_TILE, IN_COLS // INPROJ_COL_TILE),
        in_specs=[
            pl.BlockSpec((ROW_TILE, D_MODEL), lambda i, j: (i, 0)),
            pl.BlockSpec((1, D_MODEL), lambda i, j: (0, 0)),
            pl.BlockSpec((1, D_MODEL), lambda i, j: (0, 0)),
            pl.BlockSpec((D_MODEL, INPROJ_COL_TILE), lambda i, j: (0, j)),
        ],
        out_specs=pl.BlockSpec((ROW_TILE, INPROJ_COL_TILE), lambda i, j: (i, j)),
        scratch_shapes=[pltpu.VMEM((ROW_TILE, D_MODEL), BF16)],
        compiler_params=pltpu.CompilerParams(
            dimension_semantics=("arbitrary", "arbitrary"), vmem_limit_bytes=VMEM_LIMIT),
        name="ln_inproj",
    )(x_all, g, b, w_bf16)


def _prompt_mixer_kernel(sink_ref, b_ref, c_ref, x_ref, q_ref, kv_ref, kvp_ref, cp_ref, xp_ref,
                         rope_ref, ropep_ref, convw_ref, gna_ref, gnc_ref,
                         mixed_ref, kwin_ref, vwin_ref, cstate_ref, *, n_steps):
    i = pl.program_id(1)
    blk = ATTN_BLOCK
    rows = q_ref.shape[0]
    cos_t, sin_lo, sin_hi = rope_ref[0], rope_ref[1], rope_ref[2]
    qr = _rope_wide(q_ref[...], cos_t, sin_lo, sin_hi) * (HEAD_DIM ** -0.5)
    k_cur = _rope_wide(kv_ref[:, :KV_WIDTH], cos_t, sin_lo, sin_hi)
    k_prev = _rope_wide(kvp_ref[:, :KV_WIDTH], ropep_ref[0], ropep_ref[1], ropep_ref[2])
    v_cur = kv_ref[:, KV_WIDTH:]
    v_prev = kvp_ref[:, KV_WIDTH:]
    k_ext = jnp.concatenate([k_prev, k_cur], axis=0).astype(BF16)
    v_ext = jnp.concatenate([v_prev, v_cur], axis=0).astype(BF16)

    qi = lax.broadcasted_iota(jnp.int32, (blk, 2 * blk), 0)
    sj = lax.broadcasted_iota(jnp.int32, (blk, 2 * blk), 1)
    rel = qi + blk - sj
    local = (rel >= 0) & (rel <= WINDOW)
    for sb in range(rows // blk):
        mask = local if sb > 0 else local & ((sj >= blk) | (i > 0))
        mask = jnp.concatenate([mask] * GQA_GROUP, axis=0)
        q_sb = qr[sb * blk:(sb + 1) * blk]
        k_all = k_ext[sb * blk:(sb + 2) * blk]
        v_all = v_ext[sb * blk:(sb + 2) * blk]
        heads = []
        for g in range(N_KV_HEADS):
            qg = jnp.concatenate(
                [q_sb[:, (g * GQA_GROUP + hh) * HEAD_DIM:(g * GQA_GROUP + hh + 1) * HEAD_DIM]
                 for hh in range(GQA_GROUP)], axis=0).astype(BF16)
            kg = k_all[:, g * HEAD_DIM:(g + 1) * HEAD_DIM]
            vg = v_all[:, g * HEAD_DIM:(g + 1) * HEAD_DIM]
            s = lax.dot_general(qg, kg, (((1,), (1,)), ((), ())), preferred_element_type=F32)
            s = jnp.where(mask, s, -jnp.inf)
            sink = jnp.concatenate(
                [jnp.full((blk, 1), sink_ref[g * GQA_GROUP + hh], F32) for hh in range(GQA_GROUP)], axis=0)
            m = jnp.maximum(jnp.max(s, -1, keepdims=True), sink)
            e = jnp.exp(s - m)
            p = e / (jnp.sum(e, -1, keepdims=True) + jnp.exp(sink - m))
            og = jnp.dot(p.astype(BF16), vg, preferred_element_type=F32)
            heads += [og[hh * blk:(hh + 1) * blk] for hh in range(GQA_GROUP)]
        attn = jnp.concatenate(heads, axis=-1)
        mixed_ref[sb * blk:(sb + 1) * blk, :ATTN_WIDTH] = _rms_norm(attn, gna_ref[...]).astype(BF16)

    u = c_ref[...] * x_ref[...]
    n_tail = CONV_K - 1
    tail_rows = cp_ref.shape[0]
    row_id = lax.broadcasted_iota(jnp.int32, u.shape, 0)
    acc = u * convw_ref[CONV_K - 1:CONV_K, :]
    for d in range(1, CONV_K):
        ud = pltpu.roll(u, d, 0)
        for r in range(d):
            src = tail_rows - d + r
            up = jnp.where(i > 0, cp_ref[src:src + 1, :] * xp_ref[src:src + 1, :], 0.0)
            ud = jnp.where(row_id == r, up, ud)
        acc = acc + ud * convw_ref[CONV_K - 1 - d:CONV_K - d, :]
    mixed_ref[:, ATTN_WIDTH:] = _rms_norm(b_ref[...] * acc, gnc_ref[...]).astype(BF16)

    @pl.when(i == n_steps - 1)
    def _():
        kwin_ref[0] = k_cur[rows - blk:, :]
        vwin_ref[0] = v_cur[rows - blk:, :]
        cstate_ref[0] = u[rows - n_tail:, :]


def _prompt_mixer(proj, sinks, rope_tab, conv_w, gn_attn, gn_conv, n_batch, seq):
    blk = ATTN_BLOCK
    qb = MIXER_BLOCKS_PER_STEP
    rows = qb * blk
    nb = seq // rows
    tail = SUBLANES
    wide = CONV_WIDTH
    kvb = COL_KV // (2 * KV_WIDTH)

    def row(b, i, s):
        return b * nb + i

    def prev_blk(b, i, s):
        return jnp.maximum((b * nb + i) * qb - 1, b * nb * qb)

    def prev_tail(b, i, s):
        return jnp.maximum((b * nb + i) * (rows // tail) - 1, 0)

    grid_spec = pltpu.PrefetchScalarGridSpec(
        num_scalar_prefetch=1,
        grid=(n_batch, nb),
        in_specs=[
            pl.BlockSpec((rows, wide), lambda b, i, s: (row(b, i, s), COL_B // wide)),
            pl.BlockSpec((rows, wide), lambda b, i, s: (row(b, i, s), COL_C // wide)),
            pl.BlockSpec((rows, wide), lambda b, i, s: (row(b, i, s), COL_X // wide)),
            pl.BlockSpec((rows, ATTN_WIDTH), lambda b, i, s: (row(b, i, s), COL_Q // ATTN_WIDTH)),
            pl.BlockSpec((rows, 2 * KV_WIDTH), lambda b, i, s: (row(b, i, s), kvb)),
            pl.BlockSpec((blk, 2 * KV_WIDTH), lambda b, i, s: (prev_blk(b, i, s), kvb)),
            pl.BlockSpec((tail, wide), lambda b, i, s: (prev_tail(b, i, s), COL_C // wide)),
            pl.BlockSpec((tail, wide), lambda b, i, s: (prev_tail(b, i, s), COL_X // wide)),
            pl.BlockSpec((3, rows, LANES), lambda b, i, s: (0, i, 0)),
            pl.BlockSpec((3, blk, LANES), lambda b, i, s: (0, jnp.maximum(i * qb - 1, 0), 0)),
            pl.BlockSpec((CONV_K, CONV_WIDTH), lambda b, i, s: (0, 0)),
            pl.BlockSpec((1, ATTN_WIDTH), lambda b, i, s: (0, 0)),
            pl.BlockSpec((1, CONV_WIDTH), lambda b, i, s: (0, 0)),
        ],
        out_specs=[
            pl.BlockSpec((rows, D_MODEL), lambda b, i, s: (row(b, i, s), 0)),
            pl.BlockSpec((1, blk, KV_WIDTH), lambda b, i, s: (b, 0, 0)),
            pl.BlockSpec((1, blk, KV_WIDTH), lambda b, i, s: (b, 0, 0)),
            pl.BlockSpec((1, CONV_K - 1, CONV_WIDTH), lambda b, i, s: (b, 0, 0)),
        ],
    )
    return pl.pallas_call(
        functools.partial(_prompt_mixer_kernel, n_steps=nb),
        out_shape=[
            jax.ShapeDtypeStruct((n_batch * seq, D_MODEL), BF16),
            jax.ShapeDtypeStruct((n_batch, blk, KV_WIDTH), F32),
            jax.ShapeDtypeStruct((n_batch, blk, KV_WIDTH), F32),
            jax.ShapeDtypeStruct((n_batch, CONV_K - 1, CONV_WIDTH), F32),
        ],
        grid_spec=grid_spec,
        compiler_params=pltpu.CompilerParams(
            dimension_semantics=("arbitrary", "arbitrary"), vmem_limit_bytes=VMEM_LIMIT),
        name="prompt_mixer",
    )(sinks, proj, proj, proj, proj, proj, proj, proj, proj, rope_tab, rope_tab, conv_w, gn_attn, gn_conv)


def _expand_groups(t):
    ax = t.ndim - 1
    lane = lax.broadcasted_iota(jnp.int32, t.shape[:-1] + (LANES,), ax)
    chunks = []
    for c in range(KV_WIDTH // LANES):
        a = t[..., c * LANES:(c + 1) * LANES]
        r = pltpu.roll(a, HEAD_DIM, ax)
        lo = jnp.where(lane < HEAD_DIM, a, r)
        hi = jnp.where(lane < HEAD_DIM, r, a)
        chunks += [lo] * (GQA_GROUP // 2) + [hi] * (GQA_GROUP // 2)
    return jnp.concatenate(chunks, axis=-1)


def _sample_mixer_kernel(sink_ref, b_ref, c_ref, x_ref, q_ref, kv_ref, ck_ref, cv_ref, st_ref,
                         rope_ref, seg_ref, segt_ref, convw_ref, gna_ref, gnc_ref,
                         mixed_ref, kwin_ref, vwin_ref, cstate_ref):
    nb, wb = ck_ref.shape[0], ck_ref.shape[1]
    cos_t, sin_lo, sin_hi = rope_ref[0:1, :], rope_ref[1:2, :], rope_ref[2:3, :]
    qr = _rope_wide(q_ref[...], cos_t, sin_lo, sin_hi) * (HEAD_DIM ** -0.5)
    k_new = _rope_wide(kv_ref[:, :KV_WIDTH], cos_t, sin_lo, sin_hi)
    v_new = kv_ref[:, KV_WIDTH:]
    ck = ck_ref[...]
    cv = cv_ref[...]

    seg = seg_ref[...]
    prod = _expand_groups(ck) * qr[:, None, :]
    s_old = jnp.dot(prod.reshape(nb * wb, ATTN_WIDTH).astype(BF16), seg,
                    preferred_element_type=F32).reshape(nb, wb, N_HEADS)
    s_new = jnp.dot((_expand_groups(k_new) * qr).astype(BF16), seg, preferred_element_type=F32)
    sink = sink_ref[...]
    m = jnp.maximum(jnp.maximum(jnp.max(s_old, axis=1), s_new), sink)
    e_old = jnp.exp(s_old - m[:, None, :])
    e_new = jnp.exp(s_new - m)
    inv = 1.0 / (jnp.sum(e_old, axis=1) + e_new + jnp.exp(sink - m))
    p_old = (e_old * inv[:, None, :]).astype(BF16)
    p_new = (e_new * inv).astype(BF16)
    segt = segt_ref[...]
    pe_old = jnp.dot(p_old.reshape(nb * wb, N_HEADS), segt, preferred_element_type=F32).reshape(nb, wb, ATTN_WIDTH)
    pe_new = jnp.dot(p_new, segt, preferred_element_type=F32)
    attn = jnp.sum(pe_old * _expand_groups(cv), axis=1) + pe_new * _expand_groups(v_new)
    mixed_ref[:, :ATTN_WIDTH] = _rms_norm(attn, gna_ref[...]).astype(BF16)

    u = c_ref[...] * x_ref[...]
    acc = u * convw_ref[CONV_K - 1:CONV_K, :]
    for j in range(CONV_K - 1):
        acc = acc + st_ref[j] * convw_ref[j:j + 1, :]
    mixed_ref[:, ATTN_WIDTH:] = _rms_norm(b_ref[...] * acc, gnc_ref[...]).astype(BF16)

    kwin_ref[:, 0:wb - 1, :] = ck_ref[:, 1:wb, :]
    vwin_ref[:, 0:wb - 1, :] = cv_ref[:, 1:wb, :]
    for n in range(nb):
        kwin_ref[n, wb - 1:wb, :] = k_new[n:n + 1, :]
        vwin_ref[n, wb - 1:wb, :] = v_new[n:n + 1, :]
    for j in range(CONV_K - 2):
        cstate_ref[j] = st_ref[j + 1]
    cstate_ref[CONV_K - 2] = u


def _sample_mixer(proj, row0, n_dec, sinks, cache_k, cache_v, state_conv, rope_row, conv_w, gn_attn, gn_conv):
    nb = SAMPLE_CHUNK
    wb = cache_k.shape[1]
    r0 = row0 // nb
    wide = CONV_WIDTH
    head_of_lane = np.arange(ATTN_WIDTH) // HEAD_DIM
    seg = jnp.asarray(head_of_lane[:, None] == np.arange(N_HEADS)[None, :], BF16)
    kvb = COL_KV // (2 * KV_WIDTH)
    return pl.pallas_call(
        _sample_mixer_kernel,
        out_shape=[
            jax.ShapeDtypeStruct((n_dec, D_MODEL), BF16),
            jax.ShapeDtypeStruct((n_dec, wb, KV_WIDTH), F32),
            jax.ShapeDtypeStruct((n_dec, wb, KV_WIDTH), F32),
            jax.ShapeDtypeStruct((CONV_K - 1, n_dec, CONV_WIDTH), F32),
        ],
        grid=(n_dec // nb,),
        in_specs=[
            pl.BlockSpec((1, N_HEADS), lambda i: (0, 0)),
            pl.BlockSpec((nb, wide), lambda i: (r0 + i, COL_B // wide)),
            pl.BlockSpec((nb, wide), lambda i: (r0 + i, COL_C // wide)),
            pl.BlockSpec((nb, wide), lambda i: (r0 + i, COL_X // wide)),
            pl.BlockSpec((nb, ATTN_WIDTH), lambda i: (r0 + i, COL_Q // ATTN_WIDTH)),
            pl.BlockSpec((nb, 2 * KV_WIDTH), lambda i: (r0 + i, kvb)),
            pl.BlockSpec((nb, wb, KV_WIDTH), lambda i: (i, 0, 0)),
            pl.BlockSpec((nb, wb, KV_WIDTH), lambda i: (i, 0, 0)),
            pl.BlockSpec((CONV_K - 1, nb, CONV_WIDTH), lambda i: (0, i, 0)),
            pl.BlockSpec((3, LANES), lambda i: (0, 0)),
            pl.BlockSpec((ATTN_WIDTH, N_HEADS), lambda i: (0, 0)),
            pl.BlockSpec((N_HEADS, ATTN_WIDTH), lambda i: (0, 0)),
            pl.BlockSpec((CONV_K, CONV_WIDTH), lambda i: (0, 0)),
            pl.BlockSpec((1, ATTN_WIDTH), lambda i: (0, 0)),
            pl.BlockSpec((1, CONV_WIDTH), lambda i: (0, 0)),
        ],
        out_specs=[
            pl.BlockSpec((nb, D_MODEL), lambda i: (i, 0)),
            pl.BlockSpec((nb, wb, KV_WIDTH), lambda i: (i, 0, 0)),
            pl.BlockSpec((nb, wb, KV_WIDTH), lambda i: (i, 0, 0)),
            pl.BlockSpec((CONV_K - 1, nb, CONV_WIDTH), lambda i: (0, i, 0)),
        ],
        compiler_params=pltpu.CompilerParams(
            dimension_semantics=("arbitrary",), vmem_limit_bytes=VMEM_LIMIT),
        name="sample_mixer",
    )(sinks.reshape(1, N_HEADS), proj, proj, proj, proj, proj, cache_k, cache_v, state_conv,
      rope_row, seg, seg.T, conv_w, gn_attn, gn_conv)


def _outproj_kernel(mixed_ref, x_ref, g0_ref, b0_ref, wout_ref, g1_ref, b1_ref, h1_ref, h1b_ref):
    h = _layer_norm(x_ref[...], g0_ref[...], b0_ref[...])
    mixed = jnp.dot(mixed_ref[...], wout_ref[...], preferred_element_type=F32)
    h1 = _layer_norm(DEEPNORM_ALPHA * h + mixed, g1_ref[...], b1_ref[...])
    h1_ref[...] = h1
    h1b_ref[...] = h1.astype(BF16)


def _outproj(mixed, x_all, g0, b0, w_out, g1, b1):
    t = x_all.shape[0]
    tm = ROW_TILE // 2
    const = lambda i: (0, 0)
    resident = functools.partial(pl.BlockSpec, index_map=const, pipeline_mode=pl.Buffered(1))
    return pl.pallas_call(
        _outproj_kernel,
        out_shape=[
            jax.ShapeDtypeStruct((t, D_MODEL), F32),
            jax.ShapeDtypeStruct((t, D_MODEL), BF16),
        ],
        grid=(t // tm,),
        in_specs=[
            pl.BlockSpec((tm, D_MODEL), lambda i: (i, 0)),
            pl.BlockSpec((tm, D_MODEL), lambda i: (i, 0)),
            pl.BlockSpec((1, D_MODEL), const),
            pl.BlockSpec((1, D_MODEL), const),
            resident((D_MODEL, D_MODEL)),
            pl.BlockSpec((1, D_MODEL), const),
            pl.BlockSpec((1, D_MODEL), const),
        ],
        out_specs=[
            pl.BlockSpec((tm, D_MODEL), lambda i: (i, 0)),
            pl.BlockSpec((tm, D_MODEL), lambda i: (i, 0)),
        ],
        compiler_params=pltpu.CompilerParams(
            dimension_semantics=("arbitrary",), vmem_limit_bytes=VMEM_LIMIT),
        name="outproj",
    )(mixed, x_all, g0, b0, w_out, g1, b1)


def _route_kernel(h_ref, wrt_ref, bias_ref, tri_ref, ones_ref, e_ref, gate_ref, rank_ref, cnt_ref):
    tm = h_ref.shape[0]
    neg = -jnp.inf
    big = float(N_EXPERTS)

    @pl.when(pl.program_id(0) == 0)
    def _():
        cnt_ref[...] = jnp.zeros_like(cnt_ref)

    logits = lax.dot_general(wrt_ref[...], h_ref[...], (((1,), (1,)), ((), ())), preferred_element_type=F32)
    scores = jax.nn.sigmoid(logits)
    biased = scores + bias_ref[...]

    per_group = N_EXPERTS // N_EXPERT_GROUPS
    rid = lax.broadcasted_iota(jnp.int32, (per_group, tm), 0).astype(F32)
    gscore = []
    for g in range(N_EXPERT_GROUPS):
        xg = biased[g * per_group:(g + 1) * per_group]
        m1 = jnp.max(xg, 0, keepdims=True)
        i1 = jnp.min(jnp.where(xg == m1, rid, big), 0, keepdims=True)
        m2 = jnp.max(jnp.where(rid == i1, neg, xg), 0, keepdims=True)
        gscore.append(m1 + m2)
    masked = []
    for g in range(N_EXPERT_GROUPS):
        beaten = jnp.zeros((1, tm), F32)
        for o in range(N_EXPERT_GROUPS):
            if o != g:
                wins = (gscore[o] > gscore[g]) | ((gscore[o] == gscore[g]) & (o < g))
                beaten = beaten + jnp.where(wins, 1.0, 0.0)
        masked.append(jnp.where(beaten < TOPK_GROUPS, biased[g * per_group:(g + 1) * per_group], neg))
    x = jnp.concatenate(masked, axis=0)

    eid = lax.broadcasted_iota(jnp.int32, (N_EXPERTS, tm), 0).astype(F32)
    sel = jnp.zeros((N_EXPERTS, tm), F32)
    picks, gates = [], []
    for _ in range(TOP_K):
        m = jnp.max(x, 0, keepdims=True)
        ik = jnp.min(jnp.where(x == m, eid, big), 0, keepdims=True)
        hit = eid == ik
        gates.append(jnp.sum(jnp.where(hit, scores, 0.0), 0, keepdims=True))
        picks.append(ik)
        x = jnp.where(hit, neg, x)
        sel = jnp.where(hit, 1.0, sel)

    selb = sel.astype(BF16)
    pos = cnt_ref[...] + jnp.dot(selb, tri_ref[...], preferred_element_type=F32)
    cnt_ref[...] = cnt_ref[...] + jnp.dot(selb, ones_ref[...], preferred_element_type=F32)

    gsum = gates[0]
    for k in range(1, TOP_K):
        gsum = gsum + gates[k]
    for k in range(TOP_K):
        e_ref[k:k + 1, :] = picks[k].astype(jnp.int32)
        gate_ref[k:k + 1, :] = gates[k] / gsum * ROUTED_SCALE
        rank_ref[k:k + 1, :] = jnp.sum(jnp.where(eid == picks[k], pos, 0.0), 0, keepdims=True).astype(jnp.int32)


def _route(h1b, w_router_t, bias_col):
    t = h1b.shape[0]
    tm = LANES
    tri = jnp.asarray(np.arange(tm)[:, None] < np.arange(tm)[None, :], BF16)
    ones = jnp.ones((tm, tm), BF16)
    const = lambda i: (0, 0)
    return pl.pallas_call(
        _route_kernel,
        out_shape=[
            jax.ShapeDtypeStruct((TOP_K, t), jnp.int32),
            jax.ShapeDtypeStruct((TOP_K, t), F32),
            jax.ShapeDtypeStruct((TOP_K, t), jnp.int32),
            jax.ShapeDtypeStruct((N_EXPERTS, tm), F32),
        ],
        grid=(t // tm,),
        in_specs=[
            pl.BlockSpec((tm, D_MODEL), lambda i: (i, 0)),
            pl.BlockSpec((N_EXPERTS, D_MODEL), const),
            pl.BlockSpec((N_EXPERTS, 1), const),
            pl.BlockSpec((tm, tm), const),
            pl.BlockSpec((tm, tm), const),
        ],
        out_specs=[
            pl.BlockSpec((TOP_K, tm), lambda i: (0, i)),
            pl.BlockSpec((TOP_K, tm), lambda i: (0, i)),
            pl.BlockSpec((TOP_K, tm), lambda i: (0, i)),
            pl.BlockSpec((N_EXPERTS, tm), const),
        ],
        compiler_params=pltpu.CompilerParams(
            dimension_semantics=("arbitrary",), vmem_limit_bytes=VMEM_LIMIT),
        name="route",
    )(h1b, w_router_t, bias_col, tri, ones)


def _pow2_chunks(n_max):
    return [1 << b for b in reversed(range(int(n_max).bit_length()))]


def _dest_kernel(pstart_ref, e_ref, rank_ref, dest_ref):
    e = e_ref[...]

    def body(x, acc):
        return jnp.where(e == x, pstart_ref[x], acc)
    dest_ref[...] = lax.fori_loop(0, N_EXPERTS, body, jnp.zeros_like(e)) + rank_ref[...]


def _dest_rows(pad_start, top_e, rank):
    whole = pl.BlockSpec(top_e.shape, lambda i, ps: (0, 0))
    return pl.pallas_call(
        _dest_kernel,
        out_shape=jax.ShapeDtypeStruct(top_e.shape, jnp.int32),
        grid_spec=pltpu.PrefetchScalarGridSpec(num_scalar_prefetch=1, grid=(1,), in_specs=[whole, whole],
                                               out_specs=whole),
        name="dest_rows",
    )(pad_start, top_e, rank)


def _dispatch_kernel(fill0_ref, filln_ref, nused_ref, dest_ref, h1_ref,
                     xs_hbm, zeros_vmem, row_sem, fill_sem, *, n_blocks):
    i = pl.program_id(0)
    tm = h1_ref.shape[0]
    bm = EXPERT_ROWS

    def fill_copies(e):
        n = filln_ref[e]
        start = fill0_ref[e]
        head = jnp.minimum((SUBLANES - start % SUBLANES) % SUBLANES, n)
        out = []
        for r in range(SUBLANES - 1):
            out.append((r < head, pltpu.make_async_copy(
                zeros_vmem.at[pl.ds(0, 1), :], xs_hbm.at[pl.ds(start + r, 1), :], fill_sem)))
        body0 = start + head
        m = n - head
        for c in _pow2_chunks(bm - 1):
            if c < SUBLANES:
                continue
            off = pl.multiple_of(body0 + (m // (2 * c)) * (2 * c), SUBLANES)
            out.append(((m // c) % 2 == 1, pltpu.make_async_copy(
                zeros_vmem.at[pl.ds(0, c), :], xs_hbm.at[pl.ds(off, c), :], fill_sem)))
        return out

    def tail_copy(j):
        return pltpu.make_async_copy(zeros_vmem, xs_hbm.at[pl.ds(pl.multiple_of(j * bm, bm), bm), :], fill_sem)

    @pl.when(i == 0)
    def _():
        zeros_vmem[...] = jnp.zeros_like(zeros_vmem)

        def start_fill(e, c):
            for pred, cp in fill_copies(e):
                @pl.when(pred)
                def _():
                    cp.start(priority=1)
            return c
        lax.fori_loop(0, N_EXPERTS, start_fill, 0)

        def start_tail(j, c):
            tail_copy(j).start(priority=1)
            return c
        lax.fori_loop(nused_ref[0], n_blocks, start_tail, 0)

    for t in range(tm):
        for k in range(TOP_K):
            pltpu.make_async_copy(h1_ref.at[pl.ds(t, 1), :], xs_hbm.at[pl.ds(dest_ref[k * tm + t], 1), :],
                                  row_sem).start(priority=k % 2)
    for _ in range(TOP_K):
        pltpu.make_async_copy(h1_ref, xs_hbm.at[pl.ds(0, tm), :], row_sem).wait()

    @pl.when(i == pl.num_programs(0) - 1)
    def _():
        def wait_fill(e, c):
            for pred, cp in fill_copies(e):
                @pl.when(pred)
                def _():
                    cp.wait()
            return c
        lax.fori_loop(0, N_EXPERTS, wait_fill, 0)

        def wait_tail(j, c):
            tail_copy(j).wait()
            return c
        lax.fori_loop(nused_ref[0], n_blocks, wait_tail, 0)


def _tile_major(a, tm):
    k, t = a.shape
    return a.reshape(k, t // tm, tm).swapaxes(0, 1).reshape(-1)


def _dispatch_rows(fill_start, fill_len, n_used, dest, h1, n_blocks):
    t, width = h1.shape
    tm = LANES
    grid_spec = pltpu.PrefetchScalarGridSpec(
        num_scalar_prefetch=3,
        grid=(t // tm,),
        in_specs=[pl.BlockSpec((TOP_K * tm,), lambda i, *_: (i,), memory_space=pltpu.SMEM),
                  pl.BlockSpec((tm, width), lambda i, *_: (i, 0))],
        out_specs=pl.BlockSpec(memory_space=pl.ANY),
        scratch_shapes=[pltpu.VMEM((EXPERT_ROWS, width), h1.dtype),
                        pltpu.SemaphoreType.DMA, pltpu.SemaphoreType.DMA],
    )
    return pl.pallas_call(
        functools.partial(_dispatch_kernel, n_blocks=n_blocks),
        out_shape=jax.ShapeDtypeStruct((n_blocks * EXPERT_ROWS, width), h1.dtype),
        grid_spec=grid_spec,
        compiler_params=pltpu.CompilerParams(dimension_semantics=("arbitrary",)),
        name="dispatch",
    )(fill_start, fill_len, n_used, dest, h1)


def _experts_kernel(be_ref, first_ref, slot_ref, ahead_ref, head_ref, nused_ref, x_ref, wg_hbm, wu_hbm, wd_hbm,
                    y_ref, wg_buf, wu_buf, wd_buf, wg_bf, wu_bf, wd_bf, sems):
    i = pl.program_id(0)
    n_slots = wg_buf.shape[0]

    def weight_copies(e, slot):
        return (pltpu.make_async_copy(wg_hbm.at[e], wg_buf.at[slot], sems.at[slot, 0]),
                pltpu.make_async_copy(wu_hbm.at[e], wu_buf.at[slot], sems.at[slot, 1]),
                pltpu.make_async_copy(wd_hbm.at[e], wd_buf.at[slot], sems.at[slot, 2]))

    @pl.when(i == 0)
    def _():
        for r in range(n_slots - 1):
            @pl.when(head_ref[r] >= 0)
            def _():
                for cp in weight_copies(head_ref[r], r):
                    cp.start(priority=1)

    @pl.when(first_ref[i] == 1)
    def _():
        slot = slot_ref[i]
        for cp in weight_copies(be_ref[i], slot):
            cp.wait()

        @pl.when(ahead_ref[i] >= 0)
        def _():
            for cp in weight_copies(ahead_ref[i], (slot + n_slots - 1) % n_slots):
                cp.start(priority=1)

        wg_bf[...] = wg_buf[slot].astype(BF16)
        wu_bf[...] = wu_buf[slot].astype(BF16)
        wd_bf[...] = wd_buf[slot].astype(BF16)

    @pl.when(i < nused_ref[0])
    def _():
        x = x_ref[...].astype(BF16)
        g = jnp.dot(x, wg_bf[...], preferred_element_type=F32)
        u = jnp.dot(x, wu_bf[...], preferred_element_type=F32)
        a = (g * jax.nn.sigmoid(g) * u).astype(BF16)
        y_ref[...] = jnp.dot(a, wd_bf[...], preferred_element_type=F32)

    @pl.when(i >= nused_ref[0])
    def _():
        y_ref[...] = jnp.zeros_like(y_ref)


def _experts(block_e, first, slot, ahead, head, n_used, xs, w_gate, w_up, w_down):
    p = xs.shape[0]
    bm = EXPERT_ROWS
    n_blocks = p // bm
    ns = WEIGHT_SLOTS

    def xrow(i, be, fi, sl, ah, hd, nu):
        return (jnp.minimum(i, nu[0] - 1), 0)

    grid_spec = pltpu.PrefetchScalarGridSpec(
        num_scalar_prefetch=6,
        grid=(n_blocks,),
        in_specs=[
            pl.BlockSpec((bm,) + xs.shape[1:], xrow),
            pl.BlockSpec(memory_space=pl.ANY),
            pl.BlockSpec(memory_space=pl.ANY),
            pl.BlockSpec(memory_space=pl.ANY),
        ],
        out_specs=pl.BlockSpec((bm, D_MODEL), lambda i, *_: (i, 0)),
        scratch_shapes=[
            pltpu.VMEM((ns, D_MODEL, D_EXPERT), F32),
            pltpu.VMEM((ns, D_MODEL, D_EXPERT), F32),
            pltpu.VMEM((ns, D_EXPERT, D_MODEL), F32),
            pltpu.VMEM((D_MODEL, D_EXPERT), BF16),
            pltpu.VMEM((D_MODEL, D_EXPERT), BF16),
            pltpu.VMEM((D_EXPERT, D_MODEL), BF16),
            pltpu.SemaphoreType.DMA((ns, 3)),
        ],
    )
    return pl.pallas_call(
        _experts_kernel,
        out_shape=jax.ShapeDtypeStruct((p, D_MODEL), F32),
        grid_spec=grid_spec,
        compiler_params=pltpu.CompilerParams(
            dimension_semantics=("arbitrary",), vmem_limit_bytes=VMEM_LIMIT),
        name="experts",
    )(block_e, first, slot, ahead, head, n_used, xs, w_gate, w_up, w_down)


def _combine_kernel(dest_ref, gate_ref, yb_hbm, o_ref, buf, sems):
    i = pl.program_id(0)
    n_tiles = pl.num_programs(0) - 1
    tm = o_ref.shape[0]
    n_rows = TOP_K * tm

    for s in range(2):
        @pl.when((i < n_tiles) & (i % 2 == s))
        def _():
            for idx in range(n_rows):
                pltpu.make_async_copy(yb_hbm.at[pl.ds(dest_ref[idx], 1), :], buf.at[s, pl.ds(idx, 1), :],
                                      sems.at[s]).start(priority=idx % 2)

    @pl.when(i >= 1)
    def _():
        slot = (i - 1) % 2
        pltpu.make_async_copy(yb_hbm.at[pl.ds(0, n_rows), :], buf.at[slot], sems.at[slot]).wait()
        acc = gate_ref[:, 0:1] * buf[slot, 0:tm, :]
        for k in range(1, TOP_K):
            acc = acc + gate_ref[:, k:k + 1] * buf[slot, k * tm:(k + 1) * tm, :]
        o_ref[...] = acc


def _combine(dest, gate_tk, yb):
    t = gate_tk.shape[0]
    tm = LANES
    n_tiles = t // tm
    return pl.pallas_call(
        _combine_kernel,
        out_shape=jax.ShapeDtypeStruct((t, D_MODEL), F32),
        grid=(n_tiles + 1,),
        in_specs=[
            pl.BlockSpec((TOP_K * tm,), lambda i: (jnp.minimum(i, n_tiles - 1),), memory_space=pltpu.SMEM),
            pl.BlockSpec((tm, TOP_K), lambda i: (jnp.maximum(i - 1, 0), 0)),
            pl.BlockSpec(memory_space=pl.ANY),
        ],
        out_specs=pl.BlockSpec((tm, D_MODEL), lambda i: (jnp.maximum(i - 1, 0), 0)),
        scratch_shapes=[pltpu.VMEM((2, TOP_K * tm, D_MODEL), F32), pltpu.SemaphoreType.DMA((2,))],
        compiler_params=pltpu.CompilerParams(
            dimension_semantics=("arbitrary",), vmem_limit_bytes=VMEM_LIMIT),
        name="combine",
    )(dest, gate_tk, yb)


def _final_kernel(h1_ref, h1b_ref, routed_ref, p_ref, wsg_ref, wsu_ref, wsd_ref, wpg_ref, wp_ref,
                  g2_ref, b2_ref, y_ref):
    hb = h1b_ref[...]
    g = jnp.dot(hb, wsg_ref[...], preferred_element_type=F32)
    u = jnp.dot(hb, wsu_ref[...], preferred_element_type=F32)
    shared = jnp.dot((g * jax.nn.sigmoid(g) * u).astype(BF16), wsd_ref[...], preferred_element_type=F32)
    gate = jax.nn.sigmoid(jnp.dot(hb, wpg_ref[...], preferred_element_type=F32))
    ple = gate * jnp.dot(p_ref[...], wp_ref[...], preferred_element_type=F32)
    r = DEEPNORM_ALPHA * h1_ref[...] + (routed_ref[...] + shared) + ple
    y_ref[...] = _layer_norm(r, g2_ref[...], b2_ref[...])


def _final(h1, h1b, routed, p_all, ws_gate, ws_up, ws_down, w_ple_gate, w_ple, g2, b2, *, row0, n_rows, tm):
    assert row0 % tm == 0 and n_rows % tm == 0
    blk0 = row0 // tm
    const = lambda i: (0, 0)
    rows = lambda i: (blk0 + i, 0)
    resident = functools.partial(pl.BlockSpec, index_map=const, pipeline_mode=pl.Buffered(1))
    return pl.pallas_call(
        _final_kernel,
        out_shape=jax.ShapeDtypeStruct((n_rows, D_MODEL), F32),
        grid=(n_rows // tm,),
        in_specs=[
            pl.BlockSpec((tm, D_MODEL), rows),
            pl.BlockSpec((tm, D_MODEL), rows),
            pl.BlockSpec((tm, D_MODEL), rows),
            pl.BlockSpec((tm, PLE_DIM), rows),
            resident((D_MODEL, D_SHARED)),
            resident((D_MODEL, D_SHARED)),
            resident((D_SHARED, D_MODEL)),
            resident((D_MODEL, D_MODEL)),
            resident((PLE_DIM, D_MODEL)),
            pl.BlockSpec((1, D_MODEL), const),
            pl.BlockSpec((1, D_MODEL), const),
        ],
        out_specs=pl.BlockSpec((tm, D_MODEL), lambda i: (i, 0)),
        compiler_params=pltpu.CompilerParams(
            dimension_semantics=("arbitrary",), vmem_limit_bytes=VMEM_LIMIT),
        name="final",
    )(h1, h1b, routed, p_all, ws_gate, ws_up, ws_down, w_ple_gate, w_ple, g2, b2)


def _rope_tables(pos):
    half = ROT_DIM // 2
    inv_freq = ROPE_THETA ** (-jnp.arange(half, dtype=F32) * 2.0 / ROT_DIM)
    ang = pos.astype(F32)[:, None] * inv_freq[None, :]
    cos, sin = jnp.cos(ang), jnp.sin(ang)
    n = pos.shape[0]
    ones = jnp.ones((n, HEAD_DIM - ROT_DIM), F32)
    zeros = jnp.zeros((n, HEAD_DIM - ROT_DIM), F32)
    zh = jnp.zeros((n, half), F32)
    cos_t = jnp.concatenate([cos, cos, ones], -1)
    sin_lo = jnp.concatenate([-sin, zh, zeros], -1)
    sin_hi = jnp.concatenate([zh, sin, zeros], -1)
    per_head = jnp.stack([cos_t, sin_lo, sin_hi])
    return jnp.concatenate([per_head] * (LANES // HEAD_DIM), -1)


def _block_plan(counts, n_blocks):
    bm = EXPERT_ROWS
    i32 = jnp.int32
    padded = (counts + bm - 1) // bm * bm
    pad_end = jnp.cumsum(padded)
    pad_start = pad_end - padded
    n_used = pad_end[-1] // bm
    blk = jnp.arange(n_blocks)
    valid = blk < n_used
    block_e = jnp.minimum(jnp.sum(pad_end[None, :] <= (blk * bm)[:, None], axis=1), N_EXPERTS - 1)
    block_e = jnp.where(valid, block_e, block_e[jnp.maximum(n_used - 1, 0)])
    prev_e = jnp.concatenate([jnp.full((1,), -1, block_e.dtype), block_e[:-1]])
    first = valid & (block_e != prev_e)
    run = jnp.cumsum(first) - 1
    ns = WEIGHT_SLOTS
    used = counts > 0
    n_run_slots = N_EXPERTS + ns
    run_expert = jnp.full((n_run_slots + 1,), -1, i32).at[
        jnp.where(used, jnp.cumsum(used) - 1, n_run_slots)].set(jnp.arange(N_EXPERTS, dtype=i32))
    run_expert = run_expert[:n_run_slots]
    ahead = jnp.where(first, run_expert[run + ns - 1], -1)
    head = run_expert[:ns - 1]
    return (pad_start.astype(i32), (pad_start + counts).astype(i32), (padded - counts).astype(i32),
            n_used.astype(i32).reshape(1), block_e.astype(i32), first.astype(i32), (run % ns).astype(i32),
            ahead.astype(i32), head.astype(i32))


def kernel(x_prompt, x_sample, p_prompt, p_sample, cache_k, cache_v, state_conv, ln_in_g, ln_in_b, w_in,
           attn_sinks, conv_w, gn_attn, gn_conv, w_out, ln1_g, ln1_b, w_router, router_bias, w_gate, w_up,
           w_down, ws_gate, ws_up, ws_down, w_ple_gate, w_ple, ln2_g, ln2_b):
    n_batch, seq, d = x_prompt.shape
    n_dec, dec_seq, _ = x_sample.shape
    depth = w_in.shape[0]
    wb = cache_k.shape[2]
    assert depth == 1 and dec_seq == 1 and d == D_MODEL
    assert wb == ATTN_BLOCK and seq % (ATTN_BLOCK * MIXER_BLOCKS_PER_STEP) == 0 and n_dec % SAMPLE_CHUNK == 0
    t_p = n_batch * seq
    t = t_p + n_dec
    assert t % ROW_TILE == 0 and t_p % SAMPLE_CHUNK == 0

    x_all = jnp.concatenate([x_prompt.reshape(t_p, d), x_sample.reshape(n_dec, d)], 0)
    p_all = jnp.concatenate([p_prompt[0].reshape(t_p, PLE_DIM), p_sample[0].reshape(n_dec, PLE_DIM)], 0).astype(BF16)
    row = lambda v: v.reshape(1, -1).astype(F32)

    w_in0 = w_in[0]
    qkv_w = ATTN_WIDTH + 2 * KV_WIDTH
    w_in_perm = jnp.concatenate([w_in0[:, qkv_w:], w_in0[:, :qkv_w]], axis=1).astype(BF16)

    proj = _ln_inproj(x_all, row(ln_in_g), row(ln_in_b), w_in_perm)

    rope_p = _rope_tables(jnp.arange(seq))
    rope_s = _rope_tables(PAST_LEN + jnp.arange(dec_seq))[:, 0, :]
    sinks = attn_sinks[0].astype(F32)
    cw, ga, gc = conv_w[0].astype(F32), row(gn_attn[0]), row(gn_conv[0])
    mixed_p, kwin_p, vwin_p, cstate_p = _prompt_mixer(proj, sinks, rope_p, cw, ga, gc, n_batch, seq)
    mixed_s, kwin_s, vwin_s, cstate_s = _sample_mixer(
        proj, t_p, n_dec, sinks, cache_k[0].reshape(n_dec, wb, KV_WIDTH), cache_v[0].reshape(n_dec, wb, KV_WIDTH),
        jnp.swapaxes(state_conv[0], 0, 1), rope_s, cw, ga, gc)
    cstate_s = jnp.swapaxes(cstate_s, 0, 1)
    mixed = jnp.concatenate([mixed_p, mixed_s], 0)

    h1, h1b = _outproj(
        mixed, x_all, row(ln_in_g), row(ln_in_b), w_out[0].astype(BF16), row(ln1_g[0]), row(ln1_b[0]))

    top_e, gate, rank, cnt = _route(h1b, w_router[0].T.astype(BF16), router_bias[0].astype(F32).reshape(N_EXPERTS, 1))
    n_blocks = t * TOP_K // EXPERT_ROWS + N_EXPERTS
    pad_start, fill_start, fill_len, n_used, block_e, first, slot, ahead, head = _block_plan(
        cnt[:, 0].astype(jnp.int32), n_blocks)
    dest = _tile_major(_dest_rows(pad_start, top_e, rank), LANES)
    xs = _dispatch_rows(fill_start, fill_len, n_used, dest, h1, n_blocks)
    yb = _experts(block_e, first, slot, ahead, head, n_used, xs, w_gate[0], w_up[0], w_down[0])
    routed = _combine(dest, gate.T, yb)

    final_args = (h1, h1b, routed, p_all, ws_gate[0].astype(BF16), ws_up[0].astype(BF16), ws_down[0].astype(BF16),
                  w_ple_gate[0].astype(BF16), w_ple[0].astype(BF16), row(ln2_g[0]), row(ln2_b[0]))
    y_p = _final(*final_args, row0=0, n_rows=t_p, tm=FINAL_TILE)
    y_s = _final(*final_args, row0=t_p, n_rows=n_dec, tm=n_dec)

    kv_shape = (1, -1, wb, N_KV_HEADS, HEAD_DIM)
    return (y_p.reshape(n_batch, seq, d), y_s.reshape(n_dec, dec_seq, d),
            kwin_p.reshape(kv_shape), vwin_p.reshape(kv_shape), cstate_p[None],
            kwin_s.reshape(kv_shape), vwin_s.reshape(kv_shape), cstate_s[None])
```

```python
import functools

import numpy as np
import jax
import jax.numpy as jnp
from jax import lax
from jax.experimental import pallas as pl
from jax.experimental.pallas import tpu as pltpu

D_MODEL = 2048
N_HEADS = 16
N_KV_HEADS = 4
HEAD_DIM = 64
GQA_GROUP = N_HEADS // N_KV_HEADS
ATTN_WIDTH = N_HEADS * HEAD_DIM
KV_WIDTH = N_KV_HEADS * HEAD_DIM
WINDOW = 128
ATTN_BLOCK = WINDOW
ROT_DIM = HEAD_DIM // 4
ROPE_THETA = 500000.0
CONV_WIDTH = D_MODEL - ATTN_WIDTH
CONV_K = 3
IN_COLS = ATTN_WIDTH + 2 * KV_WIDTH + 3 * CONV_WIDTH
N_EXPERTS = 256
TOP_K = 8
N_EXPERT_GROUPS = 8
TOPK_GROUPS = 4
D_EXPERT = 512
D_SHARED = 512
ROUTED_SCALE = 2.5
PLE_DIM = 256
LN_EPS = 1e-5
DEPTH = 1
PAST_LEN = 8192
DEEPNORM_ALPHA = (2 * DEPTH) ** 0.25

LANES = 128
SUBLANES = 8
COL_B, COL_C, COL_X, COL_Q, COL_KV = 0, CONV_WIDTH, 2 * CONV_WIDTH, 3 * CONV_WIDTH, 3 * CONV_WIDTH + ATTN_WIDTH

ROW_TILE = 640
INPROJ_COL_TILE = 1536
FINAL_TILE = 256
EXPERT_ROWS = 128
WEIGHT_SLOTS = 3
SAMPLE_CHUNK = 8
MIXER_BLOCKS_PER_STEP = 2
VMEM_LIMIT = 56 * 1024 * 1024

BF16 = jnp.bfloat16
F32 = jnp.float32


def _layer_norm(x, g, b):
    xc = x - jnp.mean(x, -1, keepdims=True)
    var = jnp.mean(xc * xc, -1, keepdims=True)
    return xc * lax.rsqrt(var + LN_EPS) * g + b


def _rms_norm(x, g):
    return x * lax.rsqrt(jnp.mean(x * x, -1, keepdims=True) + LN_EPS) * g


def _rope_lanes(x, cos_t, sin_lo, sin_hi):
    half = ROT_DIM // 2
    ax = x.ndim - 1
    return x * cos_t + pltpu.roll(x, LANES - half, ax) * sin_lo + pltpu.roll(x, half, ax) * sin_hi


def _rope_wide(x, cos_t, sin_lo, sin_hi):
    n = x.shape[-1] // LANES
    return jnp.concatenate(
        [_rope_lanes(x[..., c * LANES:(c + 1) * LANES], cos_t, sin_lo, sin_hi) for c in range(n)], axis=-1)


def _ln_inproj_kernel(x_ref, g_ref, b_ref, w_ref, o_ref, h_scr):
    @pl.when(pl.program_id(1) == 0)
    def _():
        h_scr[...] = _layer_norm(x_ref[...], g_ref[...], b_ref[...]).astype(BF16)

    o_ref[...] = jnp.dot(h_scr[...], w_ref[...], preferred_element_type=F32)


def _ln_inproj(x_all, g, b, w_bf16):
    t = x_all.shape[0]
    return pl.pallas_call(
        _ln_inproj_kernel,
        out_shape=jax.ShapeDtypeStruct((t, IN_COLS), F32),
        grid=(t // ROW_TILE, IN_COLS // INPROJ_COL_TILE),
        in_specs=[
            pl.BlockSpec((ROW_TILE, D_MODEL), lambda i, j: (i, 0)),
            pl.BlockSpec((1, D_MODEL), lambda i, j: (0, 0)),
            pl.BlockSpec((1, D_MODEL), lambda i, j: (0, 0)),
            pl.BlockSpec((D_MODEL, INPROJ_COL_TILE), lambda i, j: (0, j)),
        ],
        out_specs=pl.BlockSpec((ROW_TILE, INPROJ_COL_TILE), lambda i, j: (i, j)),
        scratch_shapes=[pltpu.VMEM((ROW_TILE, D_MODEL), BF16)],
        compiler_params=pltpu.CompilerParams(
            dimension_semantics=("arbitrary", "arbitrary"), vmem_limit_bytes=VMEM_LIMIT),
        name="ln_inproj",
    )(x_all, g, b, w_bf16)


def _prompt_mixer_kernel(sink_ref, b_ref, c_ref, x_ref, q_ref, kv_ref, kvp_ref, cp_ref, xp_ref,
                         rope_ref, ropep_ref, convw_ref, gna_ref, gnc_ref,
                         mixed_ref, kwin_ref, vwin_ref, cstate_ref, *, n_steps):
    i = pl.program_id(1)
    blk = ATTN_BLOCK
    rows = q_ref.shape[0]
    cos_t, sin_lo, sin_hi = rope_ref[0], rope_ref[1], rope_ref[2]
    qr = _rope_wide(q_ref[...], cos_t, sin_lo, sin_hi) * (HEAD_DIM ** -0.5)
    k_cur = _rope_wide(kv_ref[:, :KV_WIDTH], cos_t, sin_lo, sin_hi)
    k_prev = _rope_wide(kvp_ref[:, :KV_WIDTH], ropep_ref[0], ropep_ref[1], ropep_ref[2])
    v_cur = kv_ref[:, KV_WIDTH:]
    v_prev = kvp_ref[:, KV_WIDTH:]
    k_ext = jnp.concatenate([k_prev, k_cur], axis=0).astype(BF16)
    v_ext = jnp.concatenate([v_prev, v_cur], axis=0).astype(BF16)

    qi = lax.broadcasted_iota(jnp.int32, (blk, 2 * blk), 0)
    sj = lax.broadcasted_iota(jnp.int32, (blk, 2 * blk), 1)
    rel = qi + blk - sj
    local = (rel >= 0) & (rel <= WINDOW)
    for sb in range(rows // blk):
        mask = local if sb > 0 else local & ((sj >= blk) | (i > 0))
        mask = jnp.concatenate([mask] * GQA_GROUP, axis=0)
        q_sb = qr[sb * blk:(sb + 1) * blk]
        k_all = k_ext[sb * blk:(sb + 2) * blk]
        v_all = v_ext[sb * blk:(sb + 2) * blk]
        heads = []
        for g in range(N_KV_HEADS):
            qg = jnp.concatenate(
                [q_sb[:, (g * GQA_GROUP + hh) * HEAD_DIM:(g * GQA_GROUP + hh + 1) * HEAD_DIM]
                 for hh in range(GQA_GROUP)], axis=0).astype(BF16)
            kg = k_all[:, g * HEAD_DIM:(g + 1) * HEAD_DIM]
            vg = v_all[:, g * HEAD_DIM:(g + 1) * HEAD_DIM]
            s = lax.dot_general(qg, kg, (((1,), (1,)), ((), ())), preferred_element_type=F32)
            s = jnp.where(mask, s, -jnp.inf)
            sink = jnp.concatenate(
                [jnp.full((blk, 1), sink_ref[g * GQA_GROUP + hh], F32) for hh in range(GQA_GROUP)], axis=0)
            m = jnp.maximum(jnp.max(s, -1, keepdims=True), sink)
            e = jnp.exp(s - m)
            p = e / (jnp.sum(e, -1, keepdims=True) + jnp.exp(sink - m))
            og = jnp.dot(p.astype(BF16), vg, preferred_element_type=F32)
            heads += [og[hh * blk:(hh + 1) * blk] for hh in range(GQA_GROUP)]
        attn = jnp.concatenate(heads, axis=-1)
        mixed_ref[sb * blk:(sb + 1) * blk, :ATTN_WIDTH] = _rms_norm(attn, gna_ref[...]).astype(BF16)

    u = c_ref[...] * x_ref[...]
    n_tail = CONV_K - 1
    tail_rows = cp_ref.shape[0]
    row_id = lax.broadcasted_iota(jnp.int32, u.shape, 0)
    acc = u * convw_ref[CONV_K - 1:CONV_K, :]
    for d in range(1, CONV_K):
        ud = pltpu.roll(u, d, 0)
        for r in range(d):
            src = tail_rows - d + r
            up = jnp.where(i > 0, cp_ref[src:src + 1, :] * xp_ref[src:src + 1, :], 0.0)
            ud = jnp.where(row_id == r, up, ud)
        acc = acc + ud * convw_ref[CONV_K - 1 - d:CONV_K - d, :]
    mixed_ref[:, ATTN_WIDTH:] = _rms_norm(b_ref[...] * acc, gnc_ref[...]).astype(BF16)

    @pl.when(i == n_steps - 1)
    def _():
        kwin_ref[0] = k_cur[rows - blk:, :]
        vwin_ref[0] = v_cur[rows - blk:, :]
        cstate_ref[0] = u[rows - n_tail:, :]


def _prompt_mixer(proj, sinks, rope_tab, conv_w, gn_attn, gn_conv, n_batch, seq):
    blk = ATTN_BLOCK
    qb = MIXER_BLOCKS_PER_STEP
    rows = qb * blk
    nb = seq // rows
    tail = SUBLANES
    wide = CONV_WIDTH
    kvb = COL_KV // (2 * KV_WIDTH)

    def row(b, i, s):
        return b * nb + i

    def prev_blk(b, i, s):
        return jnp.maximum((b * nb + i) * qb - 1, b * nb * qb)

    def prev_tail(b, i, s):
        return jnp.maximum((b * nb + i) * (rows // tail) - 1, 0)

    grid_spec = pltpu.PrefetchScalarGridSpec(
        num_scalar_prefetch=1,
        grid=(n_batch, nb),
        in_specs=[
            pl.BlockSpec((rows, wide), lambda b, i, s: (row(b, i, s), COL_B // wide)),
            pl.BlockSpec((rows, wide), lambda b, i, s: (row(b, i, s), COL_C // wide)),
            pl.BlockSpec((rows, wide), lambda b, i, s: (row(b, i, s), COL_X // wide)),
            pl.BlockSpec((rows, ATTN_WIDTH), lambda b, i, s: (row(b, i, s), COL_Q // ATTN_WIDTH)),
            pl.BlockSpec((rows, 2 * KV_WIDTH), lambda b, i, s: (row(b, i, s), kvb)),
            pl.BlockSpec((blk, 2 * KV_WIDTH), lambda b, i, s: (prev_blk(b, i, s), kvb)),
            pl.BlockSpec((tail, wide), lambda b, i, s: (prev_tail(b, i, s), COL_C // wide)),
            pl.BlockSpec((tail, wide), lambda b, i, s: (prev_tail(b, i, s), COL_X // wide)),
            pl.BlockSpec((3, rows, LANES), lambda b, i, s: (0, i, 0)),
            pl.BlockSpec((3, blk, LANES), lambda b, i, s: (0, jnp.maximum(i * qb - 1, 0), 0)),
            pl.BlockSpec((CONV_K, CONV_WIDTH), lambda b, i, s: (0, 0)),
            pl.BlockSpec((1, ATTN_WIDTH), lambda b, i, s: (0, 0)),
            pl.BlockSpec((1, CONV_WIDTH), lambda b, i, s: (0, 0)),
        ],
        out_specs=[
            pl.BlockSpec((rows, D_MODEL), lambda b, i, s: (row(b, i, s), 0)),
            pl.BlockSpec((1, blk, KV_WIDTH), lambda b, i, s: (b, 0, 0)),
            pl.BlockSpec((1, blk, KV_WIDTH), lambda b, i, s: (b, 0, 0)),
            pl.BlockSpec((1, CONV_K - 1, CONV_WIDTH), lambda b, i, s: (b, 0, 0)),
        ],
    )
    return pl.pallas_call(
        functools.partial(_prompt_mixer_kernel, n_steps=nb),
        out_shape=[
            jax.ShapeDtypeStruct((n_batch * seq, D_MODEL), BF16),
            jax.ShapeDtypeStruct((n_batch, blk, KV_WIDTH), F32),
            jax.ShapeDtypeStruct((n_batch, blk, KV_WIDTH), F32),
            jax.ShapeDtypeStruct((n_batch, CONV_K - 1, CONV_WIDTH), F32),
        ],
        grid_spec=grid_spec,
        compiler_params=pltpu.CompilerParams(
            dimension_semantics=("arbitrary", "arbitrary"), vmem_limit_bytes=VMEM_LIMIT),
        name="prompt_mixer",
    )(sinks, proj, proj, proj, proj, proj, proj, proj, proj, rope_tab, rope_tab, conv_w, gn_attn, gn_conv)


def _expand_groups(t):
    ax = t.ndim - 1
    lane = lax.broadcasted_iota(jnp.int32, t.shape[:-1] + (LANES,), ax)
    chunks = []
    for c in range(KV_WIDTH // LANES):
        a = t[..., c * LANES:(c + 1) * LANES]
        r = pltpu.roll(a, HEAD_DIM, ax)
        lo = jnp.where(lane < HEAD_DIM, a, r)
        hi = jnp.where(lane < HEAD_DIM, r, a)
        chunks += [lo] * (GQA_GROUP // 2) + [hi] * (GQA_GROUP // 2)
    return jnp.concatenate(chunks, axis=-1)


def _sample_mixer_kernel(sink_ref, b_ref, c_ref, x_ref, q_ref, kv_ref, ck_ref, cv_ref, st_ref,
                         rope_ref, seg_ref, segt_ref, convw_ref, gna_ref, gnc_ref,
                         mixed_ref, kwin_ref, vwin_ref, cstate_ref):
    nb, wb = ck_ref.shape[0], ck_ref.shape[1]
    cos_t, sin_lo, sin_hi = rope_ref[0:1, :], rope_ref[1:2, :], rope_ref[2:3, :]
    qr = _rope_wide(q_ref[...], cos_t, sin_lo, sin_hi) * (HEAD_DIM ** -0.5)
    k_new = _rope_wide(kv_ref[:, :KV_WIDTH], cos_t, sin_lo, sin_hi)
    v_new = kv_ref[:, KV_WIDTH:]
    ck = ck_ref[...]
    cv = cv_ref[...]

    seg = seg_ref[...]
    prod = _expand_groups(ck) * qr[:, None, :]
    s_old = jnp.dot(prod.reshape(nb * wb, ATTN_WIDTH).astype(BF16), seg,
                    preferred_element_type=F32).reshape(nb, wb, N_HEADS)
    s_new = jnp.dot((_expand_groups(k_new) * qr).astype(BF16), seg, preferred_element_type=F32)
    sink = sink_ref[...]
    m = jnp.maximum(jnp.maximum(jnp.max(s_old, axis=1), s_new), sink)
    e_old = jnp.exp(s_old - m[:, None, :])
    e_new = jnp.exp(s_new - m)
    inv = 1.0 / (jnp.sum(e_old, axis=1) + e_new + jnp.exp(sink - m))
    p_old = (e_old * inv[:, None, :]).astype(BF16)
    p_new = (e_new * inv).astype(BF16)
    segt = segt_ref[...]
    pe_old = jnp.dot(p_old.reshape(nb * wb, N_HEADS), segt, preferred_element_type=F32).reshape(nb, wb, ATTN_WIDTH)
    pe_new = jnp.dot(p_new, segt, preferred_element_type=F32)
    attn = jnp.sum(pe_old * _expand_groups(cv), axis=1) + pe_new * _expand_groups(v_new)
    mixed_ref[:, :ATTN_WIDTH] = _rms_norm(attn, gna_ref[...]).astype(BF16)

    u = c_ref[...] * x_ref[...]
    acc = u * convw_ref[CONV_K - 1:CONV_K, :]
    for j in range(CONV_K - 1):
        acc = acc + st_ref[j] * convw_ref[j:j + 1, :]
    mixed_ref[:, ATTN_WIDTH:] = _rms_norm(b_ref[...] * acc, gnc_ref[...]).astype(BF16)

    kwin_ref[:, 0:wb - 1, :] = ck_ref[:, 1:wb, :]
    vwin_ref[:, 0:wb - 1, :] = cv_ref[:, 1:wb, :]
    for n in range(nb):
        kwin_ref[n, wb - 1:wb, :] = k_new[n:n + 1, :]
        vwin_ref[n, wb - 1:wb, :] = v_new[n:n + 1, :]
    for j in range(CONV_K - 2):
        cstate_ref[j] = st_ref[j + 1]
    cstate_ref[CONV_K - 2] = u


def _sample_mixer(proj, row0, n_dec, sinks, cache_k, cache_v, state_conv, rope_row, conv_w, gn_attn, gn_conv):
    nb = SAMPLE_CHUNK
    wb = cache_k.shape[1]
    r0 = row0 // nb
    wide = CONV_WIDTH
    head_of_lane = np.arange(ATTN_WIDTH) // HEAD_DIM
    seg = jnp.asarray(head_of_lane[:, None] == np.arange(N_HEADS)[None, :], BF16)
    kvb = COL_KV // (2 * KV_WIDTH)
    return pl.pallas_call(
        _sample_mixer_kernel,
        out_shape=[
            jax.ShapeDtypeStruct((n_dec, D_MODEL), BF16),
            jax.ShapeDtypeStruct((n_dec, wb, KV_WIDTH), F32),
            jax.ShapeDtypeStruct((n_dec, wb, KV_WIDTH), F32),
            jax.ShapeDtypeStruct((CONV_K - 1, n_dec, CONV_WIDTH), F32),
        ],
        grid=(n_dec // nb,),
        in_specs=[
            pl.BlockSpec((1, N_HEADS), lambda i: (0, 0)),
            pl.BlockSpec((nb, wide), lambda i: (r0 + i, COL_B // wide)),
            pl.BlockSpec((nb, wide), lambda i: (r0 + i, COL_C // wide)),
            pl.BlockSpec((nb, wide), lambda i: (r0 + i, COL_X // wide)),
            pl.BlockSpec((nb, ATTN_WIDTH), lambda i: (r0 + i, COL_Q // ATTN_WIDTH)),
            pl.BlockSpec((nb, 2 * KV_WIDTH), lambda i: (r0 + i, kvb)),
            pl.BlockSpec((nb, wb, KV_WIDTH), lambda i: (i, 0, 0)),
            pl.BlockSpec((nb, wb, KV_WIDTH), lambda i: (i, 0, 0)),
            pl.BlockSpec((CONV_K - 1, nb, CONV_WIDTH), lambda i: (0, i, 0)),
            pl.BlockSpec((3, LANES), lambda i: (0, 0)),
            pl.BlockSpec((ATTN_WIDTH, N_HEADS), lambda i: (0, 0)),
            pl.BlockSpec((N_HEADS, ATTN_WIDTH), lambda i: (0, 0)),
            pl.BlockSpec((CONV_K, CONV_WIDTH), lambda i: (0, 0)),
            pl.BlockSpec((1, ATTN_WIDTH), lambda i: (0, 0)),
            pl.BlockSpec((1, CONV_WIDTH), lambda i: (0, 0)),
        ],
        out_specs=[
            pl.BlockSpec((nb, D_MODEL), lambda i: (i, 0)),
            pl.BlockSpec((nb, wb, KV_WIDTH), lambda i: (i, 0, 0)),
            pl.BlockSpec((nb, wb, KV_WIDTH), lambda i: (i, 0, 0)),
            pl.BlockSpec((CONV_K - 1, nb, CONV_WIDTH), lambda i: (0, i, 0)),
        ],
        compiler_params=pltpu.CompilerParams(
            dimension_semantics=("arbitrary",), vmem_limit_bytes=VMEM_LIMIT),
        name="sample_mixer",
    )(sinks.reshape(1, N_HEADS), proj, proj, proj, proj, proj, cache_k, cache_v, state_conv,
      rope_row, seg, seg.T, conv_w, gn_attn, gn_conv)


def _outproj_kernel(mixed_ref, x_ref, g0_ref, b0_ref, wout_ref, g1_ref, b1_ref, h1_ref, h1b_ref):
    h = _layer_norm(x_ref[...], g0_ref[...], b0_ref[...])
    mixed = jnp.dot(mixed_ref[...], wout_ref[...], preferred_element_type=F32)
    h1 = _layer_norm(DEEPNORM_ALPHA * h + mixed, g1_ref[...], b1_ref[...])
    h1_ref[...] = h1
    h1b_ref[...] = h1.astype(BF16)


def _outproj(mixed, x_all, g0, b0, w_out, g1, b1):
    t = x_all.shape[0]
    tm = ROW_TILE // 2
    const = lambda i: (0, 0)
    resident = functools.partial(pl.BlockSpec, index_map=const, pipeline_mode=pl.Buffered(1))
    return pl.pallas_call(
        _outproj_kernel,
        out_shape=[
            jax.ShapeDtypeStruct((t, D_MODEL), F32),
            jax.ShapeDtypeStruct((t, D_MODEL), BF16),
        ],
        grid=(t // tm,),
        in_specs=[
            pl.BlockSpec((tm, D_MODEL), lambda i: (i, 0)),
            pl.BlockSpec((tm, D_MODEL), lambda i: (i, 0)),
            pl.BlockSpec((1, D_MODEL), const),
            pl.BlockSpec((1, D_MODEL), const),
            resident((D_MODEL, D_MODEL)),
            pl.BlockSpec((1, D_MODEL), const),
            pl.BlockSpec((1, D_MODEL), const),
        ],
        out_specs=[
            pl.BlockSpec((tm, D_MODEL), lambda i: (i, 0)),
            pl.BlockSpec((tm, D_MODEL), lambda i: (i, 0)),
        ],
        compiler_params=pltpu.CompilerParams(
            dimension_semantics=("arbitrary",), vmem_limit_bytes=VMEM_LIMIT),
        name="outproj",
    )(mixed, x_all, g0, b0, w_out, g1, b1)


def _route_kernel(h_ref, wrt_ref, bias_ref, tri_ref, ones_ref, e_ref, gate_ref, rank_ref, cnt_ref):
    tm = h_ref.shape[0]
    neg = -jnp.inf
    big = float(N_EXPERTS)

    @pl.when(pl.program_id(0) == 0)
    def _():
        cnt_ref[...] = jnp.zeros_like(cnt_ref)

    logits = lax.dot_general(wrt_ref[...], h_ref[...], (((1,), (1,)), ((), ())), preferred_element_type=F32)
    scores = jax.nn.sigmoid(logits)
    biased = scores + bias_ref[...]

    per_group = N_EXPERTS // N_EXPERT_GROUPS
    rid = lax.broadcasted_iota(jnp.int32, (per_group, tm), 0).astype(F32)
    gscore = []
    for g in range(N_EXPERT_GROUPS):
        xg = biased[g * per_group:(g + 1) * per_group]
        m1 = jnp.max(xg, 0, keepdims=True)
        i1 = jnp.min(jnp.where(xg == m1, rid, big), 0, keepdims=True)
        m2 = jnp.max(jnp.where(rid == i1, neg, xg), 0, keepdims=True)
        gscore.append(m1 + m2)
    masked = []
    for g in range(N_EXPERT_GROUPS):
        beaten = jnp.zeros((1, tm), F32)
        for o in range(N_EXPERT_GROUPS):
            if o != g:
                wins = (gscore[o] > gscore[g]) | ((gscore[o] == gscore[g]) & (o < g))
                beaten = beaten + jnp.where(wins, 1.0, 0.0)
        masked.append(jnp.where(beaten < TOPK_GROUPS, biased[g * per_group:(g + 1) * per_group], neg))
    x = jnp.concatenate(masked, axis=0)

    eid = lax.broadcasted_iota(jnp.int32, (N_EXPERTS, tm), 0).astype(F32)
    sel = jnp.zeros((N_EXPERTS, tm), F32)
    picks, gates = [], []
    for _ in range(TOP_K):
        m = jnp.max(x, 0, keepdims=True)
        ik = jnp.min(jnp.where(x == m, eid, big), 0, keepdims=True)
        hit = eid == ik
        gates.append(jnp.sum(jnp.where(hit, scores, 0.0), 0, keepdims=True))
        picks.append(ik)
        x = jnp.where(hit, neg, x)
        sel = jnp.where(hit, 1.0, sel)

    selb = sel.astype(BF16)
    pos = cnt_ref[...] + jnp.dot(selb, tri_ref[...], preferred_element_type=F32)
    cnt_ref[...] = cnt_ref[...] + jnp.dot(selb, ones_ref[...], preferred_element_type=F32)

    gsum = gates[0]
    for k in range(1, TOP_K):
        gsum = gsum + gates[k]
    for k in range(TOP_K):
        e_ref[k:k + 1, :] = picks[k].astype(jnp.int32)
        gate_ref[k:k + 1, :] = gates[k] / gsum * ROUTED_SCALE
        rank_ref[k:k + 1, :] = jnp.sum(jnp.where(eid == picks[k], pos, 0.0), 0, keepdims=True).astype(jnp.int32)


def _route(h1b, w_router_t, bias_col):
    t = h1b.shape[0]
    tm = LANES
    tri = jnp.asarray(np.arange(tm)[:, None] < np.arange(tm)[None, :], BF16)
    ones = jnp.ones((tm, tm), BF16)
    const = lambda i: (0, 0)
    return pl.pallas_call(
        _route_kernel,
        out_shape=[
            jax.ShapeDtypeStruct((TOP_K, t), jnp.int32),
            jax.ShapeDtypeStruct((TOP_K, t), F32),
            jax.ShapeDtypeStruct((TOP_K, t), jnp.int32),
            jax.ShapeDtypeStruct((N_EXPERTS, tm), F32),
        ],
        grid=(t // tm,),
        in_specs=[
            pl.BlockSpec((tm, D_MODEL), lambda i: (i, 0)),
            pl.BlockSpec((N_EXPERTS, D_MODEL), const),
            pl.BlockSpec((N_EXPERTS, 1), const),
            pl.BlockSpec((tm, tm), const),
            pl.BlockSpec((tm, tm), const),
        ],
        out_specs=[
            pl.BlockSpec((TOP_K, tm), lambda i: (0, i)),
            pl.BlockSpec((TOP_K, tm), lambda i: (0, i)),
            pl.BlockSpec((TOP_K, tm), lambda i: (0, i)),
            pl.BlockSpec((N_EXPERTS, tm), const),
        ],
        compiler_params=pltpu.CompilerParams(
            dimension_semantics=("arbitrary",), vmem_limit_bytes=VMEM_LIMIT),
        name="route",
    )(h1b, w_router_t, bias_col, tri, ones)


def _dest_kernel(pstart_ref, e_ref, rank_ref, dest_ref):
    e = e_ref[...]

    def body(x, acc):
        return jnp.where(e == x, pstart_ref[x], acc)
    dest_ref[...] = lax.fori_loop(0, N_EXPERTS, body, jnp.zeros_like(e)) + rank_ref[...]


def _dest_rows(pad_start, top_e, rank):
    whole = pl.BlockSpec(top_e.shape, lambda i, ps: (0, 0))
    return pl.pallas_call(
        _dest_kernel,
        out_shape=jax.ShapeDtypeStruct(top_e.shape, jnp.int32),
        grid_spec=pltpu.PrefetchScalarGridSpec(num_scalar_prefetch=1, grid=(1,), in_specs=[whole, whole],
                                               out_specs=whole),
        name="dest_rows",
    )(pad_start, top_e, rank)


def _source_rows_kernel(dest_ref, src_ref):
    i = pl.program_id(0)
    tm = dest_ref.shape[0] // TOP_K
    unroll = SUBLANES

    @pl.when(i == 0)
    def _():
        def clear(j, c):
            for r in range(unroll):
                src_ref[j * unroll + r] = 0
            return c
        lax.fori_loop(0, src_ref.shape[0] // unroll, clear, 0)

    for k in range(TOP_K):
        for t in range(tm):
            src_ref[dest_ref[k * tm + t]] = i * tm + t


def _tile_major(a, tm):
    k, t = a.shape
    return a.reshape(k, t // tm, tm).swapaxes(0, 1).reshape(-1)


def _source_rows(dest, n_rows):
    tm = LANES
    assert n_rows % SUBLANES == 0
    return pl.pallas_call(
        _source_rows_kernel,
        out_shape=jax.ShapeDtypeStruct((n_rows,), jnp.int32),
        grid=(dest.shape[0] // (TOP_K * tm),),
        in_specs=[pl.BlockSpec((TOP_K * tm,), lambda i: (i,), memory_space=pltpu.SMEM)],
        out_specs=pl.BlockSpec((n_rows,), lambda i: (0,), memory_space=pltpu.SMEM),
        compiler_params=pltpu.CompilerParams(dimension_semantics=("arbitrary",)),
        name="source_rows",
    )(dest)


def _experts_kernel(be_ref, first_ref, slot_ref, ahead_ref, head_ref, nused_ref, src_ref, src_next_ref,
                    h_hbm, wg_hbm, wu_hbm, wd_hbm, y_ref,
                    x_buf, wg_buf, wu_buf, wd_buf, wg_bf, wu_bf, wd_bf, x_sems, sems):
    i = pl.program_id(0)
    n_slots = wg_buf.shape[0]
    bm = x_buf.shape[1]

    def start_gather(rows_ref, xs):
        for r in range(bm):
            pltpu.make_async_copy(h_hbm.at[pl.ds(rows_ref[0, 0, r], 1), :], x_buf.at[xs, pl.ds(r, 1), :],
                                  x_sems.at[xs]).start()

    @pl.when(i == 0)
    def _():
        start_gather(src_ref, 0)

    for xs in range(2):
        @pl.when((i + 1 < nused_ref[0]) & ((i + 1) % 2 == xs))
        def _():
            start_gather(src_next_ref, xs)

    def weight_copies(e, slot):
        return (pltpu.make_async_copy(wg_hbm.at[e], wg_buf.at[slot], sems.at[slot, 0]),
                pltpu.make_async_copy(wu_hbm.at[e], wu_buf.at[slot], sems.at[slot, 1]),
                pltpu.make_async_copy(wd_hbm.at[e], wd_buf.at[slot], sems.at[slot, 2]))

    @pl.when(i == 0)
    def _():
        for r in range(n_slots - 1):
            @pl.when(head_ref[r] >= 0)
            def _():
                for cp in weight_copies(head_ref[r], r):
                    cp.start(priority=1)

    @pl.when(first_ref[i] == 1)
    def _():
        slot = slot_ref[i]
        for cp in weight_copies(be_ref[i], slot):
            cp.wait()

        @pl.when(ahead_ref[i] >= 0)
        def _():
            for cp in weight_copies(ahead_ref[i], (slot + n_slots - 1) % n_slots):
                cp.start(priority=1)

        wg_bf[...] = wg_buf[slot].astype(BF16)
        wu_bf[...] = wu_buf[slot].astype(BF16)
        wd_bf[...] = wd_buf[slot].astype(BF16)

    @pl.when(i < nused_ref[0])
    def _():
        xs = i % 2
        pltpu.make_async_copy(h_hbm.at[pl.ds(0, bm), :], x_buf.at[xs], x_sems.at[xs]).wait()
        x = x_buf[xs].astype(BF16)
        g = jnp.dot(x, wg_bf[...], preferred_element_type=F32)
        u = jnp.dot(x, wu_bf[...], preferred_element_type=F32)
        a = (g * jax.nn.sigmoid(g) * u).astype(BF16)
        y_ref[...] = jnp.dot(a, wd_bf[...], preferred_element_type=F32)

    @pl.when(i >= nused_ref[0])
    def _():
        y_ref[...] = jnp.zeros_like(y_ref)


def _experts(block_e, first, slot, ahead, head, n_used, src_rows, h, w_gate, w_up, w_down):
    p = src_rows.shape[0]
    bm = EXPERT_ROWS
    n_blocks = p // bm
    ns = WEIGHT_SLOTS
    src3 = src_rows.reshape(n_blocks, 1, bm)

    def cur_blk(i, be, fi, sl, ah, hd, nu):
        return (jnp.minimum(i, nu[0] - 1), 0, 0)

    def next_blk(i, be, fi, sl, ah, hd, nu):
        return (jnp.minimum(i + 1, nu[0] - 1), 0, 0)

    grid_spec = pltpu.PrefetchScalarGridSpec(
        num_scalar_prefetch=6,
        grid=(n_blocks,),
        in_specs=[
            pl.BlockSpec((1, 1, bm), cur_blk, memory_space=pltpu.SMEM),
            pl.BlockSpec((1, 1, bm), next_blk, memory_space=pltpu.SMEM),
            pl.BlockSpec(memory_space=pl.ANY),
            pl.BlockSpec(memory_space=pl.ANY),
            pl.BlockSpec(memory_space=pl.ANY),
            pl.BlockSpec(memory_space=pl.ANY),
        ],
        out_specs=pl.BlockSpec((bm, D_MODEL), lambda i, *_: (i, 0)),
        scratch_shapes=[
            pltpu.VMEM((2, bm, D_MODEL), F32),
            pltpu.VMEM((ns, D_MODEL, D_EXPERT), F32),
            pltpu.VMEM((ns, D_MODEL, D_EXPERT), F32),
            pltpu.VMEM((ns, D_EXPERT, D_MODEL), F32),
            pltpu.VMEM((D_MODEL, D_EXPERT), BF16),
            pltpu.VMEM((D_MODEL, D_EXPERT), BF16),
            pltpu.VMEM((D_EXPERT, D_MODEL), BF16),
            pltpu.SemaphoreType.DMA((2,)),
            pltpu.SemaphoreType.DMA((ns, 3)),
        ],
    )
    return pl.pallas_call(
        _experts_kernel,
        out_shape=jax.ShapeDtypeStruct((p, D_MODEL), F32),
        grid_spec=grid_spec,
        compiler_params=pltpu.CompilerParams(
            dimension_semantics=("arbitrary",), vmem_limit_bytes=VMEM_LIMIT),
        name="experts",
    )(block_e, first, slot, ahead, head, n_used, src3, src3, h, w_gate, w_up, w_down)


def _combine_kernel(dest_ref, gate_ref, yb_hbm, o_ref, buf, sems):
    i = pl.program_id(0)
    n_tiles = pl.num_programs(0) - 1
    tm = o_ref.shape[0]
    n_rows = TOP_K * tm

    for s in range(2):
        @pl.when((i < n_tiles) & (i % 2 == s))
        def _():
            for idx in range(n_rows):
                pltpu.make_async_copy(yb_hbm.at[pl.ds(dest_ref[idx], 1), :], buf.at[s, pl.ds(idx, 1), :],
                                      sems.at[s]).start(priority=idx % 2)

    @pl.when(i >= 1)
    def _():
        slot = (i - 1) % 2
        pltpu.make_async_copy(yb_hbm.at[pl.ds(0, n_rows), :], buf.at[slot], sems.at[slot]).wait()
        acc = gate_ref[:, 0:1] * buf[slot, 0:tm, :]
        for k in range(1, TOP_K):
            acc = acc + gate_ref[:, k:k + 1] * buf[slot, k * tm:(k + 1) * tm, :]
        o_ref[...] = acc


def _combine(dest, gate_tk, yb):
    t = gate_tk.shape[0]
    tm = LANES
    n_tiles = t // tm
    return pl.pallas_call(
        _combine_kernel,
        out_shape=jax.ShapeDtypeStruct((t, D_MODEL), F32),
        grid=(n_tiles + 1,),
        in_specs=[
            pl.BlockSpec((TOP_K * tm,), lambda i: (jnp.minimum(i, n_tiles - 1),), memory_space=pltpu.SMEM),
            pl.BlockSpec((tm, TOP_K), lambda i: (jnp.maximum(i - 1, 0), 0)),
            pl.BlockSpec(memory_space=pl.ANY),
        ],
        out_specs=pl.BlockSpec((tm, D_MODEL), lambda i: (jnp.maximum(i - 1, 0), 0)),
        scratch_shapes=[pltpu.VMEM((2, TOP_K * tm, D_MODEL), F32), pltpu.SemaphoreType.DMA((2,))],
        compiler_params=pltpu.CompilerParams(
            dimension_semantics=("arbitrary",), vmem_limit_bytes=VMEM_LIMIT),
        name="combine",
    )(dest, gate_tk, yb)


def _final_kernel(h1_ref, h1b_ref, routed_ref, p_ref, wsg_ref, wsu_ref, wsd_ref, wpg_ref, wp_ref,
                  g2_ref, b2_ref, y_ref):
    hb = h1b_ref[...]
    g = jnp.dot(hb, wsg_ref[...], preferred_element_type=F32)
    u = jnp.dot(hb, wsu_ref[...], preferred_element_type=F32)
    shared = jnp.dot((g * jax.nn.sigmoid(g) * u).astype(BF16), wsd_ref[...], preferred_element_type=F32)
    gate = jax.nn.sigmoid(jnp.dot(hb, wpg_ref[...], preferred_element_type=F32))
    ple = gate * jnp.dot(p_ref[...], wp_ref[...], preferred_element_type=F32)
    r = DEEPNORM_ALPHA * h1_ref[...] + (routed_ref[...] + shared) + ple
    y_ref[...] = _layer_norm(r, g2_ref[...], b2_ref[...])


def _final(h1, h1b, routed, p_all, ws_gate, ws_up, ws_down, w_ple_gate, w_ple, g2, b2, *, row0, n_rows, tm):
    assert row0 % tm == 0 and n_rows % tm == 0
    blk0 = row0 // tm
    const = lambda i: (0, 0)
    rows = lambda i: (blk0 + i, 0)
    resident = functools.partial(pl.BlockSpec, index_map=const, pipeline_mode=pl.Buffered(1))
    return pl.pallas_call(
        _final_kernel,
        out_shape=jax.ShapeDtypeStruct((n_rows, D_MODEL), F32),
        grid=(n_rows // tm,),
        in_specs=[
            pl.BlockSpec((tm, D_MODEL), rows),
            pl.BlockSpec((tm, D_MODEL), rows),
            pl.BlockSpec((tm, D_MODEL), rows),
            pl.BlockSpec((tm, PLE_DIM), rows),
            resident((D_MODEL, D_SHARED)),
            resident((D_MODEL, D_SHARED)),
            resident((D_SHARED, D_MODEL)),
            resident((D_MODEL, D_MODEL)),
            resident((PLE_DIM, D_MODEL)),
            pl.BlockSpec((1, D_MODEL), const),
            pl.BlockSpec((1, D_MODEL), const),
        ],
        out_specs=pl.BlockSpec((tm, D_MODEL), lambda i: (i, 0)),
        compiler_params=pltpu.CompilerParams(
            dimension_semantics=("arbitrary",), vmem_limit_bytes=VMEM_LIMIT),
        name="final",
    )(h1, h1b, routed, p_all, ws_gate, ws_up, ws_down, w_ple_gate, w_ple, g2, b2)


def _rope_tables(pos):
    half = ROT_DIM // 2
    inv_freq = ROPE_THETA ** (-jnp.arange(half, dtype=F32) * 2.0 / ROT_DIM)
    ang = pos.astype(F32)[:, None] * inv_freq[None, :]
    cos, sin = jnp.cos(ang), jnp.sin(ang)
    n = pos.shape[0]
    ones = jnp.ones((n, HEAD_DIM - ROT_DIM), F32)
    zeros = jnp.zeros((n, HEAD_DIM - ROT_DIM), F32)
    zh = jnp.zeros((n, half), F32)
    cos_t = jnp.concatenate([cos, cos, ones], -1)
    sin_lo = jnp.concatenate([-sin, zh, zeros], -1)
    sin_hi = jnp.concatenate([zh, sin, zeros], -1)
    per_head = jnp.stack([cos_t, sin_lo, sin_hi])
    return jnp.concatenate([per_head] * (LANES // HEAD_DIM), -1)


def _block_plan(counts, n_blocks):
    bm = EXPERT_ROWS
    i32 = jnp.int32
    padded = (counts + bm - 1) // bm * bm
    pad_end = jnp.cumsum(padded)
    pad_start = pad_end - padded
    n_used = pad_end[-1] // bm
    blk = jnp.arange(n_blocks)
    valid = blk < n_used
    block_e = jnp.minimum(jnp.sum(pad_end[None, :] <= (blk * bm)[:, None], axis=1), N_EXPERTS - 1)
    block_e = jnp.where(valid, block_e, block_e[jnp.maximum(n_used - 1, 0)])
    prev_e = jnp.concatenate([jnp.full((1,), -1, block_e.dtype), block_e[:-1]])
    first = valid & (block_e != prev_e)
    run = jnp.cumsum(first) - 1
    ns = WEIGHT_SLOTS
    used = counts > 0
    n_run_slots = N_EXPERTS + ns
    run_expert = jnp.full((n_run_slots + 1,), -1, i32).at[
        jnp.where(used, jnp.cumsum(used) - 1, n_run_slots)].set(jnp.arange(N_EXPERTS, dtype=i32))
    run_expert = run_expert[:n_run_slots]
    ahead = jnp.where(first, run_expert[run + ns - 1], -1)
    head = run_expert[:ns - 1]
    return (pad_start.astype(i32), n_used.astype(i32).reshape(1), block_e.astype(i32), first.astype(i32), (run % ns).astype(i32),
            ahead.astype(i32), head.astype(i32))


def kernel(x_prompt, x_sample, p_prompt, p_sample, cache_k, cache_v, state_conv, ln_in_g, ln_in_b, w_in,
           attn_sinks, conv_w, gn_attn, gn_conv, w_out, ln1_g, ln1_b, w_router, router_bias, w_gate, w_up,
           w_down, ws_gate, ws_up, ws_down, w_ple_gate, w_ple, ln2_g, ln2_b):
    n_batch, seq, d = x_prompt.shape
    n_dec, dec_seq, _ = x_sample.shape
    depth = w_in.shape[0]
    wb = cache_k.shape[2]
    assert depth == 1 and dec_seq == 1 and d == D_MODEL
    assert wb == ATTN_BLOCK and seq % (ATTN_BLOCK * MIXER_BLOCKS_PER_STEP) == 0 and n_dec % SAMPLE_CHUNK == 0
    t_p = n_batch * seq
    t = t_p + n_dec
    assert t % ROW_TILE == 0 and t_p % SAMPLE_CHUNK == 0

    x_all = jnp.concatenate([x_prompt.reshape(t_p, d), x_sample.reshape(n_dec, d)], 0)
    p_all = jnp.concatenate([p_prompt[0].reshape(t_p, PLE_DIM), p_sample[0].reshape(n_dec, PLE_DIM)], 0).astype(BF16)
    row = lambda v: v.reshape(1, -1).astype(F32)

    w_in0 = w_in[0]
    qkv_w = ATTN_WIDTH + 2 * KV_WIDTH
    w_in_perm = jnp.concatenate([w_in0[:, qkv_w:], w_in0[:, :qkv_w]], axis=1).astype(BF16)

    proj = _ln_inproj(x_all, row(ln_in_g), row(ln_in_b), w_in_perm)

    rope_p = _rope_tables(jnp.arange(seq))
    rope_s = _rope_tables(PAST_LEN + jnp.arange(dec_seq))[:, 0, :]
    sinks = attn_sinks[0].astype(F32)
    cw, ga, gc = conv_w[0].astype(F32), row(gn_attn[0]), row(gn_conv[0])
    mixed_p, kwin_p, vwin_p, cstate_p = _prompt_mixer(proj, sinks, rope_p, cw, ga, gc, n_batch, seq)
    mixed_s, kwin_s, vwin_s, cstate_s = _sample_mixer(
        proj, t_p, n_dec, sinks, cache_k[0].reshape(n_dec, wb, KV_WIDTH), cache_v[0].reshape(n_dec, wb, KV_WIDTH),
        jnp.swapaxes(state_conv[0], 0, 1), rope_s, cw, ga, gc)
    cstate_s = jnp.swapaxes(cstate_s, 0, 1)
    mixed = jnp.concatenate([mixed_p, mixed_s], 0)

    h1, h1b = _outproj(
        mixed, x_all, row(ln_in_g), row(ln_in_b), w_out[0].astype(BF16), row(ln1_g[0]), row(ln1_b[0]))

    top_e, gate, rank, cnt = _route(h1b, w_router[0].T.astype(BF16), router_bias[0].astype(F32).reshape(N_EXPERTS, 1))
    n_blocks = t * TOP_K // EXPERT_ROWS + N_EXPERTS
    pad_start, n_used, block_e, first, slot, ahead, head = _block_plan(cnt[:, 0].astype(jnp.int32), n_blocks)
    dest = _tile_major(_dest_rows(pad_start, top_e, rank), LANES)
    src_rows = _source_rows(dest, n_blocks * EXPERT_ROWS)
    yb = _experts(block_e, first, slot, ahead, head, n_used, src_rows, h1, w_gate[0], w_up[0], w_down[0])
    routed = _combine(dest, gate.T, yb)

    final_args = (h1, h1b, routed, p_all, ws_gate[0].astype(BF16), ws_up[0].astype(BF16), ws_down[0].astype(BF16),
                  w_ple_gate[0].astype(BF16), w_ple[0].astype(BF16), row(ln2_g[0]), row(ln2_b[0]))
    y_p = _final(*final_args, row0=0, n_rows=t_p, tm=FINAL_TILE)
    y_s = _final(*final_args, row0=t_p, n_rows=n_dec, tm=n_dec)

    kv_shape = (1, -1, wb, N_KV_HEADS, HEAD_DIM)
    return (y_p.reshape(n_batch, seq, d), y_s.reshape(n_dec, dec_seq, d),
            kwin_p.reshape(kv_shape), vwin_p.reshape(kv_shape), cstate_p[None],
            kwin_s.reshape(kv_shape), vwin_s.reshape(kv_shape), cstate_s[None])
```

```python
import functools

import numpy as np
import jax
import jax.numpy as jnp
from jax import lax
from jax.experimental import pallas as pl
from jax.experimental.pallas import tpu as pltpu

D_MODEL = 2048
N_HEADS = 16
N_KV_HEADS = 4
HEAD_DIM = 64
GQA_GROUP = N_HEADS // N_KV_HEADS
ATTN_WIDTH = N_HEADS * HEAD_DIM
KV_WIDTH = N_KV_HEADS * HEAD_DIM
WINDOW = 128
ATTN_BLOCK = WINDOW
ROT_DIM = HEAD_DIM // 4
ROPE_THETA = 500000.0
CONV_WIDTH = D_MODEL - ATTN_WIDTH
CONV_K = 3
IN_COLS = ATTN_WIDTH + 2 * KV_WIDTH + 3 * CONV_WIDTH
N_EXPERTS = 256
TOP_K = 8
N_EXPERT_GROUPS = 8
TOPK_GROUPS = 4
D_EXPERT = 512
D_SHARED = 512
ROUTED_SCALE = 2.5
PLE_DIM = 256
LN_EPS = 1e-5
DEPTH = 1
PAST_LEN = 8192
DEEPNORM_ALPHA = (2 * DEPTH) ** 0.25

LANES = 128
SUBLANES = 8
COL_B, COL_C, COL_X, COL_Q, COL_KV = 0, CONV_WIDTH, 2 * CONV_WIDTH, 3 * CONV_WIDTH, 3 * CONV_WIDTH + ATTN_WIDTH

ROW_TILE = 640
INPROJ_COL_TILE = 1536
FINAL_TILE = 256
EXPERT_ROWS = 128
WEIGHT_SLOTS = 3
SAMPLE_CHUNK = 8
MIXER_BLOCKS_PER_STEP = 2
VMEM_LIMIT = 56 * 1024 * 1024

BF16 = jnp.bfloat16
F32 = jnp.float32


def _layer_norm(x, g, b):
    xc = x - jnp.mean(x, -1, keepdims=True)
    var = jnp.mean(xc * xc, -1, keepdims=True)
    return xc * lax.rsqrt(var + LN_EPS) * g + b


def _rms_norm(x, g):
    return x * lax.rsqrt(jnp.mean(x * x, -1, keepdims=True) + LN_EPS) * g


def _rope_lanes(x, cos_t, sin_lo, sin_hi):
    half = ROT_DIM // 2
    ax = x.ndim - 1
    return x * cos_t + pltpu.roll(x, LANES - half, ax) * sin_lo + pltpu.roll(x, half, ax) * sin_hi


def _rope_wide(x, cos_t, sin_lo, sin_hi):
    n = x.shape[-1] // LANES
    return jnp.concatenate(
        [_rope_lanes(x[..., c * LANES:(c + 1) * LANES], cos_t, sin_lo, sin_hi) for c in range(n)], axis=-1)


def _ln_inproj_kernel(x_ref, g_ref, b_ref, w_ref, o_ref, h_scr):
    @pl.when(pl.program_id(1) == 0)
    def _():
        h_scr[...] = _layer_norm(x_ref[...], g_ref[...], b_ref[...]).astype(BF16)

    o_ref[...] = jnp.dot(h_scr[...], w_ref[...], preferred_element_type=F32)


def _ln_inproj(x_all, g, b, w_bf16):
    t = x_all.shape[0]
    return pl.pallas_call(
        _ln_inproj_kernel,
        out_shape=jax.ShapeDtypeStruct((t, IN_COLS), F32),
        grid=(t // ROW_TILE, IN_COLS // INPROJ_COL_TILE),
        in_specs=[
            pl.BlockSpec((ROW_TILE, D_MODEL), lambda i, j: (i, 0)),
            pl.BlockSpec((1, D_MODEL), lambda i, j: (0, 0)),
            pl.BlockSpec((1, D_MODEL), lambda i, j: (0, 0)),
            pl.BlockSpec((D_MODEL, INPROJ_COL_TILE), lambda i, j: (0, j)),
        ],
        out_specs=pl.BlockSpec((ROW_TILE, INPROJ_COL_TILE), lambda i, j: (i, j)),
        scratch_shapes=[pltpu.VMEM((ROW_TILE, D_MODEL), BF16)],
        compiler_params=pltpu.CompilerParams(
            dimension_semantics=("arbitrary", "arbitrary"), vmem_limit_bytes=VMEM_LIMIT),
        name="ln_inproj",
    )(x_all, g, b, w_bf16)


def _prompt_mixer_kernel(sink_ref, b_ref, c_ref, x_ref, q_ref, kv_ref, kvp_ref, cp_ref, xp_ref,
                         rope_ref, ropep_ref, convw_ref, gna_ref, gnc_ref,
                         mixed_ref, kwin_ref, vwin_ref, cstate_ref, *, n_steps):
    i = pl.program_id(1)
    blk = ATTN_BLOCK
    rows = q_ref.shape[0]
    cos_t, sin_lo, sin_hi = rope_ref[0], rope_ref[1], rope_ref[2]
    qr = _rope_wide(q_ref[...], cos_t, sin_lo, sin_hi) * (HEAD_DIM ** -0.5)
    k_cur = _rope_wide(kv_ref[:, :KV_WIDTH], cos_t, sin_lo, sin_hi)
    k_prev = _rope_wide(kvp_ref[:, :KV_WIDTH], ropep_ref[0], ropep_ref[1], ropep_ref[2])
    v_cur = kv_ref[:, KV_WIDTH:]
    v_prev = kvp_ref[:, KV_WIDTH:]
    k_ext = jnp.concatenate([k_prev, k_cur], axis=0).astype(BF16)
    v_ext = jnp.concatenate([v_prev, v_cur], axis=0).astype(BF16)

    qi = lax.broadcasted_iota(jnp.int32, (blk, 2 * blk), 0)
    sj = lax.broadcasted_iota(jnp.int32, (blk, 2 * blk), 1)
    rel = qi + blk - sj
    local = (rel >= 0) & (rel <= WINDOW)
    for sb in range(rows // blk):
        mask = local if sb > 0 else local & ((sj >= blk) | (i > 0))
        mask = jnp.concatenate([mask] * GQA_GROUP, axis=0)
        q_sb = qr[sb * blk:(sb + 1) * blk]
        k_all = k_ext[sb * blk:(sb + 2) * blk]
        v_all = v_ext[sb * blk:(sb + 2) * blk]
        heads = []
        for g in range(N_KV_HEADS):
            qg = jnp.concatenate(
                [q_sb[:, (g * GQA_GROUP + hh) * HEAD_DIM:(g * GQA_GROUP + hh + 1) * HEAD_DIM]
                 for hh in range(GQA_GROUP)], axis=0).astype(BF16)
            kg = k_all[:, g * HEAD_DIM:(g + 1) * HEAD_DIM]
            vg = v_all[:, g * HEAD_DIM:(g + 1) * HEAD_DIM]
            s = lax.dot_general(qg, kg, (((1,), (1,)), ((), ())), preferred_element_type=F32)
            s = jnp.where(mask, s, -jnp.inf)
            sink = jnp.concatenate(
                [jnp.full((blk, 1), sink_ref[g * GQA_GROUP + hh], F32) for hh in range(GQA_GROUP)], axis=0)
            m = jnp.maximum(jnp.max(s, -1, keepdims=True), sink)
            e = jnp.exp(s - m)
            p = e / (jnp.sum(e, -1, keepdims=True) + jnp.exp(sink - m))
            og = jnp.dot(p.astype(BF16), vg, preferred_element_type=F32)
            heads += [og[hh * blk:(hh + 1) * blk] for hh in range(GQA_GROUP)]
        attn = jnp.concatenate(heads, axis=-1)
        mixed_ref[sb * blk:(sb + 1) * blk, :ATTN_WIDTH] = _rms_norm(attn, gna_ref[...]).astype(BF16)

    u = c_ref[...] * x_ref[...]
    n_tail = CONV_K - 1
    tail_rows = cp_ref.shape[0]
    row_id = lax.broadcasted_iota(jnp.int32, u.shape, 0)
    acc = u * convw_ref[CONV_K - 1:CONV_K, :]
    for d in range(1, CONV_K):
        ud = pltpu.roll(u, d, 0)
        for r in range(d):
            src = tail_rows - d + r
            up = jnp.where(i > 0, cp_ref[src:src + 1, :] * xp_ref[src:src + 1, :], 0.0)
            ud = jnp.where(row_id == r, up, ud)
        acc = acc + ud * convw_ref[CONV_K - 1 - d:CONV_K - d, :]
    mixed_ref[:, ATTN_WIDTH:] = _rms_norm(b_ref[...] * acc, gnc_ref[...]).astype(BF16)

    @pl.when(i == n_steps - 1)
    def _():
        kwin_ref[0] = k_cur[rows - blk:, :]
        vwin_ref[0] = v_cur[rows - blk:, :]
        cstate_ref[0] = u[rows - n_tail:, :]


def _prompt_mixer(proj, sinks, rope_tab, conv_w, gn_attn, gn_conv, n_batch, seq):
    blk = ATTN_BLOCK
    qb = MIXER_BLOCKS_PER_STEP
    rows = qb * blk
    nb = seq // rows
    tail = SUBLANES
    wide = CONV_WIDTH
    kvb = COL_KV // (2 * KV_WIDTH)

    def row(b, i, s):
        return b * nb + i

    def prev_blk(b, i, s):
        return jnp.maximum((b * nb + i) * qb - 1, b * nb * qb)

    def prev_tail(b, i, s):
        return jnp.maximum((b * nb + i) * (rows // tail) - 1, 0)

    grid_spec = pltpu.PrefetchScalarGridSpec(
        num_scalar_prefetch=1,
        grid=(n_batch, nb),
        in_specs=[
            pl.BlockSpec((rows, wide), lambda b, i, s: (row(b, i, s), COL_B // wide)),
            pl.BlockSpec((rows, wide), lambda b, i, s: (row(b, i, s), COL_C // wide)),
            pl.BlockSpec((rows, wide), lambda b, i, s: (row(b, i, s), COL_X // wide)),
            pl.BlockSpec((rows, ATTN_WIDTH), lambda b, i, s: (row(b, i, s), COL_Q // ATTN_WIDTH)),
            pl.BlockSpec((rows, 2 * KV_WIDTH), lambda b, i, s: (row(b, i, s), kvb)),
            pl.BlockSpec((blk, 2 * KV_WIDTH), lambda b, i, s: (prev_blk(b, i, s), kvb)),
            pl.BlockSpec((tail, wide), lambda b, i, s: (prev_tail(b, i, s), COL_C // wide)),
            pl.BlockSpec((tail, wide), lambda b, i, s: (prev_tail(b, i, s), COL_X // wide)),
            pl.BlockSpec((3, rows, LANES), lambda b, i, s: (0, i, 0)),
            pl.BlockSpec((3, blk, LANES), lambda b, i, s: (0, jnp.maximum(i * qb - 1, 0), 0)),
            pl.BlockSpec((CONV_K, CONV_WIDTH), lambda b, i, s: (0, 0)),
            pl.BlockSpec((1, ATTN_WIDTH), lambda b, i, s: (0, 0)),
            pl.BlockSpec((1, CONV_WIDTH), lambda b, i, s: (0, 0)),
        ],
        out_specs=[
            pl.BlockSpec((rows, D_MODEL), lambda b, i, s: (row(b, i, s), 0)),
            pl.BlockSpec((1, blk, KV_WIDTH), lambda b, i, s: (b, 0, 0)),
            pl.BlockSpec((1, blk, KV_WIDTH), lambda b, i, s: (b, 0, 0)),
            pl.BlockSpec((1, CONV_K - 1, CONV_WIDTH), lambda b, i, s: (b, 0, 0)),
        ],
    )
    return pl.pallas_call(
        functools.partial(_prompt_mixer_kernel, n_steps=nb),
        out_shape=[
            jax.ShapeDtypeStruct((n_batch * seq, D_MODEL), BF16),
            jax.ShapeDtypeStruct((n_batch, blk, KV_WIDTH), F32),
            jax.ShapeDtypeStruct((n_batch, blk, KV_WIDTH), F32),
            jax.ShapeDtypeStruct((n_batch, CONV_K - 1, CONV_WIDTH), F32),
        ],
        grid_spec=grid_spec,
        compiler_params=pltpu.CompilerParams(
            dimension_semantics=("arbitrary", "arbitrary"), vmem_limit_bytes=VMEM_LIMIT),
        name="prompt_mixer",
    )(sinks, proj, proj, proj, proj, proj, proj, proj, proj, rope_tab, rope_tab, conv_w, gn_attn, gn_conv)


def _expand_groups(t):
    ax = t.ndim - 1
    lane = lax.broadcasted_iota(jnp.int32, t.shape[:-1] + (LANES,), ax)
    chunks = []
    for c in range(KV_WIDTH // LANES):
        a = t[..., c * LANES:(c + 1) * LANES]
        r = pltpu.roll(a, HEAD_DIM, ax)
        lo = jnp.where(lane < HEAD_DIM, a, r)
        hi = jnp.where(lane < HEAD_DIM, r, a)
        chunks += [lo] * (GQA_GROUP // 2) + [hi] * (GQA_GROUP // 2)
    return jnp.concatenate(chunks, axis=-1)


def _sample_mixer_kernel(sink_ref, b_ref, c_ref, x_ref, q_ref, kv_ref, ck_ref, cv_ref, st_ref,
                         rope_ref, seg_ref, segt_ref, convw_ref, gna_ref, gnc_ref,
                         mixed_ref, kwin_ref, vwin_ref, cstate_ref):
    nb, wb = ck_ref.shape[0], ck_ref.shape[1]
    cos_t, sin_lo, sin_hi = rope_ref[0:1, :], rope_ref[1:2, :], rope_ref[2:3, :]
    qr = _rope_wide(q_ref[...], cos_t, sin_lo, sin_hi) * (HEAD_DIM ** -0.5)
    k_new = _rope_wide(kv_ref[:, :KV_WIDTH], cos_t, sin_lo, sin_hi)
    v_new = kv_ref[:, KV_WIDTH:]
    ck = ck_ref[...]
    cv = cv_ref[...]

    seg = seg_ref[...]
    prod = _expand_groups(ck) * qr[:, None, :]
    s_old = jnp.dot(prod.reshape(nb * wb, ATTN_WIDTH).astype(BF16), seg,
                    preferred_element_type=F32).reshape(nb, wb, N_HEADS)
    s_new = jnp.dot((_expand_groups(k_new) * qr).astype(BF16), seg, preferred_element_type=F32)
    sink = sink_ref[...]
    m = jnp.maximum(jnp.maximum(jnp.max(s_old, axis=1), s_new), sink)
    e_old = jnp.exp(s_old - m[:, None, :])
    e_new = jnp.exp(s_new - m)
    inv = 1.0 / (jnp.sum(e_old, axis=1) + e_new + jnp.exp(sink - m))
    p_old = (e_old * inv[:, None, :]).astype(BF16)
    p_new = (e_new * inv).astype(BF16)
    segt = segt_ref[...]
    pe_old = jnp.dot(p_old.reshape(nb * wb, N_HEADS), segt, preferred_element_type=F32).reshape(nb, wb, ATTN_WIDTH)
    pe_new = jnp.dot(p_new, segt, preferred_element_type=F32)
    attn = jnp.sum(pe_old * _expand_groups(cv), axis=1) + pe_new * _expand_groups(v_new)
    mixed_ref[:, :ATTN_WIDTH] = _rms_norm(attn, gna_ref[...]).astype(BF16)

    u = c_ref[...] * x_ref[...]
    acc = u * convw_ref[CONV_K - 1:CONV_K, :]
    for j in range(CONV_K - 1):
        acc = acc + st_ref[j] * convw_ref[j:j + 1, :]
    mixed_ref[:, ATTN_WIDTH:] = _rms_norm(b_ref[...] * acc, gnc_ref[...]).astype(BF16)

    kwin_ref[:, 0:wb - 1, :] = ck_ref[:, 1:wb, :]
    vwin_ref[:, 0:wb - 1, :] = cv_ref[:, 1:wb, :]
    for n in range(nb):
        kwin_ref[n, wb - 1:wb, :] = k_new[n:n + 1, :]
        vwin_ref[n, wb - 1:wb, :] = v_new[n:n + 1, :]
    for j in range(CONV_K - 2):
        cstate_ref[j] = st_ref[j + 1]
    cstate_ref[CONV_K - 2] = u


def _sample_mixer(proj, row0, n_dec, sinks, cache_k, cache_v, state_conv, rope_row, conv_w, gn_attn, gn_conv):
    nb = SAMPLE_CHUNK
    wb = cache_k.shape[1]
    r0 = row0 // nb
    wide = CONV_WIDTH
    head_of_lane = np.arange(ATTN_WIDTH) // HEAD_DIM
    seg = jnp.asarray(head_of_lane[:, None] == np.arange(N_HEADS)[None, :], BF16)
    kvb = COL_KV // (2 * KV_WIDTH)
    return pl.pallas_call(
        _sample_mixer_kernel,
        out_shape=[
            jax.ShapeDtypeStruct((n_dec, D_MODEL), BF16),
            jax.ShapeDtypeStruct((n_dec, wb, KV_WIDTH), F32),
            jax.ShapeDtypeStruct((n_dec, wb, KV_WIDTH), F32),
            jax.ShapeDtypeStruct((CONV_K - 1, n_dec, CONV_WIDTH), F32),
        ],
        grid=(n_dec // nb,),
        in_specs=[
            pl.BlockSpec((1, N_HEADS), lambda i: (0, 0)),
            pl.BlockSpec((nb, wide), lambda i: (r0 + i, COL_B // wide)),
            pl.BlockSpec((nb, wide), lambda i: (r0 + i, COL_C // wide)),
            pl.BlockSpec((nb, wide), lambda i: (r0 + i, COL_X // wide)),
            pl.BlockSpec((nb, ATTN_WIDTH), lambda i: (r0 + i, COL_Q // ATTN_WIDTH)),
            pl.BlockSpec((nb, 2 * KV_WIDTH), lambda i: (r0 + i, kvb)),
            pl.BlockSpec((nb, wb, KV_WIDTH), lambda i: (i, 0, 0)),
            pl.BlockSpec((nb, wb, KV_WIDTH), lambda i: (i, 0, 0)),
            pl.BlockSpec((CONV_K - 1, nb, CONV_WIDTH), lambda i: (0, i, 0)),
            pl.BlockSpec((3, LANES), lambda i: (0, 0)),
            pl.BlockSpec((ATTN_WIDTH, N_HEADS), lambda i: (0, 0)),
            pl.BlockSpec((N_HEADS, ATTN_WIDTH), lambda i: (0, 0)),
            pl.BlockSpec((CONV_K, CONV_WIDTH), lambda i: (0, 0)),
            pl.BlockSpec((1, ATTN_WIDTH), lambda i: (0, 0)),
            pl.BlockSpec((1, CONV_WIDTH), lambda i: (0, 0)),
        ],
        out_specs=[
            pl.BlockSpec((nb, D_MODEL), lambda i: (i, 0)),
            pl.BlockSpec((nb, wb, KV_WIDTH), lambda i: (i, 0, 0)),
            pl.BlockSpec((nb, wb, KV_WIDTH), lambda i: (i, 0, 0)),
            pl.BlockSpec((CONV_K - 1, nb, CONV_WIDTH), lambda i: (0, i, 0)),
        ],
        compiler_params=pltpu.CompilerParams(
            dimension_semantics=("arbitrary",), vmem_limit_bytes=VMEM_LIMIT),
        name="sample_mixer",
    )(sinks.reshape(1, N_HEADS), proj, proj, proj, proj, proj, cache_k, cache_v, state_conv,
      rope_row, seg, seg.T, conv_w, gn_attn, gn_conv)


def _outproj_kernel(mixed_ref, x_ref, g0_ref, b0_ref, wout_ref, g1_ref, b1_ref, h1_ref, h1b_ref):
    h = _layer_norm(x_ref[...], g0_ref[...], b0_ref[...])
    mixed = jnp.dot(mixed_ref[...], wout_ref[...], preferred_element_type=F32)
    h1 = _layer_norm(DEEPNORM_ALPHA * h + mixed, g1_ref[...], b1_ref[...])
    h1_ref[...] = h1
    h1b_ref[...] = h1.astype(BF16)


def _outproj(mixed, x_all, g0, b0, w_out, g1, b1):
    t = x_all.shape[0]
    tm = ROW_TILE // 2
    const = lambda i: (0, 0)
    resident = functools.partial(pl.BlockSpec, index_map=const, pipeline_mode=pl.Buffered(1))
    return pl.pallas_call(
        _outproj_kernel,
        out_shape=[
            jax.ShapeDtypeStruct((t, D_MODEL), F32),
            jax.ShapeDtypeStruct((t, D_MODEL), BF16),
        ],
        grid=(t // tm,),
        in_specs=[
            pl.BlockSpec((tm, D_MODEL), lambda i: (i, 0)),
            pl.BlockSpec((tm, D_MODEL), lambda i: (i, 0)),
            pl.BlockSpec((1, D_MODEL), const),
            pl.BlockSpec((1, D_MODEL), const),
            resident((D_MODEL, D_MODEL)),
            pl.BlockSpec((1, D_MODEL), const),
            pl.BlockSpec((1, D_MODEL), const),
        ],
        out_specs=[
            pl.BlockSpec((tm, D_MODEL), lambda i: (i, 0)),
            pl.BlockSpec((tm, D_MODEL), lambda i: (i, 0)),
        ],
        compiler_params=pltpu.CompilerParams(
            dimension_semantics=("arbitrary",), vmem_limit_bytes=VMEM_LIMIT),
        name="outproj",
    )(mixed, x_all, g0, b0, w_out, g1, b1)


def _route_kernel(h_ref, wrt_ref, bias_ref, tri_ref, ones_ref, e_ref, gate_ref, rank_ref, cnt_ref):
    tm = h_ref.shape[0]
    neg = -jnp.inf
    big = float(N_EXPERTS)

    @pl.when(pl.program_id(0) == 0)
    def _():
        cnt_ref[...] = jnp.zeros_like(cnt_ref)

    logits = lax.dot_general(wrt_ref[...], h_ref[...], (((1,), (1,)), ((), ())), preferred_element_type=F32)
    scores = jax.nn.sigmoid(logits)
    biased = scores + bias_ref[...]

    per_group = N_EXPERTS // N_EXPERT_GROUPS
    rid = lax.broadcasted_iota(jnp.int32, (per_group, tm), 0).astype(F32)
    gscore = []
    for g in range(N_EXPERT_GROUPS):
        xg = biased[g * per_group:(g + 1) * per_group]
        m1 = jnp.max(xg, 0, keepdims=True)
        i1 = jnp.min(jnp.where(xg == m1, rid, big), 0, keepdims=True)
        m2 = jnp.max(jnp.where(rid == i1, neg, xg), 0, keepdims=True)
        gscore.append(m1 + m2)
    masked = []
    for g in range(N_EXPERT_GROUPS):
        beaten = jnp.zeros((1, tm), F32)
        for o in range(N_EXPERT_GROUPS):
            if o != g:
                wins = (gscore[o] > gscore[g]) | ((gscore[o] == gscore[g]) & (o < g))
                beaten = beaten + jnp.where(wins, 1.0, 0.0)
        masked.append(jnp.where(beaten < TOPK_GROUPS, biased[g * per_group:(g + 1) * per_group], neg))
    x = jnp.concatenate(masked, axis=0)

    eid = lax.broadcasted_iota(jnp.int32, (N_EXPERTS, tm), 0).astype(F32)
    sel = jnp.zeros((N_EXPERTS, tm), F32)
    picks, gates = [], []
    for _ in range(TOP_K):
        m = jnp.max(x, 0, keepdims=True)
        ik = jnp.min(jnp.where(x == m, eid, big), 0, keepdims=True)
        hit = eid == ik
        gates.append(jnp.sum(jnp.where(hit, scores, 0.0), 0, keepdims=True))
        picks.append(ik)
        x = jnp.where(hit, neg, x)
        sel = jnp.where(hit, 1.0, sel)

    selb = sel.astype(BF16)
    pos = cnt_ref[...] + jnp.dot(selb, tri_ref[...], preferred_element_type=F32)
    cnt_ref[...] = cnt_ref[...] + jnp.dot(selb, ones_ref[...], preferred_element_type=F32)

    gsum = gates[0]
    for k in range(1, TOP_K):
        gsum = gsum + gates[k]
    for k in range(TOP_K):
        e_ref[k:k + 1, :] = picks[k].astype(jnp.int32)
        gate_ref[k:k + 1, :] = gates[k] / gsum * ROUTED_SCALE
        rank_ref[k:k + 1, :] = jnp.sum(jnp.where(eid == picks[k], pos, 0.0), 0, keepdims=True).astype(jnp.int32)


def _route(h1b, w_router_t, bias_col):
    t = h1b.shape[0]
    tm = LANES
    tri = jnp.asarray(np.arange(tm)[:, None] < np.arange(tm)[None, :], BF16)
    ones = jnp.ones((tm, tm), BF16)
    const = lambda i: (0, 0)
    return pl.pallas_call(
        _route_kernel,
        out_shape=[
            jax.ShapeDtypeStruct((TOP_K, t), jnp.int32),
            jax.ShapeDtypeStruct((TOP_K, t), F32),
            jax.ShapeDtypeStruct((TOP_K, t), jnp.int32),
            jax.ShapeDtypeStruct((N_EXPERTS, tm), F32),
        ],
        grid=(t // tm,),
        in_specs=[
            pl.BlockSpec((tm, D_MODEL), lambda i: (i, 0)),
            pl.BlockSpec((N_EXPERTS, D_MODEL), const),
            pl.BlockSpec((N_EXPERTS, 1), const),
            pl.BlockSpec((tm, tm), const),
            pl.BlockSpec((tm, tm), const),
        ],
        out_specs=[
            pl.BlockSpec((TOP_K, tm), lambda i: (0, i)),
            pl.BlockSpec((TOP_K, tm), lambda i: (0, i)),
            pl.BlockSpec((TOP_K, tm), lambda i: (0, i)),
            pl.BlockSpec((N_EXPERTS, tm), const),
        ],
        compiler_params=pltpu.CompilerParams(
            dimension_semantics=("arbitrary",), vmem_limit_bytes=VMEM_LIMIT),
        name="route",
    )(h1b, w_router_t, bias_col, tri, ones)


def _dest_kernel(pstart_ref, e_ref, rank_ref, dest_ref):
    e = e_ref[...]

    def body(x, acc):
        return jnp.where(e == x, pstart_ref[x], acc)
    dest_ref[...] = lax.fori_loop(0, N_EXPERTS, body, jnp.zeros_like(e)) + rank_ref[...]


def _dest_rows(pad_start, top_e, rank):
    whole = pl.BlockSpec(top_e.shape, lambda i, ps: (0, 0))
    return pl.pallas_call(
        _dest_kernel,
        out_shape=jax.ShapeDtypeStruct(top_e.shape, jnp.int32),
        grid_spec=pltpu.PrefetchScalarGridSpec(num_scalar_prefetch=1, grid=(1,), in_specs=[whole, whole],
                                               out_specs=whole),
        name="dest_rows",
    )(pad_start, top_e, rank)


def _pow2_chunks(n_max):
    return [1 << b for b in reversed(range(int(n_max).bit_length()))]


def _dispatch_kernel(fill0_ref, filln_ref, nused_ref, dest_ref, h1_ref,
                     xs_hbm, rows, zeros_vmem, row_sem, fill_sem, *, n_blocks):
    i = pl.program_id(0)
    tm = h1_ref.shape[0]
    bm = EXPERT_ROWS

    def fill_copies(e):
        n = filln_ref[e]
        start = fill0_ref[e]
        out = []
        for c in _pow2_chunks(bm - 1):
            off = start + (n // (2 * c)) * (2 * c)
            out.append(((n // c) % 2 == 1,
                        pltpu.make_async_copy(zeros_vmem.at[pl.ds(0, c)], xs_hbm.at[pl.ds(off, c)], fill_sem)))
        return out

    def tail_copy(j):
        return pltpu.make_async_copy(zeros_vmem, xs_hbm.at[pl.ds(j * bm, bm)], fill_sem)

    @pl.when(i == 0)
    def _():
        zeros_vmem[...] = jnp.zeros_like(zeros_vmem)

        def start_fill(e, c):
            for pred, cp in fill_copies(e):
                @pl.when(pred)
                def _():
                    cp.start(priority=1)
            return c
        lax.fori_loop(0, N_EXPERTS, start_fill, 0)

        def start_tail(j, c):
            tail_copy(j).start(priority=1)
            return c
        lax.fori_loop(nused_ref[0], n_blocks, start_tail, 0)

    for j in range(rows.shape[1]):
        rows[:, j, :] = h1_ref[:, j * LANES:(j + 1) * LANES]

    for t in range(tm):
        for k in range(TOP_K):
            pltpu.make_async_copy(rows.at[t], xs_hbm.at[dest_ref[k * tm + t]], row_sem).start(priority=k % 2)
    for _ in range(TOP_K):
        pltpu.make_async_copy(rows, xs_hbm.at[pl.ds(0, tm)], row_sem).wait()

    @pl.when(i == pl.num_programs(0) - 1)
    def _():
        def wait_fill(e, c):
            for pred, cp in fill_copies(e):
                @pl.when(pred)
                def _():
                    cp.wait()
            return c
        lax.fori_loop(0, N_EXPERTS, wait_fill, 0)

        def wait_tail(j, c):
            tail_copy(j).wait()
            return c
        lax.fori_loop(nused_ref[0], n_blocks, wait_tail, 0)


def _tile_major(a, tm):
    k, t = a.shape
    return a.reshape(k, t // tm, tm).swapaxes(0, 1).reshape(-1)


def _dispatch_rows(fill_start, fill_len, n_used, dest, h1, n_blocks):
    t, width = h1.shape
    tm = LANES
    row_shape = (width // LANES, LANES)
    grid_spec = pltpu.PrefetchScalarGridSpec(
        num_scalar_prefetch=3,
        grid=(t // tm,),
        in_specs=[pl.BlockSpec((TOP_K * tm,), lambda i, *_: (i,), memory_space=pltpu.SMEM),
                  pl.BlockSpec((tm, width), lambda i, *_: (i, 0))],
        out_specs=pl.BlockSpec(memory_space=pl.ANY),
        scratch_shapes=[pltpu.VMEM((tm,) + row_shape, h1.dtype), pltpu.VMEM((EXPERT_ROWS,) + row_shape, h1.dtype),
                        pltpu.SemaphoreType.DMA, pltpu.SemaphoreType.DMA],
    )
    return pl.pallas_call(
        functools.partial(_dispatch_kernel, n_blocks=n_blocks),
        out_shape=jax.ShapeDtypeStruct((n_blocks * EXPERT_ROWS,) + row_shape, h1.dtype),
        grid_spec=grid_spec,
        compiler_params=pltpu.CompilerParams(dimension_semantics=("arbitrary",)),
        name="dispatch",
    )(fill_start, fill_len, n_used, dest, h1)


def _experts_kernel(be_ref, first_ref, slot_ref, ahead_ref, head_ref, nused_ref, x_ref, wg_hbm, wu_hbm, wd_hbm,
                    y_ref, wg_buf, wu_buf, wd_buf, wg_bf, wu_bf, wd_bf, sems):
    i = pl.program_id(0)
    n_slots = wg_buf.shape[0]

    def weight_copies(e, slot):
        return (pltpu.make_async_copy(wg_hbm.at[e], wg_buf.at[slot], sems.at[slot, 0]),
                pltpu.make_async_copy(wu_hbm.at[e], wu_buf.at[slot], sems.at[slot, 1]),
                pltpu.make_async_copy(wd_hbm.at[e], wd_buf.at[slot], sems.at[slot, 2]))

    @pl.when(i == 0)
    def _():
        for r in range(n_slots - 1):
            @pl.when(head_ref[r] >= 0)
            def _():
                for cp in weight_copies(head_ref[r], r):
                    cp.start(priority=1)

    @pl.when(first_ref[i] == 1)
    def _():
        slot = slot_ref[i]
        for cp in weight_copies(be_ref[i], slot):
            cp.wait()

        @pl.when(ahead_ref[i] >= 0)
        def _():
            for cp in weight_copies(ahead_ref[i], (slot + n_slots - 1) % n_slots):
                cp.start(priority=1)

        wg_bf[...] = wg_buf[slot].astype(BF16)
        wu_bf[...] = wu_buf[slot].astype(BF16)
        wd_bf[...] = wd_buf[slot].astype(BF16)

    @pl.when(i < nused_ref[0])
    def _():
        x = jnp.concatenate([x_ref[:, j, :] for j in range(x_ref.shape[1])], axis=1).astype(BF16)
        g = jnp.dot(x, wg_bf[...], preferred_element_type=F32)
        u = jnp.dot(x, wu_bf[...], preferred_element_type=F32)
        a = (g * jax.nn.sigmoid(g) * u).astype(BF16)
        y_ref[...] = jnp.dot(a, wd_bf[...], preferred_element_type=F32)

    @pl.when(i >= nused_ref[0])
    def _():
        y_ref[...] = jnp.zeros_like(y_ref)


def _experts(block_e, first, slot, ahead, head, n_used, xs, w_gate, w_up, w_down):
    p = xs.shape[0]
    bm = EXPERT_ROWS
    n_blocks = p // bm
    ns = WEIGHT_SLOTS

    def xrow(i, be, fi, sl, ah, hd, nu):
        return (jnp.minimum(i, nu[0] - 1), 0, 0)

    grid_spec = pltpu.PrefetchScalarGridSpec(
        num_scalar_prefetch=6,
        grid=(n_blocks,),
        in_specs=[
            pl.BlockSpec((bm,) + xs.shape[1:], xrow),
            pl.BlockSpec(memory_space=pl.ANY),
            pl.BlockSpec(memory_space=pl.ANY),
            pl.BlockSpec(memory_space=pl.ANY),
        ],
        out_specs=pl.BlockSpec((bm, D_MODEL), lambda i, *_: (i, 0)),
        scratch_shapes=[
            pltpu.VMEM((ns, D_MODEL, D_EXPERT), F32),
            pltpu.VMEM((ns, D_MODEL, D_EXPERT), F32),
            pltpu.VMEM((ns, D_EXPERT, D_MODEL), F32),
            pltpu.VMEM((D_MODEL, D_EXPERT), BF16),
            pltpu.VMEM((D_MODEL, D_EXPERT), BF16),
            pltpu.VMEM((D_EXPERT, D_MODEL), BF16),
            pltpu.SemaphoreType.DMA((ns, 3)),
        ],
    )
    return pl.pallas_call(
        _experts_kernel,
        out_shape=jax.ShapeDtypeStruct((p, D_MODEL), F32),
        grid_spec=grid_spec,
        compiler_params=pltpu.CompilerParams(
            dimension_semantics=("arbitrary",), vmem_limit_bytes=VMEM_LIMIT),
        name="experts",
    )(block_e, first, slot, ahead, head, n_used, xs, w_gate, w_up, w_down)


def _combine_kernel(dest_ref, gate_ref, yb_hbm, o_ref, buf, sems):
    i = pl.program_id(0)
    n_tiles = pl.num_programs(0) - 1
    tm = o_ref.shape[0]
    n_rows = TOP_K * tm

    for s in range(2):
        @pl.when((i < n_tiles) & (i % 2 == s))
        def _():
            for idx in range(n_rows):
                pltpu.make_async_copy(yb_hbm.at[pl.ds(dest_ref[idx], 1), :], buf.at[s, pl.ds(idx, 1), :],
                                      sems.at[s]).start(priority=idx % 2)

    @pl.when(i >= 1)
    def _():
        slot = (i - 1) % 2
        pltpu.make_async_copy(yb_hbm.at[pl.ds(0, n_rows), :], buf.at[slot], sems.at[slot]).wait()
        acc = gate_ref[:, 0:1] * buf[slot, 0:tm, :]
        for k in range(1, TOP_K):
            acc = acc + gate_ref[:, k:k + 1] * buf[slot, k * tm:(k + 1) * tm, :]
        o_ref[...] = acc


def _combine(dest, gate_tk, yb):
    t = gate_tk.shape[0]
    tm = LANES
    n_tiles = t // tm
    return pl.pallas_call(
        _combine_kernel,
        out_shape=jax.ShapeDtypeStruct((t, D_MODEL), F32),
        grid=(n_tiles + 1,),
        in_specs=[
            pl.BlockSpec((TOP_K * tm,), lambda i: (jnp.minimum(i, n_tiles - 1),), memory_space=pltpu.SMEM),
            pl.BlockSpec((tm, TOP_K), lambda i: (jnp.maximum(i - 1, 0), 0)),
            pl.BlockSpec(memory_space=pl.ANY),
        ],
        out_specs=pl.BlockSpec((tm, D_MODEL), lambda i: (jnp.maximum(i - 1, 0), 0)),
        scratch_shapes=[pltpu.VMEM((2, TOP_K * tm, D_MODEL), F32), pltpu.SemaphoreType.DMA((2,))],
        compiler_params=pltpu.CompilerParams(
            dimension_semantics=("arbitrary",), vmem_limit_bytes=VMEM_LIMIT),
        name="combine",
    )(dest, gate_tk, yb)


def _final_kernel(h1_ref, h1b_ref, routed_ref, p_ref, wsg_ref, wsu_ref, wsd_ref, wpg_ref, wp_ref,
                  g2_ref, b2_ref, y_ref):
    hb = h1b_ref[...]
    g = jnp.dot(hb, wsg_ref[...], preferred_element_type=F32)
    u = jnp.dot(hb, wsu_ref[...], preferred_element_type=F32)
    shared = jnp.dot((g * jax.nn.sigmoid(g) * u).astype(BF16), wsd_ref[...], preferred_element_type=F32)
    gate = jax.nn.sigmoid(jnp.dot(hb, wpg_ref[...], preferred_element_type=F32))
    ple = gate * jnp.dot(p_ref[...], wp_ref[...], preferred_element_type=F32)
    r = DEEPNORM_ALPHA * h1_ref[...] + (routed_ref[...] + shared) + ple
    y_ref[...] = _layer_norm(r, g2_ref[...], b2_ref[...])


def _final(h1, h1b, routed, p_all, ws_gate, ws_up, ws_down, w_ple_gate, w_ple, g2, b2, *, row0, n_rows, tm):
    assert row0 % tm == 0 and n_rows % tm == 0
    blk0 = row0 // tm
    const = lambda i: (0, 0)
    rows = lambda i: (blk0 + i, 0)
    resident = functools.partial(pl.BlockSpec, index_map=const, pipeline_mode=pl.Buffered(1))
    return pl.pallas_call(
        _final_kernel,
        out_shape=jax.ShapeDtypeStruct((n_rows, D_MODEL), F32),
        grid=(n_rows // tm,),
        in_specs=[
            pl.BlockSpec((tm, D_MODEL), rows),
            pl.BlockSpec((tm, D_MODEL), rows),
            pl.BlockSpec((tm, D_MODEL), rows),
            pl.BlockSpec((tm, PLE_DIM), rows),
            resident((D_MODEL, D_SHARED)),
            resident((D_MODEL, D_SHARED)),
            resident((D_SHARED, D_MODEL)),
            resident((D_MODEL, D_MODEL)),
            resident((PLE_DIM, D_MODEL)),
            pl.BlockSpec((1, D_MODEL), const),
            pl.BlockSpec((1, D_MODEL), const),
        ],
        out_specs=pl.BlockSpec((tm, D_MODEL), lambda i: (i, 0)),
        compiler_params=pltpu.CompilerParams(
            dimension_semantics=("arbitrary",), vmem_limit_bytes=VMEM_LIMIT),
        name="final",
    )(h1, h1b, routed, p_all, ws_gate, ws_up, ws_down, w_ple_gate, w_ple, g2, b2)


def _rope_tables(pos):
    half = ROT_DIM // 2
    inv_freq = ROPE_THETA ** (-jnp.arange(half, dtype=F32) * 2.0 / ROT_DIM)
    ang = pos.astype(F32)[:, None] * inv_freq[None, :]
    cos, sin = jnp.cos(ang), jnp.sin(ang)
    n = pos.shape[0]
    ones = jnp.ones((n, HEAD_DIM - ROT_DIM), F32)
    zeros = jnp.zeros((n, HEAD_DIM - ROT_DIM), F32)
    zh = jnp.zeros((n, half), F32)
    cos_t = jnp.concatenate([cos, cos, ones], -1)
    sin_lo = jnp.concatenate([-sin, zh, zeros], -1)
    sin_hi = jnp.concatenate([zh, sin, zeros], -1)
    per_head = jnp.stack([cos_t, sin_lo, sin_hi])
    return jnp.concatenate([per_head] * (LANES // HEAD_DIM), -1)


def _block_plan(counts, n_blocks):
    bm = EXPERT_ROWS
    i32 = jnp.int32
    padded = (counts + bm - 1) // bm * bm
    pad_end = jnp.cumsum(padded)
    pad_start = pad_end - padded
    n_used = pad_end[-1] // bm
    blk = jnp.arange(n_blocks)
    valid = blk < n_used
    block_e = jnp.minimum(jnp.sum(pad_end[None, :] <= (blk * bm)[:, None], axis=1), N_EXPERTS - 1)
    block_e = jnp.where(valid, block_e, block_e[jnp.maximum(n_used - 1, 0)])
    prev_e = jnp.concatenate([jnp.full((1,), -1, block_e.dtype), block_e[:-1]])
    first = valid & (block_e != prev_e)
    run = jnp.cumsum(first) - 1
    ns = WEIGHT_SLOTS
    used = counts > 0
    n_run_slots = N_EXPERTS + ns
    run_expert = jnp.full((n_run_slots + 1,), -1, i32).at[
        jnp.where(used, jnp.cumsum(used) - 1, n_run_slots)].set(jnp.arange(N_EXPERTS, dtype=i32))
    run_expert = run_expert[:n_run_slots]
    ahead = jnp.where(first, run_expert[run + ns - 1], -1)
    head = run_expert[:ns - 1]
    return (pad_start.astype(i32), (pad_start + counts).astype(i32), (padded - counts).astype(i32),
            n_used.astype(i32).reshape(1), block_e.astype(i32), first.astype(i32), (run % ns).astype(i32),
            ahead.astype(i32), head.astype(i32))


def kernel(x_prompt, x_sample, p_prompt, p_sample, cache_k, cache_v, state_conv, ln_in_g, ln_in_b, w_in,
           attn_sinks, conv_w, gn_attn, gn_conv, w_out, ln1_g, ln1_b, w_router, router_bias, w_gate, w_up,
           w_down, ws_gate, ws_up, ws_down, w_ple_gate, w_ple, ln2_g, ln2_b):
    n_batch, seq, d = x_prompt.shape
    n_dec, dec_seq, _ = x_sample.shape
    depth = w_in.shape[0]
    wb = cache_k.shape[2]
    assert depth == 1 and dec_seq == 1 and d == D_MODEL
    assert wb == ATTN_BLOCK and seq % (ATTN_BLOCK * MIXER_BLOCKS_PER_STEP) == 0 and n_dec % SAMPLE_CHUNK == 0
    t_p = n_batch * seq
    t = t_p + n_dec
    assert t % ROW_TILE == 0 and t_p % SAMPLE_CHUNK == 0

    x_all = jnp.concatenate([x_prompt.reshape(t_p, d), x_sample.reshape(n_dec, d)], 0)
    p_all = jnp.concatenate([p_prompt[0].reshape(t_p, PLE_DIM), p_sample[0].reshape(n_dec, PLE_DIM)], 0).astype(BF16)
    row = lambda v: v.reshape(1, -1).astype(F32)

    w_in0 = w_in[0]
    qkv_w = ATTN_WIDTH + 2 * KV_WIDTH
    w_in_perm = jnp.concatenate([w_in0[:, qkv_w:], w_in0[:, :qkv_w]], axis=1).astype(BF16)

    proj = _ln_inproj(x_all, row(ln_in_g), row(ln_in_b), w_in_perm)

    rope_p = _rope_tables(jnp.arange(seq))
    rope_s = _rope_tables(PAST_LEN + jnp.arange(dec_seq))[:, 0, :]
    sinks = attn_sinks[0].astype(F32)
    cw, ga, gc = conv_w[0].astype(F32), row(gn_attn[0]), row(gn_conv[0])
    mixed_p, kwin_p, vwin_p, cstate_p = _prompt_mixer(proj, sinks, rope_p, cw, ga, gc, n_batch, seq)
    mixed_s, kwin_s, vwin_s, cstate_s = _sample_mixer(
        proj, t_p, n_dec, sinks, cache_k[0].reshape(n_dec, wb, KV_WIDTH), cache_v[0].reshape(n_dec, wb, KV_WIDTH),
        jnp.swapaxes(state_conv[0], 0, 1), rope_s, cw, ga, gc)
    cstate_s = jnp.swapaxes(cstate_s, 0, 1)
    mixed = jnp.concatenate([mixed_p, mixed_s], 0)

    h1, h1b = _outproj(
        mixed, x_all, row(ln_in_g), row(ln_in_b), w_out[0].astype(BF16), row(ln1_g[0]), row(ln1_b[0]))

    top_e, gate, rank, cnt = _route(h1b, w_router[0].T.astype(BF16), router_bias[0].astype(F32).reshape(N_EXPERTS, 1))
    n_blocks = t * TOP_K // EXPERT_ROWS + N_EXPERTS
    pad_start, fill_start, fill_len, n_used, block_e, first, slot, ahead, head = _block_plan(
        cnt[:, 0].astype(jnp.int32), n_blocks)
    dest = _tile_major(_dest_rows(pad_start, top_e, rank), LANES)
    xs = _dispatch_rows(fill_start, fill_len, n_used, dest, h1, n_blocks)
    yb = _experts(block_e, first, slot, ahead, head, n_used, xs, w_gate[0], w_up[0], w_down[0])
    routed = _combine(dest, gate.T, yb)

    final_args = (h1, h1b, routed, p_all, ws_gate[0].astype(BF16), ws_up[0].astype(BF16), ws_down[0].astype(BF16),
                  w_ple_gate[0].astype(BF16), w_ple[0].astype(BF16), row(ln2_g[0]), row(ln2_b[0]))
    y_p = _final(*final_args, row0=0, n_rows=t_p, tm=FINAL_TILE)
    y_s = _final(*final_args, row0=t_p, n_rows=n_dec, tm=n_dec)

    kv_shape = (1, -1, wb, N_KV_HEADS, HEAD_DIM)
    return (y_p.reshape(n_batch, seq, d), y_s.reshape(n_dec, dec_seq, d),
            kwin_p.reshape(kv_shape), vwin_p.reshape(kv_shape), cstate_p[None],
            kwin_s.reshape(kv_shape), vwin_s.reshape(kv_shape), cstate_s[None])
```

```python
import functools

import numpy as np
import jax
import jax.numpy as jnp
from jax import lax
from jax.experimental import pallas as pl
from jax.experimental.pallas import tpu as pltpu

D_MODEL = 2048
N_HEADS = 16
N_KV_HEADS = 4
HEAD_DIM = 64
GQA_GROUP = N_HEADS // N_KV_HEADS
ATTN_WIDTH = N_HEADS * HEAD_DIM
KV_WIDTH = N_KV_HEADS * HEAD_DIM
WINDOW = 128
ATTN_BLOCK = WINDOW
ROT_DIM = HEAD_DIM // 4
ROPE_THETA = 500000.0
CONV_WIDTH = D_MODEL - ATTN_WIDTH
CONV_K = 3
IN_COLS = ATTN_WIDTH + 2 * KV_WIDTH + 3 * CONV_WIDTH
N_EXPERTS = 256
TOP_K = 8
N_EXPERT_GROUPS = 8
TOPK_GROUPS = 4
D_EXPERT = 512
D_SHARED = 512
ROUTED_SCALE = 2.5
PLE_DIM = 256
LN_EPS = 1e-5
DEPTH = 1
PAST_LEN = 8192
DEEPNORM_ALPHA = (2 * DEPTH) ** 0.25

LANES = 128
SUBLANES = 8
COL_B, COL_C, COL_X, COL_Q, COL_KV = 0, CONV_WIDTH, 2 * CONV_WIDTH, 3 * CONV_WIDTH, 3 * CONV_WIDTH + ATTN_WIDTH

ROW_TILE = 640
INPROJ_COL_TILE = 1536
FINAL_TILE = 256
EXPERT_ROWS = 128
WEIGHT_SLOTS = 3
SAMPLE_CHUNK = 8
MIXER_BLOCKS_PER_STEP = 2
VMEM_LIMIT = 56 * 1024 * 1024

BF16 = jnp.bfloat16
F32 = jnp.float32


def _layer_norm(x, g, b):
    xc = x - jnp.mean(x, -1, keepdims=True)
    var = jnp.mean(xc * xc, -1, keepdims=True)
    return xc * lax.rsqrt(var + LN_EPS) * g + b


def _rms_norm(x, g):
    return x * lax.rsqrt(jnp.mean(x * x, -1, keepdims=True) + LN_EPS) * g


def _rope_lanes(x, cos_t, sin_lo, sin_hi):
    half = ROT_DIM // 2
    ax = x.ndim - 1
    return x * cos_t + pltpu.roll(x, LANES - half, ax) * sin_lo + pltpu.roll(x, half, ax) * sin_hi


def _rope_wide(x, cos_t, sin_lo, sin_hi):
    n = x.shape[-1] // LANES
    return jnp.concatenate(
        [_rope_lanes(x[..., c * LANES:(c + 1) * LANES], cos_t, sin_lo, sin_hi) for c in range(n)], axis=-1)


def _ln_inproj_kernel(x_ref, g_ref, b_ref, w_ref, o_ref, h_scr):
    @pl.when(pl.program_id(1) == 0)
    def _():
        h_scr[...] = _layer_norm(x_ref[...], g_ref[...], b_ref[...]).astype(BF16)

    o_ref[...] = jnp.dot(h_scr[...], w_ref[...], preferred_element_type=F32)


def _ln_inproj(x_all, g, b, w_bf16):
    t = x_all.shape[0]
    return pl.pallas_call(
        _ln_inproj_kernel,
        out_shape=jax.ShapeDtypeStruct((t, IN_COLS), F32),
        grid=(t // ROW_TILE, IN_COLS // INPROJ_COL_TILE),
        in_specs=[
            pl.BlockSpec((ROW_TILE, D_MODEL), lambda i, j: (i, 0)),
            pl.BlockSpec((1, D_MODEL), lambda i, j: (0, 0)),
            pl.BlockSpec((1, D_MODEL), lambda i, j: (0, 0)),
            pl.BlockSpec((D_MODEL, INPROJ_COL_TILE), lambda i, j: (0, j)),
        ],
        out_specs=pl.BlockSpec((ROW_TILE, INPROJ_COL_TILE), lambda i, j: (i, j)),
        scratch_shapes=[pltpu.VMEM((ROW_TILE, D_MODEL), BF16)],
        compiler_params=pltpu.CompilerParams(
            dimension_semantics=("arbitrary", "arbitrary"), vmem_limit_bytes=VMEM_LIMIT),
        name="ln_inproj",
    )(x_all, g, b, w_bf16)


def _prompt_mixer_kernel(sink_ref, b_ref, c_ref, x_ref, q_ref, kv_ref, kvp_ref, cp_ref, xp_ref,
                         rope_ref, ropep_ref, convw_ref, gna_ref, gnc_ref,
                         mixed_ref, kwin_ref, vwin_ref, cstate_ref, *, n_steps):
    i = pl.program_id(1)
    blk = ATTN_BLOCK
    rows = q_ref.shape[0]
    cos_t, sin_lo, sin_hi = rope_ref[0], rope_ref[1], rope_ref[2]
    qr = _rope_wide(q_ref[...], cos_t, sin_lo, sin_hi) * (HEAD_DIM ** -0.5)
    k_cur = _rope_wide(kv_ref[:, :KV_WIDTH], cos_t, sin_lo, sin_hi)
    k_prev = _rope_wide(kvp_ref[:, :KV_WIDTH], ropep_ref[0], ropep_ref[1], ropep_ref[2])
    v_cur = kv_ref[:, KV_WIDTH:]
    v_prev = kvp_ref[:, KV_WIDTH:]
    k_ext = jnp.concatenate([k_prev, k_cur], axis=0).astype(BF16)
    v_ext = jnp.concatenate([v_prev, v_cur], axis=0).astype(BF16)

    qi = lax.broadcasted_iota(jnp.int32, (blk, 2 * blk), 0)
    sj = lax.broadcasted_iota(jnp.int32, (blk, 2 * blk), 1)
    rel = qi + blk - sj
    local = (rel >= 0) & (rel <= WINDOW)
    for sb in range(rows // blk):
        mask = local if sb > 0 else local & ((sj >= blk) | (i > 0))
        mask = jnp.concatenate([mask] * GQA_GROUP, axis=0)
        q_sb = qr[sb * blk:(sb + 1) * blk]
        k_all = k_ext[sb * blk:(sb + 2) * blk]
        v_all = v_ext[sb * blk:(sb + 2) * blk]
        heads = []
        for g in range(N_KV_HEADS):
            qg = jnp.concatenate(
                [q_sb[:, (g * GQA_GROUP + hh) * HEAD_DIM:(g * GQA_GROUP + hh + 1) * HEAD_DIM]
                 for hh in range(GQA_GROUP)], axis=0).astype(BF16)
            kg = k_all[:, g * HEAD_DIM:(g + 1) * HEAD_DIM]
            vg = v_all[:, g * HEAD_DIM:(g + 1) * HEAD_DIM]
            s = lax.dot_general(qg, kg, (((1,), (1,)), ((), ())), preferred_element_type=F32)
            s = jnp.where(mask, s, -jnp.inf)
            sink = jnp.concatenate(
                [jnp.full((blk, 1), sink_ref[g * GQA_GROUP + hh], F32) for hh in range(GQA_GROUP)], axis=0)
            m = jnp.maximum(jnp.max(s, -1, keepdims=True), sink)
            e = jnp.exp(s - m)
            p = e / (jnp.sum(e, -1, keepdims=True) + jnp.exp(sink - m))
            og = jnp.dot(p.astype(BF16), vg, preferred_element_type=F32)
            heads += [og[hh * blk:(hh + 1) * blk] for hh in range(GQA_GROUP)]
        attn = jnp.concatenate(heads, axis=-1)
        mixed_ref[sb * blk:(sb + 1) * blk, :ATTN_WIDTH] = _rms_norm(attn, gna_ref[...]).astype(BF16)

    u = c_ref[...] * x_ref[...]
    n_tail = CONV_K - 1
    tail_rows = cp_ref.shape[0]
    row_id = lax.broadcasted_iota(jnp.int32, u.shape, 0)
    acc = u * convw_ref[CONV_K - 1:CONV_K, :]
    for d in range(1, CONV_K):
        ud = pltpu.roll(u, d, 0)
        for r in range(d):
            src = tail_rows - d + r
            up = jnp.where(i > 0, cp_ref[src:src + 1, :] * xp_ref[src:src + 1, :], 0.0)
            ud = jnp.where(row_id == r, up, ud)
        acc = acc + ud * convw_ref[CONV_K - 1 - d:CONV_K - d, :]
    mixed_ref[:, ATTN_WIDTH:] = _rms_norm(b_ref[...] * acc, gnc_ref[...]).astype(BF16)

    @pl.when(i == n_steps - 1)
    def _():
        kwin_ref[0] = k_cur[rows - blk:, :]
        vwin_ref[0] = v_cur[rows - blk:, :]
        cstate_ref[0] = u[rows - n_tail:, :]


def _prompt_mixer(proj, sinks, rope_tab, conv_w, gn_attn, gn_conv, n_batch, seq):
    blk = ATTN_BLOCK
    qb = MIXER_BLOCKS_PER_STEP
    rows = qb * blk
    nb = seq // rows
    tail = SUBLANES
    wide = CONV_WIDTH
    kvb = COL_KV // (2 * KV_WIDTH)

    def row(b, i, s):
        return b * nb + i

    def prev_blk(b, i, s):
        return jnp.maximum((b * nb + i) * qb - 1, b * nb * qb)

    def prev_tail(b, i, s):
        return jnp.maximum((b * nb + i) * (rows // tail) - 1, 0)

    grid_spec = pltpu.PrefetchScalarGridSpec(
        num_scalar_prefetch=1,
        grid=(n_batch, nb),
        in_specs=[
            pl.BlockSpec((rows, wide), lambda b, i, s: (row(b, i, s), COL_B // wide)),
            pl.BlockSpec((rows, wide), lambda b, i, s: (row(b, i, s), COL_C // wide)),
            pl.BlockSpec((rows, wide), lambda b, i, s: (row(b, i, s), COL_X // wide)),
            pl.BlockSpec((rows, ATTN_WIDTH), lambda b, i, s: (row(b, i, s), COL_Q // ATTN_WIDTH)),
            pl.BlockSpec((rows, 2 * KV_WIDTH), lambda b, i, s: (row(b, i, s), kvb)),
            pl.BlockSpec((blk, 2 * KV_WIDTH), lambda b, i, s: (prev_blk(b, i, s), kvb)),
            pl.BlockSpec((tail, wide), lambda b, i, s: (prev_tail(b, i, s), COL_C // wide)),
            pl.BlockSpec((tail, wide), lambda b, i, s: (prev_tail(b, i, s), COL_X // wide)),
            pl.BlockSpec((3, rows, LANES), lambda b, i, s: (0, i, 0)),
            pl.BlockSpec((3, blk, LANES), lambda b, i, s: (0, jnp.maximum(i * qb - 1, 0), 0)),
            pl.BlockSpec((CONV_K, CONV_WIDTH), lambda b, i, s: (0, 0)),
            pl.BlockSpec((1, ATTN_WIDTH), lambda b, i, s: (0, 0)),
            pl.BlockSpec((1, CONV_WIDTH), lambda b, i, s: (0, 0)),
        ],
        out_specs=[
            pl.BlockSpec((rows, D_MODEL), lambda b, i, s: (row(b, i, s), 0)),
            pl.BlockSpec((1, blk, KV_WIDTH), lambda b, i, s: (b, 0, 0)),
            pl.BlockSpec((1, blk, KV_WIDTH), lambda b, i, s: (b, 0, 0)),
            pl.BlockSpec((1, CONV_K - 1, CONV_WIDTH), lambda b, i, s: (b, 0, 0)),
        ],
    )
    return pl.pallas_call(
        functools.partial(_prompt_mixer_kernel, n_steps=nb),
        out_shape=[
            jax.ShapeDtypeStruct((n_batch * seq, D_MODEL), BF16),
            jax.ShapeDtypeStruct((n_batch, blk, KV_WIDTH), F32),
            jax.ShapeDtypeStruct((n_batch, blk, KV_WIDTH), F32),
            jax.ShapeDtypeStruct((n_batch, CONV_K - 1, CONV_WIDTH), F32),
        ],
        grid_spec=grid_spec,
        compiler_params=pltpu.CompilerParams(
            dimension_semantics=("arbitrary", "arbitrary"), vmem_limit_bytes=VMEM_LIMIT),
        name="prompt_mixer",
    )(sinks, proj, proj, proj, proj, proj, proj, proj, proj, rope_tab, rope_tab, conv_w, gn_attn, gn_conv)


def _expand_groups(t):
    ax = t.ndim - 1
    lane = lax.broadcasted_iota(jnp.int32, t.shape[:-1] + (LANES,), ax)
    chunks = []
    for c in range(KV_WIDTH // LANES):
        a = t[..., c * LANES:(c + 1) * LANES]
        r = pltpu.roll(a, HEAD_DIM, ax)
        lo = jnp.where(lane < HEAD_DIM, a, r)
        hi = jnp.where(lane < HEAD_DIM, r, a)
        chunks += [lo] * (GQA_GROUP // 2) + [hi] * (GQA_GROUP // 2)
    return jnp.concatenate(chunks, axis=-1)


def _sample_mixer_kernel(sink_ref, b_ref, c_ref, x_ref, q_ref, kv_ref, ck_ref, cv_ref, st_ref,
                         rope_ref, seg_ref, segt_ref, convw_ref, gna_ref, gnc_ref,
                         mixed_ref, kwin_ref, vwin_ref, cstate_ref):
    nb, wb = ck_ref.shape[0], ck_ref.shape[1]
    cos_t, sin_lo, sin_hi = rope_ref[0:1, :], rope_ref[1:2, :], rope_ref[2:3, :]
    qr = _rope_wide(q_ref[...], cos_t, sin_lo, sin_hi) * (HEAD_DIM ** -0.5)
    k_new = _rope_wide(kv_ref[:, :KV_WIDTH], cos_t, sin_lo, sin_hi)
    v_new = kv_ref[:, KV_WIDTH:]
    ck = ck_ref[...]
    cv = cv_ref[...]

    seg = seg_ref[...]
    prod = _expand_groups(ck) * qr[:, None, :]
    s_old = jnp.dot(prod.reshape(nb * wb, ATTN_WIDTH).astype(BF16), seg,
                    preferred_element_type=F32).reshape(nb, wb, N_HEADS)
    s_new = jnp.dot((_expand_groups(k_new) * qr).astype(BF16), seg, preferred_element_type=F32)
    sink = sink_ref[...]
    m = jnp.maximum(jnp.maximum(jnp.max(s_old, axis=1), s_new), sink)
    e_old = jnp.exp(s_old - m[:, None, :])
    e_new = jnp.exp(s_new - m)
    inv = 1.0 / (jnp.sum(e_old, axis=1) + e_new + jnp.exp(sink - m))
    p_old = (e_old * inv[:, None, :]).astype(BF16)
    p_new = (e_new * inv).astype(BF16)
    segt = segt_ref[...]
    pe_old = jnp.dot(p_old.reshape(nb * wb, N_HEADS), segt, preferred_element_type=F32).reshape(nb, wb, ATTN_WIDTH)
    pe_new = jnp.dot(p_new, segt, preferred_element_type=F32)
    attn = jnp.sum(pe_old * _expand_groups(cv), axis=1) + pe_new * _expand_groups(v_new)
    mixed_ref[:, :ATTN_WIDTH] = _rms_norm(attn, gna_ref[...]).astype(BF16)

    u = c_ref[...] * x_ref[...]
    acc = u * convw_ref[CONV_K - 1:CONV_K, :]
    for j in range(CONV_K - 1):
        acc = acc + st_ref[j] * convw_ref[j:j + 1, :]
    mixed_ref[:, ATTN_WIDTH:] = _rms_norm(b_ref[...] * acc, gnc_ref[...]).astype(BF16)

    kwin_ref[:, 0:wb - 1, :] = ck_ref[:, 1:wb, :]
    vwin_ref[:, 0:wb - 1, :] = cv_ref[:, 1:wb, :]
    for n in range(nb):
        kwin_ref[n, wb - 1:wb, :] = k_new[n:n + 1, :]
        vwin_ref[n, wb - 1:wb, :] = v_new[n:n + 1, :]
    for j in range(CONV_K - 2):
        cstate_ref[j] = st_ref[j + 1]
    cstate_ref[CONV_K - 2] = u


def _sample_mixer(proj, row0, n_dec, sinks, cache_k, cache_v, state_conv, rope_row, conv_w, gn_attn, gn_conv):
    nb = SAMPLE_CHUNK
    wb = cache_k.shape[1]
    r0 = row0 // nb
    wide = CONV_WIDTH
    head_of_lane = np.arange(ATTN_WIDTH) // HEAD_DIM
    seg = jnp.asarray(head_of_lane[:, None] == np.arange(N_HEADS)[None, :], BF16)
    kvb = COL_KV // (2 * KV_WIDTH)
    return pl.pallas_call(
        _sample_mixer_kernel,
        out_shape=[
            jax.ShapeDtypeStruct((n_dec, D_MODEL), BF16),
            jax.ShapeDtypeStruct((n_dec, wb, KV_WIDTH), F32),
            jax.ShapeDtypeStruct((n_dec, wb, KV_WIDTH), F32),
            jax.ShapeDtypeStruct((CONV_K - 1, n_dec, CONV_WIDTH), F32),
        ],
        grid=(n_dec // nb,),
        in_specs=[
            pl.BlockSpec((1, N_HEADS), lambda i: (0, 0)),
            pl.BlockSpec((nb, wide), lambda i: (r0 + i, COL_B // wide)),
            pl.BlockSpec((nb, wide), lambda i: (r0 + i, COL_C // wide)),
            pl.BlockSpec((nb, wide), lambda i: (r0 + i, COL_X // wide)),
            pl.BlockSpec((nb, ATTN_WIDTH), lambda i: (r0 + i, COL_Q // ATTN_WIDTH)),
            pl.BlockSpec((nb, 2 * KV_WIDTH), lambda i: (r0 + i, kvb)),
            pl.BlockSpec((nb, wb, KV_WIDTH), lambda i: (i, 0, 0)),
            pl.BlockSpec((nb, wb, KV_WIDTH), lambda i: (i, 0, 0)),
            pl.BlockSpec((CONV_K - 1, nb, CONV_WIDTH), lambda i: (0, i, 0)),
            pl.BlockSpec((3, LANES), lambda i: (0, 0)),
            pl.BlockSpec((ATTN_WIDTH, N_HEADS), lambda i: (0, 0)),
            pl.BlockSpec((N_HEADS, ATTN_WIDTH), lambda i: (0, 0)),
            pl.BlockSpec((CONV_K, CONV_WIDTH), lambda i: (0, 0)),
            pl.BlockSpec((1, ATTN_WIDTH), lambda i: (0, 0)),
            pl.BlockSpec((1, CONV_WIDTH), lambda i: (0, 0)),
        ],
        out_specs=[
            pl.BlockSpec((nb, D_MODEL), lambda i: (i, 0)),
            pl.BlockSpec((nb, wb, KV_WIDTH), lambda i: (i, 0, 0)),
            pl.BlockSpec((nb, wb, KV_WIDTH), lambda i: (i, 0, 0)),
            pl.BlockSpec((CONV_K - 1, nb, CONV_WIDTH), lambda i: (0, i, 0)),
        ],
        compiler_params=pltpu.CompilerParams(
            dimension_semantics=("arbitrary",), vmem_limit_bytes=VMEM_LIMIT),
        name="sample_mixer",
    )(sinks.reshape(1, N_HEADS), proj, proj, proj, proj, proj, cache_k, cache_v, state_conv,
      rope_row, seg, seg.T, conv_w, gn_attn, gn_conv)


def _outproj_kernel(mixed_ref, x_ref, g0_ref, b0_ref, wout_ref, g1_ref, b1_ref, h1_ref, h1b_ref):
    h = _layer_norm(x_ref[...], g0_ref[...], b0_ref[...])
    mixed = jnp.dot(mixed_ref[...], wout_ref[...], preferred_element_type=F32)
    h1 = _layer_norm(DEEPNORM_ALPHA * h + mixed, g1_ref[...], b1_ref[...])
    h1_ref[...] = h1
    h1b_ref[...] = h1.astype(BF16)


def _outproj(mixed, x_all, g0, b0, w_out, g1, b1):
    t = x_all.shape[0]
    tm = ROW_TILE // 2
    const = lambda i: (0, 0)
    resident = functools.partial(pl.BlockSpec, index_map=const, pipeline_mode=pl.Buffered(1))
    return pl.pallas_call(
        _outproj_kernel,
        out_shape=[
            jax.ShapeDtypeStruct((t, D_MODEL), F32),
            jax.ShapeDtypeStruct((t, D_MODEL), BF16),
        ],
        grid=(t // tm,),
        in_specs=[
            pl.BlockSpec((tm, D_MODEL), lambda i: (i, 0)),
            pl.BlockSpec((tm, D_MODEL), lambda i: (i, 0)),
            pl.BlockSpec((1, D_MODEL), const),
            pl.BlockSpec((1, D_MODEL), const),
            resident((D_MODEL, D_MODEL)),
            pl.BlockSpec((1, D_MODEL), const),
            pl.BlockSpec((1, D_MODEL), const),
        ],
        out_specs=[
            pl.BlockSpec((tm, D_MODEL), lambda i: (i, 0)),
            pl.BlockSpec((tm, D_MODEL), lambda i: (i, 0)),
        ],
        compiler_params=pltpu.CompilerParams(
            dimension_semantics=("arbitrary",), vmem_limit_bytes=VMEM_LIMIT),
        name="outproj",
    )(mixed, x_all, g0, b0, w_out, g1, b1)


def _route_kernel(h_ref, wrt_ref, bias_ref, tri_ref, ones_ref, p_ref, wpg_ref, wp_ref,
                  e_ref, gate_ref, rank_ref, cnt_ref, ple_ref):
    tm = h_ref.shape[0]
    neg = -jnp.inf
    big = float(N_EXPERTS)
    ple_gate = jax.nn.sigmoid(jnp.dot(h_ref[...], wpg_ref[...], preferred_element_type=F32))
    ple_ref[...] = ple_gate * jnp.dot(p_ref[...], wp_ref[...], preferred_element_type=F32)

    @pl.when(pl.program_id(0) == 0)
    def _():
        cnt_ref[...] = jnp.zeros_like(cnt_ref)

    logits = lax.dot_general(wrt_ref[...], h_ref[...], (((1,), (1,)), ((), ())), preferred_element_type=F32)
    scores = jax.nn.sigmoid(logits)
    biased = scores + bias_ref[...]

    per_group = N_EXPERTS // N_EXPERT_GROUPS
    rid = lax.broadcasted_iota(jnp.int32, (per_group, tm), 0).astype(F32)
    gscore = []
    for g in range(N_EXPERT_GROUPS):
        xg = biased[g * per_group:(g + 1) * per_group]
        m1 = jnp.max(xg, 0, keepdims=True)
        i1 = jnp.min(jnp.where(xg == m1, rid, big), 0, keepdims=True)
        m2 = jnp.max(jnp.where(rid == i1, neg, xg), 0, keepdims=True)
        gscore.append(m1 + m2)
    masked = []
    for g in range(N_EXPERT_GROUPS):
        beaten = jnp.zeros((1, tm), F32)
        for o in range(N_EXPERT_GROUPS):
            if o != g:
                wins = (gscore[o] > gscore[g]) | ((gscore[o] == gscore[g]) & (o < g))
                beaten = beaten + jnp.where(wins, 1.0, 0.0)
        masked.append(jnp.where(beaten < TOPK_GROUPS, biased[g * per_group:(g + 1) * per_group], neg))
    x = jnp.concatenate(masked, axis=0)

    eid = lax.broadcasted_iota(jnp.int32, (N_EXPERTS, tm), 0).astype(F32)
    sel = jnp.zeros((N_EXPERTS, tm), F32)
    picks, gates = [], []
    for _ in range(TOP_K):
        m = jnp.max(x, 0, keepdims=True)
        ik = jnp.min(jnp.where(x == m, eid, big), 0, keepdims=True)
        hit = eid == ik
        gates.append(jnp.sum(jnp.where(hit, scores, 0.0), 0, keepdims=True))
        picks.append(ik)
        x = jnp.where(hit, neg, x)
        sel = jnp.where(hit, 1.0, sel)

    selb = sel.astype(BF16)
    pos = cnt_ref[...] + jnp.dot(selb, tri_ref[...], preferred_element_type=F32)
    cnt_ref[...] = cnt_ref[...] + jnp.dot(selb, ones_ref[...], preferred_element_type=F32)

    gsum = gates[0]
    for k in range(1, TOP_K):
        gsum = gsum + gates[k]
    for k in range(TOP_K):
        e_ref[k:k + 1, :] = picks[k].astype(jnp.int32)
        gate_ref[k:k + 1, :] = gates[k] / gsum * ROUTED_SCALE
        rank_ref[k:k + 1, :] = jnp.sum(jnp.where(eid == picks[k], pos, 0.0), 0, keepdims=True).astype(jnp.int32)


def _route(h1b, w_router_t, bias_col, p_all, w_ple_gate, w_ple):
    t = h1b.shape[0]
    tm = LANES
    tri = jnp.asarray(np.arange(tm)[:, None] < np.arange(tm)[None, :], BF16)
    ones = jnp.ones((tm, tm), BF16)
    const = lambda i: (0, 0)
    resident = functools.partial(pl.BlockSpec, index_map=const, pipeline_mode=pl.Buffered(1))
    return pl.pallas_call(
        _route_kernel,
        out_shape=[
            jax.ShapeDtypeStruct((TOP_K, t), jnp.int32),
            jax.ShapeDtypeStruct((TOP_K, t), F32),
            jax.ShapeDtypeStruct((TOP_K, t), jnp.int32),
            jax.ShapeDtypeStruct((N_EXPERTS, tm), F32),
            jax.ShapeDtypeStruct((t, D_MODEL), F32),
        ],
        grid=(t // tm,),
        in_specs=[
            pl.BlockSpec((tm, D_MODEL), lambda i: (i, 0)),
            pl.BlockSpec((N_EXPERTS, D_MODEL), const),
            pl.BlockSpec((N_EXPERTS, 1), const),
            pl.BlockSpec((tm, tm), const),
            pl.BlockSpec((tm, tm), const),
            pl.BlockSpec((tm, PLE_DIM), lambda i: (i, 0)),
            resident((D_MODEL, D_MODEL)),
            resident((PLE_DIM, D_MODEL)),
        ],
        out_specs=[
            pl.BlockSpec((TOP_K, tm), lambda i: (0, i)),
            pl.BlockSpec((TOP_K, tm), lambda i: (0, i)),
            pl.BlockSpec((TOP_K, tm), lambda i: (0, i)),
            pl.BlockSpec((N_EXPERTS, tm), const),
            pl.BlockSpec((tm, D_MODEL), lambda i: (i, 0)),
        ],
        compiler_params=pltpu.CompilerParams(
            dimension_semantics=("arbitrary",), vmem_limit_bytes=VMEM_LIMIT),
        name="route",
    )(h1b, w_router_t, bias_col, tri, ones, p_all, w_ple_gate, w_ple)


def _dest_kernel(pstart_ref, e_ref, rank_ref, dest_ref):
    e = e_ref[...]

    def body(x, acc):
        return jnp.where(e == x, pstart_ref[x], acc)
    dest_ref[...] = lax.fori_loop(0, N_EXPERTS, body, jnp.zeros_like(e)) + rank_ref[...]


def _dest_rows(pad_start, top_e, rank):
    whole = pl.BlockSpec(top_e.shape, lambda i, ps: (0, 0))
    return pl.pallas_call(
        _dest_kernel,
        out_shape=jax.ShapeDtypeStruct(top_e.shape, jnp.int32),
        grid_spec=pltpu.PrefetchScalarGridSpec(num_scalar_prefetch=1, grid=(1,), in_specs=[whole, whole],
                                               out_specs=whole),
        name="dest_rows",
    )(pad_start, top_e, rank)


def _pow2_chunks(n_max):
    return [1 << b for b in reversed(range(int(n_max).bit_length()))]


def _dispatch_kernel(fill0_ref, filln_ref, nused_ref, dest_ref, h1_ref,
                     xs_hbm, zeros_vmem, row_sem, fill_sem, *, n_blocks):
    i = pl.program_id(0)
    tm = h1_ref.shape[0]
    bm = EXPERT_ROWS

    def fill_copies(e):
        n = filln_ref[e]
        start = fill0_ref[e]
        head = jnp.minimum((SUBLANES - start % SUBLANES) % SUBLANES, n)
        out = []
        for r in range(SUBLANES - 1):
            out.append((r < head, pltpu.make_async_copy(
                zeros_vmem.at[pl.ds(0, 1), :], xs_hbm.at[pl.ds(start + r, 1), :], fill_sem)))
        body0 = start + head
        m = n - head
        for c in _pow2_chunks(bm - 1):
            if c < SUBLANES:
                continue
            off = pl.multiple_of(body0 + (m // (2 * c)) * (2 * c), SUBLANES)
            out.append(((m // c) % 2 == 1, pltpu.make_async_copy(
                zeros_vmem.at[pl.ds(0, c), :], xs_hbm.at[pl.ds(off, c), :], fill_sem)))
        return out

    def tail_copy(j):
        return pltpu.make_async_copy(zeros_vmem, xs_hbm.at[pl.ds(pl.multiple_of(j * bm, bm), bm), :], fill_sem)

    @pl.when(i == 0)
    def _():
        zeros_vmem[...] = jnp.zeros_like(zeros_vmem)

        def start_fill(e, c):
            for pred, cp in fill_copies(e):
                @pl.when(pred)
                def _():
                    cp.start(priority=1)
            return c
        lax.fori_loop(0, N_EXPERTS, start_fill, 0)

        def start_tail(j, c):
            tail_copy(j).start(priority=1)
            return c
        lax.fori_loop(nused_ref[0], n_blocks, start_tail, 0)

    for t in range(tm):
        for k in range(TOP_K):
            pltpu.make_async_copy(h1_ref.at[pl.ds(t, 1), :], xs_hbm.at[pl.ds(dest_ref[k * tm + t], 1), :],
                                  row_sem).start(priority=k % 2)

    for _ in range(TOP_K):
        pltpu.make_async_copy(h1_ref, xs_hbm.at[pl.ds(0, tm), :], row_sem).wait()

    @pl.when(i == pl.num_programs(0) - 1)
    def _():
        def wait_fill(e, c):
            for pred, cp in fill_copies(e):
                @pl.when(pred)
                def _():
                    cp.wait()
            return c
        lax.fori_loop(0, N_EXPERTS, wait_fill, 0)

        def wait_tail(j, c):
            tail_copy(j).wait()
            return c
        lax.fori_loop(nused_ref[0], n_blocks, wait_tail, 0)


def _tile_major(a, tm):
    k, t = a.shape
    return a.reshape(k, t // tm, tm).swapaxes(0, 1).reshape(-1)


def _dispatch_rows(fill_start, fill_len, n_used, dest, h1, n_blocks):
    t, width = h1.shape
    tm = LANES
    grid_spec = pltpu.PrefetchScalarGridSpec(
        num_scalar_prefetch=3,
        grid=(t // tm,),
        in_specs=[pl.BlockSpec((TOP_K * tm,), lambda i, *_: (i,), memory_space=pltpu.SMEM),
                  pl.BlockSpec((tm, width), lambda i, *_: (i, 0))],
        out_specs=pl.BlockSpec(memory_space=pl.ANY),
        scratch_shapes=[pltpu.VMEM((EXPERT_ROWS, width), h1.dtype),
                        pltpu.SemaphoreType.DMA, pltpu.SemaphoreType.DMA],
    )
    return pl.pallas_call(
        functools.partial(_dispatch_kernel, n_blocks=n_blocks),
        out_shape=jax.ShapeDtypeStruct((n_blocks * EXPERT_ROWS, width), h1.dtype),
        grid_spec=grid_spec,
        compiler_params=pltpu.CompilerParams(dimension_semantics=("arbitrary",)),
        name="dispatch",
    )(fill_start, fill_len, n_used, dest, h1)


def _experts_kernel(be_ref, first_ref, slot_ref, ahead_ref, head_ref, nused_ref, x_ref, wg_hbm, wu_hbm, wd_hbm,
                    y_ref, wg_buf, wu_buf, wd_buf, wg_bf, wu_bf, wd_bf, sems):
    i = pl.program_id(0)
    n_slots = wg_buf.shape[0]

    def weight_copies(e, slot):
        return (pltpu.make_async_copy(wg_hbm.at[e], wg_buf.at[slot], sems.at[slot, 0]),
                pltpu.make_async_copy(wu_hbm.at[e], wu_buf.at[slot], sems.at[slot, 1]),
                pltpu.make_async_copy(wd_hbm.at[e], wd_buf.at[slot], sems.at[slot, 2]))

    @pl.when(i == 0)
    def _():
        for r in range(n_slots - 1):
            @pl.when(head_ref[r] >= 0)
            def _():
                for cp in weight_copies(head_ref[r], r):
                    cp.start(priority=1)

    @pl.when(first_ref[i] == 1)
    def _():
        slot = slot_ref[i]
        for cp in weight_copies(be_ref[i], slot):
            cp.wait()

        @pl.when(ahead_ref[i] >= 0)
        def _():
            for cp in weight_copies(ahead_ref[i], (slot + n_slots - 1) % n_slots):
                cp.start(priority=1)

        wg_bf[...] = wg_buf[slot].astype(BF16)
        wu_bf[...] = wu_buf[slot].astype(BF16)
        wd_bf[...] = wd_buf[slot].astype(BF16)

    @pl.when(i < nused_ref[0])
    def _():
        x = x_ref[...].astype(BF16)
        g = jnp.dot(x, wg_bf[...], preferred_element_type=F32)
        u = jnp.dot(x, wu_bf[...], preferred_element_type=F32)
        a = (g * jax.nn.sigmoid(g) * u).astype(BF16)
        y_ref[...] = jnp.dot(a, wd_bf[...], preferred_element_type=F32)

    @pl.when(i >= nused_ref[0])
    def _():
        y_ref[...] = jnp.zeros_like(y_ref)


def _experts(block_e, first, slot, ahead, head, n_used, xs, w_gate, w_up, w_down):
    p = xs.shape[0]
    bm = EXPERT_ROWS
    n_blocks = p // bm
    ns = WEIGHT_SLOTS

    def xrow(i, be, fi, sl, ah, hd, nu):
        return (jnp.minimum(i, nu[0] - 1), 0)

    grid_spec = pltpu.PrefetchScalarGridSpec(
        num_scalar_prefetch=6,
        grid=(n_blocks,),
        in_specs=[
            pl.BlockSpec((bm,) + xs.shape[1:], xrow),
            pl.BlockSpec(memory_space=pl.ANY),
            pl.BlockSpec(memory_space=pl.ANY),
            pl.BlockSpec(memory_space=pl.ANY),
        ],
        out_specs=pl.BlockSpec((bm, D_MODEL), lambda i, *_: (i, 0)),
        scratch_shapes=[
            pltpu.VMEM((ns, D_MODEL, D_EXPERT), F32),
            pltpu.VMEM((ns, D_MODEL, D_EXPERT), F32),
            pltpu.VMEM((ns, D_EXPERT, D_MODEL), F32),
            pltpu.VMEM((D_MODEL, D_EXPERT), BF16),
            pltpu.VMEM((D_MODEL, D_EXPERT), BF16),
            pltpu.VMEM((D_EXPERT, D_MODEL), BF16),
            pltpu.SemaphoreType.DMA((ns, 3)),
        ],
    )
    return pl.pallas_call(
        _experts_kernel,
        out_shape=jax.ShapeDtypeStruct((p, D_MODEL), F32),
        grid_spec=grid_spec,
        compiler_params=pltpu.CompilerParams(
            dimension_semantics=("arbitrary",), vmem_limit_bytes=VMEM_LIMIT),
        name="experts",
    )(block_e, first, slot, ahead, head, n_used, xs, w_gate, w_up, w_down)


def _combine_kernel(dest_ref, gate_ref, yb_hbm, o_ref, buf, sems):
    i = pl.program_id(0)
    n_tiles = pl.num_programs(0) - 1
    tm = o_ref.shape[0]
    n_rows = TOP_K * tm

    for s in range(2):
        @pl.when((i < n_tiles) & (i % 2 == s))
        def _():
            for idx in range(n_rows):
                pltpu.make_async_copy(yb_hbm.at[pl.ds(dest_ref[idx], 1), :], buf.at[s, pl.ds(idx, 1), :],
                                      sems.at[s]).start(priority=idx % 2)

    @pl.when(i >= 1)
    def _():
        slot = (i - 1) % 2
        pltpu.make_async_copy(yb_hbm.at[pl.ds(0, n_rows), :], buf.at[slot], sems.at[slot]).wait()
        acc = gate_ref[:, 0:1] * buf[slot, 0:tm, :]
        for k in range(1, TOP_K):
            acc = acc + gate_ref[:, k:k + 1] * buf[slot, k * tm:(k + 1) * tm, :]
        o_ref[...] = acc


def _combine(dest, gate_tk, yb):
    t = gate_tk.shape[0]
    tm = LANES
    n_tiles = t // tm
    return pl.pallas_call(
        _combine_kernel,
        out_shape=jax.ShapeDtypeStruct((t, D_MODEL), F32),
        grid=(n_tiles + 1,),
        in_specs=[
            pl.BlockSpec((TOP_K * tm,), lambda i: (jnp.minimum(i, n_tiles - 1),), memory_space=pltpu.SMEM),
            pl.BlockSpec((tm, TOP_K), lambda i: (jnp.maximum(i - 1, 0), 0)),
            pl.BlockSpec(memory_space=pl.ANY),
        ],
        out_specs=pl.BlockSpec((tm, D_MODEL), lambda i: (jnp.maximum(i - 1, 0), 0)),
        scratch_shapes=[pltpu.VMEM((2, TOP_K * tm, D_MODEL), F32), pltpu.SemaphoreType.DMA((2,))],
        compiler_params=pltpu.CompilerParams(
            dimension_semantics=("arbitrary",), vmem_limit_bytes=VMEM_LIMIT),
        name="combine",
    )(dest, gate_tk, yb)


def _final_kernel(h1_ref, h1b_ref, routed_ref, ple_ref, wsg_ref, wsu_ref, wsd_ref, g2_ref, b2_ref, y_ref):
    hb = h1b_ref[...]
    g = jnp.dot(hb, wsg_ref[...], preferred_element_type=F32)
    u = jnp.dot(hb, wsu_ref[...], preferred_element_type=F32)
    shared = jnp.dot((g * jax.nn.sigmoid(g) * u).astype(BF16), wsd_ref[...], preferred_element_type=F32)
    r = DEEPNORM_ALPHA * h1_ref[...] + (routed_ref[...] + shared) + ple_ref[...]
    y_ref[...] = _layer_norm(r, g2_ref[...], b2_ref[...])


def _final(h1, h1b, routed, ple, ws_gate, ws_up, ws_down, g2, b2, *, row0, n_rows, tm):
    assert row0 % tm == 0 and n_rows % tm == 0
    blk0 = row0 // tm
    const = lambda i: (0, 0)
    rows = lambda i: (blk0 + i, 0)
    resident = functools.partial(pl.BlockSpec, index_map=const, pipeline_mode=pl.Buffered(1))
    return pl.pallas_call(
        _final_kernel,
        out_shape=jax.ShapeDtypeStruct((n_rows, D_MODEL), F32),
        grid=(n_rows // tm,),
        in_specs=[
            pl.BlockSpec((tm, D_MODEL), rows),
            pl.BlockSpec((tm, D_MODEL), rows),
            pl.BlockSpec((tm, D_MODEL), rows),
            pl.BlockSpec((tm, D_MODEL), rows),
            resident((D_MODEL, D_SHARED)),
            resident((D_MODEL, D_SHARED)),
            resident((D_SHARED, D_MODEL)),
            pl.BlockSpec((1, D_MODEL), const),
            pl.BlockSpec((1, D_MODEL), const),
        ],
        out_specs=pl.BlockSpec((tm, D_MODEL), lambda i: (i, 0)),
        compiler_params=pltpu.CompilerParams(
            dimension_semantics=("arbitrary",), vmem_limit_bytes=VMEM_LIMIT),
        name="final",
    )(h1, h1b, routed, ple, ws_gate, ws_up, ws_down, g2, b2)


def _rope_tables(pos):
    half = ROT_DIM // 2
    inv_freq = ROPE_THETA ** (-jnp.arange(half, dtype=F32) * 2.0 / ROT_DIM)
    ang = pos.astype(F32)[:, None] * inv_freq[None, :]
    cos, sin = jnp.cos(ang), jnp.sin(ang)
    n = pos.shape[0]
    ones = jnp.ones((n, HEAD_DIM - ROT_DIM), F32)
    zeros = jnp.zeros((n, HEAD_DIM - ROT_DIM), F32)
    zh = jnp.zeros((n, half), F32)
    cos_t = jnp.concatenate([cos, cos, ones], -1)
    sin_lo = jnp.concatenate([-sin, zh, zeros], -1)
    sin_hi = jnp.concatenate([zh, sin, zeros], -1)
    per_head = jnp.stack([cos_t, sin_lo, sin_hi])
    return jnp.concatenate([per_head] * (LANES // HEAD_DIM), -1)


def _block_plan(counts, n_blocks):
    bm = EXPERT_ROWS
    i32 = jnp.int32
    padded = (counts + bm - 1) // bm * bm
    pad_end = jnp.cumsum(padded)
    pad_start = pad_end - padded
    n_used = pad_end[-1] // bm
    blk = jnp.arange(n_blocks)
    valid = blk < n_used
    block_e = jnp.minimum(jnp.sum(pad_end[None, :] <= (blk * bm)[:, None], axis=1), N_EXPERTS - 1)
    block_e = jnp.where(valid, block_e, block_e[jnp.maximum(n_used - 1, 0)])
    prev_e = jnp.concatenate([jnp.full((1,), -1, block_e.dtype), block_e[:-1]])
    first = valid & (block_e != prev_e)
    run = jnp.cumsum(first) - 1
    ns = WEIGHT_SLOTS
    used = counts > 0
    n_run_slots = N_EXPERTS + ns
    run_expert = jnp.full((n_run_slots + 1,), -1, i32).at[
        jnp.where(used, jnp.cumsum(used) - 1, n_run_slots)].set(jnp.arange(N_EXPERTS, dtype=i32))
    run_expert = run_expert[:n_run_slots]
    ahead = jnp.where(first, run_expert[run + ns - 1], -1)
    head = run_expert[:ns - 1]
    return (pad_start.astype(i32), (pad_start + counts).astype(i32), (padded - counts).astype(i32),
            n_used.astype(i32).reshape(1), block_e.astype(i32), first.astype(i32), (run % ns).astype(i32),
            ahead.astype(i32), head.astype(i32))


def kernel(x_prompt, x_sample, p_prompt, p_sample, cache_k, cache_v, state_conv, ln_in_g, ln_in_b, w_in,
           attn_sinks, conv_w, gn_attn, gn_conv, w_out, ln1_g, ln1_b, w_router, router_bias, w_gate, w_up,
           w_down, ws_gate, ws_up, ws_down, w_ple_gate, w_ple, ln2_g, ln2_b):
    n_batch, seq, d = x_prompt.shape
    n_dec, dec_seq, _ = x_sample.shape
    depth = w_in.shape[0]
    wb = cache_k.shape[2]
    assert depth == 1 and dec_seq == 1 and d == D_MODEL
    assert wb == ATTN_BLOCK and seq % (ATTN_BLOCK * MIXER_BLOCKS_PER_STEP) == 0 and n_dec % SAMPLE_CHUNK == 0
    t_p = n_batch * seq
    t = t_p + n_dec
    assert t % ROW_TILE == 0 and t_p % SAMPLE_CHUNK == 0

    x_all = jnp.concatenate([x_prompt.reshape(t_p, d), x_sample.reshape(n_dec, d)], 0)
    p_all = jnp.concatenate([p_prompt[0].reshape(t_p, PLE_DIM), p_sample[0].reshape(n_dec, PLE_DIM)], 0).astype(BF16)
    row = lambda v: v.reshape(1, -1).astype(F32)

    w_in0 = w_in[0]
    qkv_w = ATTN_WIDTH + 2 * KV_WIDTH
    w_in_perm = jnp.concatenate([w_in0[:, qkv_w:], w_in0[:, :qkv_w]], axis=1).astype(BF16)

    proj = _ln_inproj(x_all, row(ln_in_g), row(ln_in_b), w_in_perm)

    rope_p = _rope_tables(jnp.arange(seq))
    rope_s = _rope_tables(PAST_LEN + jnp.arange(dec_seq))[:, 0, :]
    sinks = attn_sinks[0].astype(F32)
    cw, ga, gc = conv_w[0].astype(F32), row(gn_attn[0]), row(gn_conv[0])
    mixed_p, kwin_p, vwin_p, cstate_p = _prompt_mixer(proj, sinks, rope_p, cw, ga, gc, n_batch, seq)
    mixed_s, kwin_s, vwin_s, cstate_s = _sample_mixer(
        proj, t_p, n_dec, sinks, cache_k[0].reshape(n_dec, wb, KV_WIDTH), cache_v[0].reshape(n_dec, wb, KV_WIDTH),
        jnp.swapaxes(state_conv[0], 0, 1), rope_s, cw, ga, gc)
    cstate_s = jnp.swapaxes(cstate_s, 0, 1)
    mixed = jnp.concatenate([mixed_p, mixed_s], 0)

    h1, h1b = _outproj(
        mixed, x_all, row(ln_in_g), row(ln_in_b), w_out[0].astype(BF16), row(ln1_g[0]), row(ln1_b[0]))

    top_e, gate, rank, cnt, ple = _route(
        h1b, w_router[0].T.astype(BF16), router_bias[0].astype(F32).reshape(N_EXPERTS, 1), p_all,
        w_ple_gate[0].astype(BF16), w_ple[0].astype(BF16))
    n_blocks = t * TOP_K // EXPERT_ROWS + N_EXPERTS
    pad_start, fill_start, fill_len, n_used, block_e, first, slot, ahead, head = _block_plan(
        cnt[:, 0].astype(jnp.int32), n_blocks)
    dest = _tile_major(_dest_rows(pad_start, top_e, rank), LANES)
    xs = _dispatch_rows(fill_start, fill_len, n_used, dest, h1, n_blocks)
    yb = _experts(block_e, first, slot, ahead, head, n_used, xs, w_gate[0], w_up[0], w_down[0])
    routed = _combine(dest, gate.T, yb)

    final_args = (h1, h1b, routed, ple, ws_gate[0].astype(BF16), ws_up[0].astype(BF16), ws_down[0].astype(BF16),
                  row(ln2_g[0]), row(ln2_b[0]))
    y_p = _final(*final_args, row0=0, n_rows=t_p, tm=FINAL_TILE)
    y_s = _final(*final_args, row0=t_p, n_rows=n_dec, tm=n_dec)

    kv_shape = (1, -1, wb, N_KV_HEADS, HEAD_DIM)
    return (y_p.reshape(n_batch, seq, d), y_s.reshape(n_dec, dec_seq, d),
            kwin_p.reshape(kv_shape), vwin_p.reshape(kv_shape), cstate_p[None],
            kwin_s.reshape(kv_shape), vwin_s.reshape(kv_shape), cstate_s[None])
```

```python
import functools

import numpy as np
import jax
import jax.numpy as jnp
from jax import lax
from jax.experimental import pallas as pl
from jax.experimental.pallas import tpu as pltpu

D_MODEL = 2048
N_HEADS = 16
N_KV_HEADS = 4
HEAD_DIM = 64
GQA_GROUP = N_HEADS // N_KV_HEADS
ATTN_WIDTH = N_HEADS * HEAD_DIM
KV_WIDTH = N_KV_HEADS * HEAD_DIM
WINDOW = 128
ATTN_BLOCK = WINDOW
ROT_DIM = HEAD_DIM // 4
ROPE_THETA = 500000.0
CONV_WIDTH = D_MODEL - ATTN_WIDTH
CONV_K = 3
IN_COLS = ATTN_WIDTH + 2 * KV_WIDTH + 3 * CONV_WIDTH
N_EXPERTS = 256
TOP_K = 8
N_EXPERT_GROUPS = 8
TOPK_GROUPS = 4
D_EXPERT = 512
D_SHARED = 512
ROUTED_SCALE = 2.5
PLE_DIM = 256
LN_EPS = 1e-5
DEPTH = 1
PAST_LEN = 8192
DEEPNORM_ALPHA = (2 * DEPTH) ** 0.25

LANES = 128
SUBLANES = 8
COL_B, COL_C, COL_X, COL_Q, COL_KV = 0, CONV_WIDTH, 2 * CONV_WIDTH, 3 * CONV_WIDTH, 3 * CONV_WIDTH + ATTN_WIDTH

ROW_TILE = 640
INPROJ_COL_TILE = 1536
FINAL_TILE = 256
EXPERT_ROWS = 128
WEIGHT_SLOTS = 3
SAMPLE_CHUNK = 8
MIXER_BLOCKS_PER_STEP = 2
VMEM_LIMIT = 56 * 1024 * 1024

BF16 = jnp.bfloat16
F32 = jnp.float32


def _layer_norm(x, g, b):
    xc = x - jnp.mean(x, -1, keepdims=True)
    var = jnp.mean(xc * xc, -1, keepdims=True)
    return xc * lax.rsqrt(var + LN_EPS) * g + b


def _rms_norm(x, g):
    return x * lax.rsqrt(jnp.mean(x * x, -1, keepdims=True) + LN_EPS) * g


def _rope_lanes(x, cos_t, sin_lo, sin_hi):
    half = ROT_DIM // 2
    ax = x.ndim - 1
    return x * cos_t + pltpu.roll(x, LANES - half, ax) * sin_lo + pltpu.roll(x, half, ax) * sin_hi


def _rope_wide(x, cos_t, sin_lo, sin_hi):
    n = x.shape[-1] // LANES
    return jnp.concatenate(
        [_rope_lanes(x[..., c * LANES:(c + 1) * LANES], cos_t, sin_lo, sin_hi) for c in range(n)], axis=-1)


def _ln_inproj_kernel(x_ref, g_ref, b_ref, w_ref, o_ref, h_scr):
    @pl.when(pl.program_id(1) == 0)
    def _():
        h_scr[...] = _layer_norm(x_ref[...], g_ref[...], b_ref[...]).astype(BF16)

    o_ref[...] = jnp.dot(h_scr[...], w_ref[...], preferred_element_type=F32)


def _ln_inproj(x_all, g, b, w_bf16):
    t = x_all.shape[0]
    return pl.pallas_call(
        _ln_inproj_kernel,
        out_shape=jax.ShapeDtypeStruct((t, IN_COLS), F32),
        grid=(t // ROW_TILE, IN_COLS // INPROJ_COL_TILE),
        in_specs=[
            pl.BlockSpec((ROW_TILE, D_MODEL), lambda i, j: (i, 0)),
            pl.BlockSpec((1, D_MODEL), lambda i, j: (0, 0)),
            pl.BlockSpec((1, D_MODEL), lambda i, j: (0, 0)),
            pl.BlockSpec((D_MODEL, INPROJ_COL_TILE), lambda i, j: (0, j)),
        ],
        out_specs=pl.BlockSpec((ROW_TILE, INPROJ_COL_TILE), lambda i, j: (i, j)),
        scratch_shapes=[pltpu.VMEM((ROW_TILE, D_MODEL), BF16)],
        compiler_params=pltpu.CompilerParams(
            dimension_semantics=("arbitrary", "arbitrary"), vmem_limit_bytes=VMEM_LIMIT),
        name="ln_inproj",
    )(x_all, g, b, w_bf16)


def _prompt_mixer_kernel(sink_ref, b_ref, c_ref, x_ref, q_ref, kv_ref, kvp_ref, cp_ref, xp_ref,
                         rope_ref, ropep_ref, convw_ref, gna_ref, gnc_ref,
                         mixed_ref, kwin_ref, vwin_ref, cstate_ref, *, n_steps):
    i = pl.program_id(1)
    blk = ATTN_BLOCK
    rows = q_ref.shape[0]
    cos_t, sin_lo, sin_hi = rope_ref[0], rope_ref[1], rope_ref[2]
    qr = _rope_wide(q_ref[...], cos_t, sin_lo, sin_hi) * (HEAD_DIM ** -0.5)
    k_cur = _rope_wide(kv_ref[:, :KV_WIDTH], cos_t, sin_lo, sin_hi)
    k_prev = _rope_wide(kvp_ref[:, :KV_WIDTH], ropep_ref[0], ropep_ref[1], ropep_ref[2])
    v_cur = kv_ref[:, KV_WIDTH:]
    v_prev = kvp_ref[:, KV_WIDTH:]
    k_ext = jnp.concatenate([k_prev, k_cur], axis=0).astype(BF16)
    v_ext = jnp.concatenate([v_prev, v_cur], axis=0).astype(BF16)

    qi = lax.broadcasted_iota(jnp.int32, (blk, 2 * blk), 0)
    sj = lax.broadcasted_iota(jnp.int32, (blk, 2 * blk), 1)
    rel = qi + blk - sj
    local = (rel >= 0) & (rel <= WINDOW)
    for sb in range(rows // blk):
        mask = local if sb > 0 else local & ((sj >= blk) | (i > 0))
        mask = jnp.concatenate([mask] * GQA_GROUP, axis=0)
        q_sb = qr[sb * blk:(sb + 1) * blk]
        k_all = k_ext[sb * blk:(sb + 2) * blk]
        v_all = v_ext[sb * blk:(sb + 2) * blk]
        heads = []
        for g in range(N_KV_HEADS):
            qg = jnp.concatenate(
                [q_sb[:, (g * GQA_GROUP + hh) * HEAD_DIM:(g * GQA_GROUP + hh + 1) * HEAD_DIM]
                 for hh in range(GQA_GROUP)], axis=0).astype(BF16)
            kg = k_all[:, g * HEAD_DIM:(g + 1) * HEAD_DIM]
            vg = v_all[:, g * HEAD_DIM:(g + 1) * HEAD_DIM]
            s = lax.dot_general(qg, kg, (((1,), (1,)), ((), ())), preferred_element_type=F32)
            s = jnp.where(mask, s, -jnp.inf)
            sink = jnp.concatenate(
                [jnp.full((blk, 1), sink_ref[g * GQA_GROUP + hh], F32) for hh in range(GQA_GROUP)], axis=0)
            m = jnp.maximum(jnp.max(s, -1, keepdims=True), sink)
            e = jnp.exp(s - m)
            p = e / (jnp.sum(e, -1, keepdims=True) + jnp.exp(sink - m))
            og = jnp.dot(p.astype(BF16), vg, preferred_element_type=F32)
            heads += [og[hh * blk:(hh + 1) * blk] for hh in range(GQA_GROUP)]
        attn = jnp.concatenate(heads, axis=-1)
        mixed_ref[sb * blk:(sb + 1) * blk, :ATTN_WIDTH] = _rms_norm(attn, gna_ref[...]).astype(BF16)

    u = c_ref[...] * x_ref[...]
    n_tail = CONV_K - 1
    tail_rows = cp_ref.shape[0]
    row_id = lax.broadcasted_iota(jnp.int32, u.shape, 0)
    acc = u * convw_ref[CONV_K - 1:CONV_K, :]
    for d in range(1, CONV_K):
        ud = pltpu.roll(u, d, 0)
        for r in range(d):
            src = tail_rows - d + r
            up = jnp.where(i > 0, cp_ref[src:src + 1, :] * xp_ref[src:src + 1, :], 0.0)
            ud = jnp.where(row_id == r, up, ud)
        acc = acc + ud * convw_ref[CONV_K - 1 - d:CONV_K - d, :]
    mixed_ref[:, ATTN_WIDTH:] = _rms_norm(b_ref[...] * acc, gnc_ref[...]).astype(BF16)

    @pl.when(i == n_steps - 1)
    def _():
        kwin_ref[0] = k_cur[rows - blk:, :]
        vwin_ref[0] = v_cur[rows - blk:, :]
        cstate_ref[0] = u[rows - n_tail:, :]


def _prompt_mixer(proj, sinks, rope_tab, conv_w, gn_attn, gn_conv, n_batch, seq):
    blk = ATTN_BLOCK
    qb = MIXER_BLOCKS_PER_STEP
    rows = qb * blk
    nb = seq // rows
    tail = SUBLANES
    wide = CONV_WIDTH
    kvb = COL_KV // (2 * KV_WIDTH)

    def row(b, i, s):
        return b * nb + i

    def prev_blk(b, i, s):
        return jnp.maximum((b * nb + i) * qb - 1, b * nb * qb)

    def prev_tail(b, i, s):
        return jnp.maximum((b * nb + i) * (rows // tail) - 1, 0)

    grid_spec = pltpu.PrefetchScalarGridSpec(
        num_scalar_prefetch=1,
        grid=(n_batch, nb),
        in_specs=[
            pl.BlockSpec((rows, wide), lambda b, i, s: (row(b, i, s), COL_B // wide)),
            pl.BlockSpec((rows, wide), lambda b, i, s: (row(b, i, s), COL_C // wide)),
            pl.BlockSpec((rows, wide), lambda b, i, s: (row(b, i, s), COL_X // wide)),
            pl.BlockSpec((rows, ATTN_WIDTH), lambda b, i, s: (row(b, i, s), COL_Q // ATTN_WIDTH)),
            pl.BlockSpec((rows, 2 * KV_WIDTH), lambda b, i, s: (row(b, i, s), kvb)),
            pl.BlockSpec((blk, 2 * KV_WIDTH), lambda b, i, s: (prev_blk(b, i, s), kvb)),
            pl.BlockSpec((tail, wide), lambda b, i, s: (prev_tail(b, i, s), COL_C // wide)),
            pl.BlockSpec((tail, wide), lambda b, i, s: (prev_tail(b, i, s), COL_X // wide)),
            pl.BlockSpec((3, rows, LANES), lambda b, i, s: (0, i, 0)),
            pl.BlockSpec((3, blk, LANES), lambda b, i, s: (0, jnp.maximum(i * qb - 1, 0), 0)),
            pl.BlockSpec((CONV_K, CONV_WIDTH), lambda b, i, s: (0, 0)),
            pl.BlockSpec((1, ATTN_WIDTH), lambda b, i, s: (0, 0)),
            pl.BlockSpec((1, CONV_WIDTH), lambda b, i, s: (0, 0)),
        ],
        out_specs=[
            pl.BlockSpec((rows, D_MODEL), lambda b, i, s: (row(b, i, s), 0)),
            pl.BlockSpec((1, blk, KV_WIDTH), lambda b, i, s: (b, 0, 0)),
            pl.BlockSpec((1, blk, KV_WIDTH), lambda b, i, s: (b, 0, 0)),
            pl.BlockSpec((1, CONV_K - 1, CONV_WIDTH), lambda b, i, s: (b, 0, 0)),
        ],
    )
    return pl.pallas_call(
        functools.partial(_prompt_mixer_kernel, n_steps=nb),
        out_shape=[
            jax.ShapeDtypeStruct((n_batch * seq, D_MODEL), BF16),
            jax.ShapeDtypeStruct((n_batch, blk, KV_WIDTH), F32),
            jax.ShapeDtypeStruct((n_batch, blk, KV_WIDTH), F32),
            jax.ShapeDtypeStruct((n_batch, CONV_K - 1, CONV_WIDTH), F32),
        ],
        grid_spec=grid_spec,
        compiler_params=pltpu.CompilerParams(
            dimension_semantics=("arbitrary", "arbitrary"), vmem_limit_bytes=VMEM_LIMIT),
        name="prompt_mixer",
    )(sinks, proj, proj, proj, proj, proj, proj, proj, proj, rope_tab, rope_tab, conv_w, gn_attn, gn_conv)


def _expand_groups(t):
    ax = t.ndim - 1
    lane = lax.broadcasted_iota(jnp.int32, t.shape[:-1] + (LANES,), ax)
    chunks = []
    for c in range(KV_WIDTH // LANES):
        a = t[..., c * LANES:(c + 1) * LANES]
        r = pltpu.roll(a, HEAD_DIM, ax)
        lo = jnp.where(lane < HEAD_DIM, a, r)
        hi = jnp.where(lane < HEAD_DIM, r, a)
        chunks += [lo] * (GQA_GROUP // 2) + [hi] * (GQA_GROUP // 2)
    return jnp.concatenate(chunks, axis=-1)


def _sample_mixer_kernel(sink_ref, b_ref, c_ref, x_ref, q_ref, kv_ref, ck_ref, cv_ref, st_ref,
                         rope_ref, seg_ref, segt_ref, convw_ref, gna_ref, gnc_ref,
                         mixed_ref, kwin_ref, vwin_ref, cstate_ref):
    nb, wb = ck_ref.shape[0], ck_ref.shape[1]
    cos_t, sin_lo, sin_hi = rope_ref[0:1, :], rope_ref[1:2, :], rope_ref[2:3, :]
    qr = _rope_wide(q_ref[...], cos_t, sin_lo, sin_hi) * (HEAD_DIM ** -0.5)
    k_new = _rope_wide(kv_ref[:, :KV_WIDTH], cos_t, sin_lo, sin_hi)
    v_new = kv_ref[:, KV_WIDTH:]
    ck = ck_ref[...]
    cv = cv_ref[...]

    seg = seg_ref[...]
    prod = _expand_groups(ck) * qr[:, None, :]
    s_old = jnp.dot(prod.reshape(nb * wb, ATTN_WIDTH).astype(BF16), seg,
                    preferred_element_type=F32).reshape(nb, wb, N_HEADS)
    s_new = jnp.dot((_expand_groups(k_new) * qr).astype(BF16), seg, preferred_element_type=F32)
    sink = sink_ref[...]
    m = jnp.maximum(jnp.maximum(jnp.max(s_old, axis=1), s_new), sink)
    e_old = jnp.exp(s_old - m[:, None, :])
    e_new = jnp.exp(s_new - m)
    inv = 1.0 / (jnp.sum(e_old, axis=1) + e_new + jnp.exp(sink - m))
    p_old = (e_old * inv[:, None, :]).astype(BF16)
    p_new = (e_new * inv).astype(BF16)
    segt = segt_ref[...]
    pe_old = jnp.dot(p_old.reshape(nb * wb, N_HEADS), segt, preferred_element_type=F32).reshape(nb, wb, ATTN_WIDTH)
    pe_new = jnp.dot(p_new, segt, preferred_element_type=F32)
    attn = jnp.sum(pe_old * _expand_groups(cv), axis=1) + pe_new * _expand_groups(v_new)
    mixed_ref[:, :ATTN_WIDTH] = _rms_norm(attn, gna_ref[...]).astype(BF16)

    u = c_ref[...] * x_ref[...]
    acc = u * convw_ref[CONV_K - 1:CONV_K, :]
    for j in range(CONV_K - 1):
        acc = acc + st_ref[j] * convw_ref[j:j + 1, :]
    mixed_ref[:, ATTN_WIDTH:] = _rms_norm(b_ref[...] * acc, gnc_ref[...]).astype(BF16)

    kwin_ref[:, 0:wb - 1, :] = ck_ref[:, 1:wb, :]
    vwin_ref[:, 0:wb - 1, :] = cv_ref[:, 1:wb, :]
    for n in range(nb):
        kwin_ref[n, wb - 1:wb, :] = k_new[n:n + 1, :]
        vwin_ref[n, wb - 1:wb, :] = v_new[n:n + 1, :]
    for j in range(CONV_K - 2):
        cstate_ref[j] = st_ref[j + 1]
    cstate_ref[CONV_K - 2] = u


def _sample_mixer(proj, row0, n_dec, sinks, cache_k, cache_v, state_conv, rope_row, conv_w, gn_attn, gn_conv):
    nb = SAMPLE_CHUNK
    wb = cache_k.shape[1]
    r0 = row0 // nb
    wide = CONV_WIDTH
    head_of_lane = np.arange(ATTN_WIDTH) // HEAD_DIM
    seg = jnp.asarray(head_of_lane[:, None] == np.arange(N_HEADS)[None, :], BF16)
    kvb = COL_KV // (2 * KV_WIDTH)
    return pl.pallas_call(
        _sample_mixer_kernel,
        out_shape=[
            jax.ShapeDtypeStruct((n_dec, D_MODEL), BF16),
            jax.ShapeDtypeStruct((n_dec, wb, KV_WIDTH), F32),
            jax.ShapeDtypeStruct((n_dec, wb, KV_WIDTH), F32),
            jax.ShapeDtypeStruct((CONV_K - 1, n_dec, CONV_WIDTH), F32),
        ],
        grid=(n_dec // nb,),
        in_specs=[
            pl.BlockSpec((1, N_HEADS), lambda i: (0, 0)),
            pl.BlockSpec((nb, wide), lambda i: (r0 + i, COL_B // wide)),
            pl.BlockSpec((nb, wide), lambda i: (r0 + i, COL_C // wide)),
            pl.BlockSpec((nb, wide), lambda i: (r0 + i, COL_X // wide)),
            pl.BlockSpec((nb, ATTN_WIDTH), lambda i: (r0 + i, COL_Q // ATTN_WIDTH)),
            pl.BlockSpec((nb, 2 * KV_WIDTH), lambda i: (r0 + i, kvb)),
            pl.BlockSpec((nb, wb, KV_WIDTH), lambda i: (i, 0, 0)),
            pl.BlockSpec((nb, wb, KV_WIDTH), lambda i: (i, 0, 0)),
            pl.BlockSpec((CONV_K - 1, nb, CONV_WIDTH), lambda i: (0, i, 0)),
            pl.BlockSpec((3, LANES), lambda i: (0, 0)),
            pl.BlockSpec((ATTN_WIDTH, N_HEADS), lambda i: (0, 0)),
            pl.BlockSpec((N_HEADS, ATTN_WIDTH), lambda i: (0, 0)),
            pl.BlockSpec((CONV_K, CONV_WIDTH), lambda i: (0, 0)),
            pl.BlockSpec((1, ATTN_WIDTH), lambda i: (0, 0)),
            pl.BlockSpec((1, CONV_WIDTH), lambda i: (0, 0)),
        ],
        out_specs=[
            pl.BlockSpec((nb, D_MODEL), lambda i: (i, 0)),
            pl.BlockSpec((nb, wb, KV_WIDTH), lambda i: (i, 0, 0)),
            pl.BlockSpec((nb, wb, KV_WIDTH), lambda i: (i, 0, 0)),
            pl.BlockSpec((CONV_K - 1, nb, CONV_WIDTH), lambda i: (0, i, 0)),
        ],
        compiler_params=pltpu.CompilerParams(
            dimension_semantics=("arbitrary",), vmem_limit_bytes=VMEM_LIMIT),
        name="sample_mixer",
    )(sinks.reshape(1, N_HEADS), proj, proj, proj, proj, proj, cache_k, cache_v, state_conv,
      rope_row, seg, seg.T, conv_w, gn_attn, gn_conv)


def _outproj_kernel(mixed_ref, x_ref, g0_ref, b0_ref, wout_ref, g1_ref, b1_ref, h1_ref, h1b_ref):
    h = _layer_norm(x_ref[...], g0_ref[...], b0_ref[...])
    mixed = jnp.dot(mixed_ref[...], wout_ref[...], preferred_element_type=F32)
    h1 = _layer_norm(DEEPNORM_ALPHA * h + mixed, g1_ref[...], b1_ref[...])
    h1_ref[...] = h1
    h1b_ref[...] = h1.astype(BF16)


def _outproj(mixed, x_all, g0, b0, w_out, g1, b1):
    t = x_all.shape[0]
    tm = ROW_TILE // 2
    const = lambda i: (0, 0)
    resident = functools.partial(pl.BlockSpec, index_map=const, pipeline_mode=pl.Buffered(1))
    return pl.pallas_call(
        _outproj_kernel,
        out_shape=[
            jax.ShapeDtypeStruct((t, D_MODEL), F32),
            jax.ShapeDtypeStruct((t, D_MODEL), BF16),
        ],
        grid=(t // tm,),
        in_specs=[
            pl.BlockSpec((tm, D_MODEL), lambda i: (i, 0)),
            pl.BlockSpec((tm, D_MODEL), lambda i: (i, 0)),
            pl.BlockSpec((1, D_MODEL), const),
            pl.BlockSpec((1, D_MODEL), const),
            resident((D_MODEL, D_MODEL)),
            pl.BlockSpec((1, D_MODEL), const),
            pl.BlockSpec((1, D_MODEL), const),
        ],
        out_specs=[
            pl.BlockSpec((tm, D_MODEL), lambda i: (i, 0)),
            pl.BlockSpec((tm, D_MODEL), lambda i: (i, 0)),
        ],
        compiler_params=pltpu.CompilerParams(
            dimension_semantics=("arbitrary",), vmem_limit_bytes=VMEM_LIMIT),
        name="outproj",
    )(mixed, x_all, g0, b0, w_out, g1, b1)


def _route_kernel(h_ref, wrt_ref, bias_ref, tri_ref, ones_ref, e_ref, gate_ref, rank_ref, cnt_ref):
    tm = h_ref.shape[0]
    neg = -jnp.inf
    big = float(N_EXPERTS)

    @pl.when(pl.program_id(0) == 0)
    def _():
        cnt_ref[...] = jnp.zeros_like(cnt_ref)

    logits = lax.dot_general(wrt_ref[...], h_ref[...], (((1,), (1,)), ((), ())), preferred_element_type=F32)
    scores = jax.nn.sigmoid(logits)
    biased = scores + bias_ref[...]

    per_group = N_EXPERTS // N_EXPERT_GROUPS
    rid = lax.broadcasted_iota(jnp.int32, (per_group, tm), 0).astype(F32)
    gscore = []
    for g in range(N_EXPERT_GROUPS):
        xg = biased[g * per_group:(g + 1) * per_group]
        m1 = jnp.max(xg, 0, keepdims=True)
        i1 = jnp.min(jnp.where(xg == m1, rid, big), 0, keepdims=True)
        m2 = jnp.max(jnp.where(rid == i1, neg, xg), 0, keepdims=True)
        gscore.append(m1 + m2)
    masked = []
    for g in range(N_EXPERT_GROUPS):
        beaten = jnp.zeros((1, tm), F32)
        for o in range(N_EXPERT_GROUPS):
            if o != g:
                wins = (gscore[o] > gscore[g]) | ((gscore[o] == gscore[g]) & (o < g))
                beaten = beaten + jnp.where(wins, 1.0, 0.0)
        masked.append(jnp.where(beaten < TOPK_GROUPS, biased[g * per_group:(g + 1) * per_group], neg))
    x = jnp.concatenate(masked, axis=0)

    eid = lax.broadcasted_iota(jnp.int32, (N_EXPERTS, tm), 0).astype(F32)
    sel = jnp.zeros((N_EXPERTS, tm), F32)
    picks, gates = [], []
    for _ in range(TOP_K):
        m = jnp.max(x, 0, keepdims=True)
        ik = jnp.min(jnp.where(x == m, eid, big), 0, keepdims=True)
        hit = eid == ik
        gates.append(jnp.sum(jnp.where(hit, scores, 0.0), 0, keepdims=True))
        picks.append(ik)
        x = jnp.where(hit, neg, x)
        sel = jnp.where(hit, 1.0, sel)

    selb = sel.astype(BF16)
    pos = cnt_ref[...] + jnp.dot(selb, tri_ref[...], preferred_element_type=F32)
    cnt_ref[...] = cnt_ref[...] + jnp.dot(selb, ones_ref[...], preferred_element_type=F32)

    gsum = gates[0]
    for k in range(1, TOP_K):
        gsum = gsum + gates[k]
    for k in range(TOP_K):
        e_ref[k:k + 1, :] = picks[k].astype(jnp.int32)
        gate_ref[k:k + 1, :] = gates[k] / gsum * ROUTED_SCALE
        rank_ref[k:k + 1, :] = jnp.sum(jnp.where(eid == picks[k], pos, 0.0), 0, keepdims=True).astype(jnp.int32)


def _route(h1b, w_router_t, bias_col):
    t = h1b.shape[0]
    tm = LANES
    tri = jnp.asarray(np.arange(tm)[:, None] < np.arange(tm)[None, :], BF16)
    ones = jnp.ones((tm, tm), BF16)
    const = lambda i: (0, 0)
    return pl.pallas_call(
        _route_kernel,
        out_shape=[
            jax.ShapeDtypeStruct((TOP_K, t), jnp.int32),
            jax.ShapeDtypeStruct((TOP_K, t), F32),
            jax.ShapeDtypeStruct((TOP_K, t), jnp.int32),
            jax.ShapeDtypeStruct((N_EXPERTS, tm), F32),
        ],
        grid=(t // tm,),
        in_specs=[
            pl.BlockSpec((tm, D_MODEL), lambda i: (i, 0)),
            pl.BlockSpec((N_EXPERTS, D_MODEL), const),
            pl.BlockSpec((N_EXPERTS, 1), const),
            pl.BlockSpec((tm, tm), const),
            pl.BlockSpec((tm, tm), const),
        ],
        out_specs=[
            pl.BlockSpec((TOP_K, tm), lambda i: (0, i)),
            pl.BlockSpec((TOP_K, tm), lambda i: (0, i)),
            pl.BlockSpec((TOP_K, tm), lambda i: (0, i)),
            pl.BlockSpec((N_EXPERTS, tm), const),
        ],
        compiler_params=pltpu.CompilerParams(
            dimension_semantics=("arbitrary",), vmem_limit_bytes=VMEM_LIMIT),
        name="route",
    )(h1b, w_router_t, bias_col, tri, ones)


def _dest_kernel(pstart_ref, e_ref, rank_ref, dest_ref):
    e = e_ref[...]

    def body(x, acc):
        return jnp.where(e == x, pstart_ref[x], acc)
    dest_ref[...] = lax.fori_loop(0, N_EXPERTS, body, jnp.zeros_like(e)) + rank_ref[...]


def _dest_rows(pad_start, top_e, rank):
    whole = pl.BlockSpec(top_e.shape, lambda i, ps: (0, 0))
    return pl.pallas_call(
        _dest_kernel,
        out_shape=jax.ShapeDtypeStruct(top_e.shape, jnp.int32),
        grid_spec=pltpu.PrefetchScalarGridSpec(num_scalar_prefetch=1, grid=(1,), in_specs=[whole, whole],
                                               out_specs=whole),
        name="dest_rows",
    )(pad_start, top_e, rank)


def _pow2_chunks(n_max):
    return [1 << b for b in reversed(range(int(n_max).bit_length()))]


def _dispatch_kernel(fill0_ref, filln_ref, nused_ref, dest_ref, h1_ref, h1b_ref, pp_ref, ps_ref, wpg_ref, wp_ref,
                     xs_hbm, ple_ref, zeros_vmem, row_sem, fill_sem, *, n_blocks, n_prompt_tiles):
    i = pl.program_id(0)
    tm = h1_ref.shape[0]
    bm = EXPERT_ROWS

    def fill_copies(e):
        n = filln_ref[e]
        start = fill0_ref[e]
        head = jnp.minimum((SUBLANES - start % SUBLANES) % SUBLANES, n)
        out = []
        for r in range(SUBLANES - 1):
            out.append((r < head, pltpu.make_async_copy(
                zeros_vmem.at[pl.ds(0, 1), :], xs_hbm.at[pl.ds(start + r, 1), :], fill_sem)))
        body0 = start + head
        m = n - head
        for c in _pow2_chunks(bm - 1):
            if c < SUBLANES:
                continue
            off = pl.multiple_of(body0 + (m // (2 * c)) * (2 * c), SUBLANES)
            out.append(((m // c) % 2 == 1, pltpu.make_async_copy(
                zeros_vmem.at[pl.ds(0, c), :], xs_hbm.at[pl.ds(off, c), :], fill_sem)))
        return out

    def tail_copy(j):
        return pltpu.make_async_copy(zeros_vmem, xs_hbm.at[pl.ds(pl.multiple_of(j * bm, bm), bm), :], fill_sem)

    @pl.when(i == 0)
    def _():
        zeros_vmem[...] = jnp.zeros_like(zeros_vmem)

        def start_fill(e, c):
            for pred, cp in fill_copies(e):
                @pl.when(pred)
                def _():
                    cp.start(priority=1)
            return c
        lax.fori_loop(0, N_EXPERTS, start_fill, 0)

        def start_tail(j, c):
            tail_copy(j).start(priority=1)
            return c
        lax.fori_loop(nused_ref[0], n_blocks, start_tail, 0)

    for t in range(tm):
        for k in range(TOP_K):
            pltpu.make_async_copy(h1_ref.at[pl.ds(t, 1), :], xs_hbm.at[pl.ds(dest_ref[k * tm + t], 1), :],
                                  row_sem).start(priority=k % 2)

    gate = jax.nn.sigmoid(jnp.dot(h1b_ref[...], wpg_ref[...], preferred_element_type=F32))
    p = jnp.where(i < n_prompt_tiles, pp_ref[...], ps_ref[...]).astype(BF16)
    ple_ref[...] = gate * jnp.dot(p, wp_ref[...], preferred_element_type=F32)

    for _ in range(TOP_K):
        pltpu.make_async_copy(h1_ref, xs_hbm.at[pl.ds(0, tm), :], row_sem).wait()

    @pl.when(i == pl.num_programs(0) - 1)
    def _():
        def wait_fill(e, c):
            for pred, cp in fill_copies(e):
                @pl.when(pred)
                def _():
                    cp.wait()
            return c
        lax.fori_loop(0, N_EXPERTS, wait_fill, 0)

        def wait_tail(j, c):
            tail_copy(j).wait()
            return c
        lax.fori_loop(nused_ref[0], n_blocks, wait_tail, 0)


def _tile_major(a, tm):
    k, t = a.shape
    return a.reshape(k, t // tm, tm).swapaxes(0, 1).reshape(-1)


def _dispatch_rows(fill_start, fill_len, n_used, dest, h1, h1b, p_prompt, p_sample, w_ple_gate, w_ple, n_blocks):
    t, width = h1.shape
    tm = LANES
    assert p_prompt.shape[0] % tm == 0 and p_sample.shape[0] % tm == 0 and p_prompt.shape[0] + p_sample.shape[0] == t
    n_pt, n_st = p_prompt.shape[0] // tm, p_sample.shape[0] // tm
    tile = lambda i, *_: (i, 0)
    const = lambda i, *_: (0, 0)
    resident = functools.partial(pl.BlockSpec, index_map=const, pipeline_mode=pl.Buffered(1))
    grid_spec = pltpu.PrefetchScalarGridSpec(
        num_scalar_prefetch=3,
        grid=(t // tm,),
        in_specs=[pl.BlockSpec((TOP_K * tm,), lambda i, *_: (i,), memory_space=pltpu.SMEM),
                  pl.BlockSpec((tm, width), tile),
                  pl.BlockSpec((tm, width), tile),
                  pl.BlockSpec((tm, PLE_DIM), lambda i, *_: (jnp.minimum(i, n_pt - 1), 0)),
                  pl.BlockSpec((tm, PLE_DIM), lambda i, *_: (jnp.clip(i - n_pt, 0, n_st - 1), 0)),
                  resident((D_MODEL, D_MODEL)),
                  resident((PLE_DIM, D_MODEL))],
        out_specs=[pl.BlockSpec(memory_space=pl.ANY), pl.BlockSpec((tm, D_MODEL), tile)],
        scratch_shapes=[pltpu.VMEM((EXPERT_ROWS, width), h1.dtype),
                        pltpu.SemaphoreType.DMA, pltpu.SemaphoreType.DMA],
    )
    return pl.pallas_call(
        functools.partial(_dispatch_kernel, n_blocks=n_blocks, n_prompt_tiles=n_pt),
        out_shape=[jax.ShapeDtypeStruct((n_blocks * EXPERT_ROWS, width), h1.dtype),
                   jax.ShapeDtypeStruct((t, D_MODEL), F32)],
        grid_spec=grid_spec,
        compiler_params=pltpu.CompilerParams(dimension_semantics=("arbitrary",), vmem_limit_bytes=VMEM_LIMIT),
        name="dispatch",
    )(fill_start, fill_len, n_used, dest, h1, h1b, p_prompt, p_sample, w_ple_gate, w_ple)


def _experts_kernel(be_ref, first_ref, slot_ref, ahead_ref, head_ref, nused_ref, x_ref, wg_hbm, wu_hbm, wd_hbm,
                    y_ref, wg_buf, wu_buf, wd_buf, wg_bf, wu_bf, wd_bf, sems):
    i = pl.program_id(0)
    n_slots = wg_buf.shape[0]

    def weight_copies(e, slot):
        return (pltpu.make_async_copy(wg_hbm.at[e], wg_buf.at[slot], sems.at[slot, 0]),
                pltpu.make_async_copy(wu_hbm.at[e], wu_buf.at[slot], sems.at[slot, 1]),
                pltpu.make_async_copy(wd_hbm.at[e], wd_buf.at[slot], sems.at[slot, 2]))

    @pl.when(i == 0)
    def _():
        for r in range(n_slots - 1):
            @pl.when(head_ref[r] >= 0)
            def _():
                for cp in weight_copies(head_ref[r], r):
                    cp.start(priority=1)

    @pl.when(first_ref[i] == 1)
    def _():
        slot = slot_ref[i]
        for cp in weight_copies(be_ref[i], slot):
            cp.wait()

        @pl.when(ahead_ref[i] >= 0)
        def _():
            for cp in weight_copies(ahead_ref[i], (slot + n_slots - 1) % n_slots):
                cp.start(priority=1)

        wg_bf[...] = wg_buf[slot].astype(BF16)
        wu_bf[...] = wu_buf[slot].astype(BF16)
        wd_bf[...] = wd_buf[slot].astype(BF16)

    @pl.when(i < nused_ref[0])
    def _():
        x = x_ref[...].astype(BF16)
        g = jnp.dot(x, wg_bf[...], preferred_element_type=F32)
        u = jnp.dot(x, wu_bf[...], preferred_element_type=F32)
        a = (g * jax.nn.sigmoid(g) * u).astype(BF16)
        y_ref[...] = jnp.dot(a, wd_bf[...], preferred_element_type=F32)

    @pl.when(i >= nused_ref[0])
    def _():
        y_ref[...] = jnp.zeros_like(y_ref)


def _experts(block_e, first, slot, ahead, head, n_used, xs, w_gate, w_up, w_down):
    p = xs.shape[0]
    bm = EXPERT_ROWS
    n_blocks = p // bm
    ns = WEIGHT_SLOTS

    def xrow(i, be, fi, sl, ah, hd, nu):
        return (jnp.minimum(i, nu[0] - 1), 0)

    grid_spec = pltpu.PrefetchScalarGridSpec(
        num_scalar_prefetch=6,
        grid=(n_blocks,),
        in_specs=[
            pl.BlockSpec((bm,) + xs.shape[1:], xrow),
            pl.BlockSpec(memory_space=pl.ANY),
            pl.BlockSpec(memory_space=pl.ANY),
            pl.BlockSpec(memory_space=pl.ANY),
        ],
        out_specs=pl.BlockSpec((bm, D_MODEL), lambda i, *_: (i, 0)),
        scratch_shapes=[
            pltpu.VMEM((ns, D_MODEL, D_EXPERT), F32),
            pltpu.VMEM((ns, D_MODEL, D_EXPERT), F32),
            pltpu.VMEM((ns, D_EXPERT, D_MODEL), F32),
            pltpu.VMEM((D_MODEL, D_EXPERT), BF16),
            pltpu.VMEM((D_MODEL, D_EXPERT), BF16),
            pltpu.VMEM((D_EXPERT, D_MODEL), BF16),
            pltpu.SemaphoreType.DMA((ns, 3)),
        ],
    )
    return pl.pallas_call(
        _experts_kernel,
        out_shape=jax.ShapeDtypeStruct((p, D_MODEL), F32),
        grid_spec=grid_spec,
        compiler_params=pltpu.CompilerParams(
            dimension_semantics=("arbitrary",), vmem_limit_bytes=VMEM_LIMIT),
        name="experts",
    )(block_e, first, slot, ahead, head, n_used, xs, w_gate, w_up, w_down)


def _combine_kernel(dest_ref, gate_ref, yb_hbm, o_ref, buf, sems):
    i = pl.program_id(0)
    n_tiles = pl.num_programs(0) - 1
    tm = o_ref.shape[0]
    n_rows = TOP_K * tm

    for s in range(2):
        @pl.when((i < n_tiles) & (i % 2 == s))
        def _():
            for idx in range(n_rows):
                pltpu.make_async_copy(yb_hbm.at[pl.ds(dest_ref[idx], 1), :], buf.at[s, pl.ds(idx, 1), :],
                                      sems.at[s]).start(priority=idx % 2)

    @pl.when(i >= 1)
    def _():
        slot = (i - 1) % 2
        pltpu.make_async_copy(yb_hbm.at[pl.ds(0, n_rows), :], buf.at[slot], sems.at[slot]).wait()
        acc = gate_ref[:, 0:1] * buf[slot, 0:tm, :]
        for k in range(1, TOP_K):
            acc = acc + gate_ref[:, k:k + 1] * buf[slot, k * tm:(k + 1) * tm, :]
        o_ref[...] = acc


def _combine(dest, gate_tk, yb):
    t = gate_tk.shape[0]
    tm = LANES
    n_tiles = t // tm
    return pl.pallas_call(
        _combine_kernel,
        out_shape=jax.ShapeDtypeStruct((t, D_MODEL), F32),
        grid=(n_tiles + 1,),
        in_specs=[
            pl.BlockSpec((TOP_K * tm,), lambda i: (jnp.minimum(i, n_tiles - 1),), memory_space=pltpu.SMEM),
            pl.BlockSpec((tm, TOP_K), lambda i: (jnp.maximum(i - 1, 0), 0)),
            pl.BlockSpec(memory_space=pl.ANY),
        ],
        out_specs=pl.BlockSpec((tm, D_MODEL), lambda i: (jnp.maximum(i - 1, 0), 0)),
        scratch_shapes=[pltpu.VMEM((2, TOP_K * tm, D_MODEL), F32), pltpu.SemaphoreType.DMA((2,))],
        compiler_params=pltpu.CompilerParams(
            dimension_semantics=("arbitrary",), vmem_limit_bytes=VMEM_LIMIT),
        name="combine",
    )(dest, gate_tk, yb)


def _final_kernel(h1_ref, h1b_ref, routed_ref, ple_ref, wsg_ref, wsu_ref, wsd_ref, g2_ref, b2_ref, y_ref):
    hb = h1b_ref[...]
    g = jnp.dot(hb, wsg_ref[...], preferred_element_type=F32)
    u = jnp.dot(hb, wsu_ref[...], preferred_element_type=F32)
    shared = jnp.dot((g * jax.nn.sigmoid(g) * u).astype(BF16), wsd_ref[...], preferred_element_type=F32)
    r = DEEPNORM_ALPHA * h1_ref[...] + (routed_ref[...] + shared) + ple_ref[...]
    y_ref[...] = _layer_norm(r, g2_ref[...], b2_ref[...])


def _final(h1, h1b, routed, ple, ws_gate, ws_up, ws_down, g2, b2, *, row0, n_rows, tm):
    assert row0 % tm == 0 and n_rows % tm == 0
    blk0 = row0 // tm
    const = lambda i: (0, 0)
    rows = lambda i: (blk0 + i, 0)
    resident = functools.partial(pl.BlockSpec, index_map=const, pipeline_mode=pl.Buffered(1))
    return pl.pallas_call(
        _final_kernel,
        out_shape=jax.ShapeDtypeStruct((n_rows, D_MODEL), F32),
        grid=(n_rows // tm,),
        in_specs=[
            pl.BlockSpec((tm, D_MODEL), rows),
            pl.BlockSpec((tm, D_MODEL), rows),
            pl.BlockSpec((tm, D_MODEL), rows),
            pl.BlockSpec((tm, D_MODEL), rows),
            resident((D_MODEL, D_SHARED)),
            resident((D_MODEL, D_SHARED)),
            resident((D_SHARED, D_MODEL)),
            pl.BlockSpec((1, D_MODEL), const),
            pl.BlockSpec((1, D_MODEL), const),
        ],
        out_specs=pl.BlockSpec((tm, D_MODEL), lambda i: (i, 0)),
        compiler_params=pltpu.CompilerParams(
            dimension_semantics=("arbitrary",), vmem_limit_bytes=VMEM_LIMIT),
        name="final",
    )(h1, h1b, routed, ple, ws_gate, ws_up, ws_down, g2, b2)


def _rope_tables(pos):
    half = ROT_DIM // 2
    inv_freq = ROPE_THETA ** (-jnp.arange(half, dtype=F32) * 2.0 / ROT_DIM)
    ang = pos.astype(F32)[:, None] * inv_freq[None, :]
    cos, sin = jnp.cos(ang), jnp.sin(ang)
    n = pos.shape[0]
    ones = jnp.ones((n, HEAD_DIM - ROT_DIM), F32)
    zeros = jnp.zeros((n, HEAD_DIM - ROT_DIM), F32)
    zh = jnp.zeros((n, half), F32)
    cos_t = jnp.concatenate([cos, cos, ones], -1)
    sin_lo = jnp.concatenate([-sin, zh, zeros], -1)
    sin_hi = jnp.concatenate([zh, sin, zeros], -1)
    per_head = jnp.stack([cos_t, sin_lo, sin_hi])
    return jnp.concatenate([per_head] * (LANES // HEAD_DIM), -1)


def _block_plan(counts, n_blocks):
    bm = EXPERT_ROWS
    i32 = jnp.int32
    padded = (counts + bm - 1) // bm * bm
    pad_end = jnp.cumsum(padded)
    pad_start = pad_end - padded
    n_used = pad_end[-1] // bm
    blk = jnp.arange(n_blocks)
    valid = blk < n_used
    block_e = jnp.minimum(jnp.sum(pad_end[None, :] <= (blk * bm)[:, None], axis=1), N_EXPERTS - 1)
    block_e = jnp.where(valid, block_e, block_e[jnp.maximum(n_used - 1, 0)])
    prev_e = jnp.concatenate([jnp.full((1,), -1, block_e.dtype), block_e[:-1]])
    first = valid & (block_e != prev_e)
    run = jnp.cumsum(first) - 1
    ns = WEIGHT_SLOTS
    used = counts > 0
    n_run_slots = N_EXPERTS + ns
    run_expert = jnp.full((n_run_slots + 1,), -1, i32).at[
        jnp.where(used, jnp.cumsum(used) - 1, n_run_slots)].set(jnp.arange(N_EXPERTS, dtype=i32))
    run_expert = run_expert[:n_run_slots]
    ahead = jnp.where(first, run_expert[run + ns - 1], -1)
    head = run_expert[:ns - 1]
    return (pad_start.astype(i32), (pad_start + counts).astype(i32), (padded - counts).astype(i32),
            n_used.astype(i32).reshape(1), block_e.astype(i32), first.astype(i32), (run % ns).astype(i32),
            ahead.astype(i32), head.astype(i32))


def kernel(x_prompt, x_sample, p_prompt, p_sample, cache_k, cache_v, state_conv, ln_in_g, ln_in_b, w_in,
           attn_sinks, conv_w, gn_attn, gn_conv, w_out, ln1_g, ln1_b, w_router, router_bias, w_gate, w_up,
           w_down, ws_gate, ws_up, ws_down, w_ple_gate, w_ple, ln2_g, ln2_b):
    n_batch, seq, d = x_prompt.shape
    n_dec, dec_seq, _ = x_sample.shape
    depth = w_in.shape[0]
    wb = cache_k.shape[2]
    assert depth == 1 and dec_seq == 1 and d == D_MODEL
    assert wb == ATTN_BLOCK and seq % (ATTN_BLOCK * MIXER_BLOCKS_PER_STEP) == 0 and n_dec % SAMPLE_CHUNK == 0
    t_p = n_batch * seq
    t = t_p + n_dec
    assert t % ROW_TILE == 0 and t_p % SAMPLE_CHUNK == 0

    x_all = jnp.concatenate([x_prompt.reshape(t_p, d), x_sample.reshape(n_dec, d)], 0)
    row = lambda v: v.reshape(1, -1).astype(F32)

    w_in0 = w_in[0]
    qkv_w = ATTN_WIDTH + 2 * KV_WIDTH
    w_in_perm = jnp.concatenate([w_in0[:, qkv_w:], w_in0[:, :qkv_w]], axis=1).astype(BF16)

    proj = _ln_inproj(x_all, row(ln_in_g), row(ln_in_b), w_in_perm)

    rope_p = _rope_tables(jnp.arange(seq))
    rope_s = _rope_tables(PAST_LEN + jnp.arange(dec_seq))[:, 0, :]
    sinks = attn_sinks[0].astype(F32)
    cw, ga, gc = conv_w[0].astype(F32), row(gn_attn[0]), row(gn_conv[0])
    mixed_p, kwin_p, vwin_p, cstate_p = _prompt_mixer(proj, sinks, rope_p, cw, ga, gc, n_batch, seq)
    mixed_s, kwin_s, vwin_s, cstate_s = _sample_mixer(
        proj, t_p, n_dec, sinks, cache_k[0].reshape(n_dec, wb, KV_WIDTH), cache_v[0].reshape(n_dec, wb, KV_WIDTH),
        jnp.swapaxes(state_conv[0], 0, 1), rope_s, cw, ga, gc)
    cstate_s = jnp.swapaxes(cstate_s, 0, 1)
    mixed = jnp.concatenate([mixed_p, mixed_s], 0)

    h1, h1b = _outproj(
        mixed, x_all, row(ln_in_g), row(ln_in_b), w_out[0].astype(BF16), row(ln1_g[0]), row(ln1_b[0]))

    top_e, gate, rank, cnt = _route(h1b, w_router[0].T.astype(BF16), router_bias[0].astype(F32).reshape(N_EXPERTS, 1))
    n_blocks = t * TOP_K // EXPERT_ROWS + N_EXPERTS
    pad_start, fill_start, fill_len, n_used, block_e, first, slot, ahead, head = _block_plan(
        cnt[:, 0].astype(jnp.int32), n_blocks)
    dest = _tile_major(_dest_rows(pad_start, top_e, rank), LANES)
    xs, ple = _dispatch_rows(fill_start, fill_len, n_used, dest, h1, h1b,
                             p_prompt[0].reshape(t_p, PLE_DIM), p_sample[0].reshape(n_dec, PLE_DIM),
                             w_ple_gate[0].astype(BF16), w_ple[0].astype(BF16), n_blocks)
    yb = _experts(block_e, first, slot, ahead, head, n_used, xs, w_gate[0], w_up[0], w_down[0])
    routed = _combine(dest, gate.T, yb)

    final_args = (h1, h1b, routed, ple, ws_gate[0].astype(BF16), ws_up[0].astype(BF16), ws_down[0].astype(BF16),
                  row(ln2_g[0]), row(ln2_b[0]))
    y_p = _final(*final_args, row0=0, n_rows=t_p, tm=FINAL_TILE)
    y_s = _final(*final_args, row0=t_p, n_rows=n_dec, tm=n_dec)

    kv_shape = (1, -1, wb, N_KV_HEADS, HEAD_DIM)
    return (y_p.reshape(n_batch, seq, d), y_s.reshape(n_dec, dec_seq, d),
            kwin_p.reshape(kv_shape), vwin_p.reshape(kv_shape), cstate_p[None],
            kwin_s.reshape(kv_shape), vwin_s.reshape(kv_shape), cstate_s[None])
```
